```python
import math
import jax
import jax.numpy as jnp
from jax import lax
import numpy as np

D_MODEL = 2048
BATCH = 2
SEQ = 8192
DEPTH = 4

GRID_W = 64
CTX_LEN = 256
N_MIXERS = 3
EPS = 1e-6

POOL_WINDOWS = (2, 4, 8, 16)
N_POOL_GROUPS = len(POOL_WINDOWS)
POOL_GROUP = D_MODEL // N_POOL_GROUPS

RWKV_HEAD = 64
RWKV_HEADS = D_MODEL // RWKV_HEAD
RWKV_DECAY_LORA = max(32, int(round(1.8 * D_MODEL ** 0.5 / 32)) * 32)
RWKV_AAA_LORA = max(32, int(round(1.8 * D_MODEL ** 0.5 / 32)) * 32)
RWKV_GATE_LORA = max(32, int(round(0.6 * D_MODEL ** 0.8 / 32)) * 32)
LN_X_EPS = 64e-5

DIFF_HEAD = 128
DIFF_HEADS = D_MODEL // (2 * DIFF_HEAD)
ROPE_BASE = 10000.0
Q_BLOCK = 128

FFN_HIDDEN = -(-(8 * D_MODEL) // (3 * 256)) * 256

f32 = jnp.float32

kernel_name = 'hybrid_pool_rwkv7_diffattn_dit_trunk'


def n_layers_of_kind(kind):
    return len(range(kind, DEPTH, N_MIXERS))


def rms_norm(x, g):
    xf = x.astype(f32)
    return (xf * lax.rsqrt(jnp.mean(xf * xf, axis=-1, keepdims=True) + EPS)).astype(x.dtype) * g


def modulate(x, shift, scale):
    return x * (1 + scale) + shift


def swiglu(h, w_in, w_out):
    gate, up = jnp.split(h @ w_in, 2, axis=-1)
    return (jax.nn.silu(gate) * up) @ w_out


def pool_mix(h, w_grp, ls):
    B, S, D = h.shape
    hf = h.astype(f32)
    cs = jnp.pad(jnp.cumsum(hf, axis=1), ((0, 0), (1, 0), (0, 0)))
    t = jnp.arange(S)
    groups = []
    for g, win in enumerate(POOL_WINDOWS):
        lo = jnp.clip(t - win // 2, 0, S)
        hi = jnp.clip(t + win - win // 2, 0, S)
        csg = cs[..., g * POOL_GROUP:(g + 1) * POOL_GROUP]
        mean = (csg[:, hi] - csg[:, lo]) / (hi - lo).astype(f32)[None, :, None]
        groups.append(mean - hf[..., g * POOL_GROUP:(g + 1) * POOL_GROUP])
    p = jnp.stack(groups, axis=2).astype(h.dtype)
    y = jnp.einsum('bsgc,gce->bsge', p, w_grp).reshape(B, S, D)
    return y * ls


def centred_shift(x):
    xp = jnp.pad(x, ((0, 0), (1, 1), (0, 0)))
    return 0.5 * (xp[:, :-2] + xp[:, 2:]) - x


def rwkv_stream(h, mu, w_rkv, dir_vec, w_la, w_lb, a_la, a_lb):
    B, S, D = h.shape
    heads = lambda t: t.reshape(B, S, RWKV_HEADS, RWKV_HEAD)
    xx = centred_shift(h)
    xr, xw, xk, xv, xa, xg = (h + xx * mu[m] for m in range(6))
    r = heads(xr @ w_rkv[0]).astype(f32)
    k = xk @ w_rkv[1]
    v = heads(xv @ w_rkv[2]).astype(f32)
    dirs = []
    for d in range(2):
        w0, a0, k_k, k_a = dir_vec[d]
        w_log = -jax.nn.softplus(-(w0 + jnp.tanh(xw @ w_la[d]) @ w_lb[d])) - 0.5
        decay = heads(jnp.exp(-jnp.exp(w_log.astype(f32))))
        a = jax.nn.sigmoid(a0 + (xa @ a_la[d]) @ a_lb[d])
        kk = heads(k * k_k).astype(f32)
        kk = kk / jnp.maximum(jnp.linalg.norm(kk, axis=-1, keepdims=True), 1e-12)
        k_d = heads(k * (1 + (a - 1) * k_a)).astype(f32)
        a_h = heads(a).astype(f32)
        dirs.append((decay, k_d, -kk, kk * a_h))
    return r, v, xg, dirs


def wkv7_scan(state0, r, v, decay, k, a, b, reverse):
    def step(state, inp):
        r_t, v_t, w_t, k_t, a_t, b_t = inp
        sa = jnp.einsum('bhvk,bhk->bhv', state, a_t)
        state = (state * w_t[:, :, None, :] + sa[..., None] * b_t[:, :, None, :]
                 + v_t[..., None] * k_t[:, :, None, :])
        return state, jnp.einsum('bhvk,bhk->bhv', state, r_t)
    xs = tuple(jnp.swapaxes(t, 0, 1) for t in (r, v, decay, k, a, b))
    state, ys = lax.scan(step, state0, xs, reverse=reverse)
    return state, jnp.swapaxes(ys, 0, 1)


def rwkv_output(ys, r, v, ks, xg, r_k, ln_x, g_la, g_lb, w_o):
    y = ys[0] + ys[1]
    B, S, H, N = y.shape
    mean = jnp.mean(y, axis=-1, keepdims=True)
    var = jnp.mean(jnp.square(y - mean), axis=-1, keepdims=True)
    yn = ((y - mean) * lax.rsqrt(var + LN_X_EPS)).reshape(B, S, H * N)
    rk = r_k.astype(f32)
    bonus = (jnp.sum(r * ks[0] * rk, axis=-1, keepdims=True)
             + jnp.sum(r * ks[1] * rk, axis=-1, keepdims=True)) * v
    o = yn * ln_x[0] + ln_x[1] + bonus.reshape(B, S, H * N)
    g = jax.nn.sigmoid(xg @ g_la) @ g_lb
    return (o.astype(xg.dtype) * g) @ w_o


def rwkv_mix(h_ctx, h_lat, mu, w_rkv, w_o, dir_vec, w_la, w_lb, a_la, a_lb, g_la, g_lb, r_k, ln_x, ctx_out):
    stream_args = (mu, w_rkv, dir_vec, w_la, w_lb, a_la, a_lb)
    rc, vc, xgc, dirs_c = rwkv_stream(h_ctx, *stream_args)
    rl, vl, xgl, dirs_l = rwkv_stream(h_lat, *stream_args)
    zero = jnp.zeros((h_lat.shape[0], RWKV_HEADS, RWKV_HEAD, RWKV_HEAD), f32)
    yc_dirs, yl_dirs = [], []
    for d, reverse in enumerate((False, True)):
        s_ctx, yc = wkv7_scan(zero, rc, vc, *dirs_c[d], reverse=reverse)
        _, yl = wkv7_scan(s_ctx, rl, vl, *dirs_l[d], reverse=reverse)
        yc_dirs.append(yc)
        yl_dirs.append(yl)
    out_args = (r_k, ln_x, g_la, g_lb, w_o)
    y_lat = rwkv_output(yl_dirs, rl, vl, [dl[1] for dl in dirs_l], xgl, *out_args)
    y_ctx = rwkv_output(yc_dirs, rc, vc, [dc[1] for dc in dirs_c], xgc, *out_args) if ctx_out else None
    return y_ctx, y_lat


def axial_rope(L):
    rows = L // GRID_W
    row = jnp.repeat(jnp.arange(rows, dtype=f32), GRID_W)
    col = jnp.tile(jnp.arange(GRID_W, dtype=f32), rows)
    n_freq = DIFF_HEAD // 4
    inv = ROPE_BASE ** (-jnp.arange(n_freq, dtype=f32) / n_freq)
    ang = jnp.concatenate([row[:, None] * inv, col[:, None] * inv], axis=-1)
    return jnp.cos(ang), jnp.sin(ang)


def apply_rope(x, cos, sin):
    xp = x.reshape(x.shape[:-1] + (DIFF_HEAD // 2, 2))
    x1, x2 = xp[..., 0], xp[..., 1]
    c, s = cos[:, None, None, :], sin[:, None, None, :]
    out = jnp.stack([x1 * c - x2 * s, x1 * s + x2 * c], axis=-1)
    return out.reshape(x.shape).astype(x.dtype)


def diff_attend(q, k, v, lam):
    s = jnp.einsum('bqhid,bkhid->bhiqk', q, k).astype(f32) * DIFF_HEAD ** -0.5
    p = jax.nn.softmax(s, axis=-1)
    p = p[:, :, 0] - lam * p[:, :, 1]
    return jnp.einsum('bhqk,bkhe->bqhe', p.astype(v.dtype), v)


def diff_mix(h_ctx, h_lat, w_qkv, w_o, qk_g, lam_vec, subln_g, lambda_init, ctx_out):
    B, L, D = h_lat.shape
    H, d = DIFF_HEADS, DIFF_HEAD
    lv = lam_vec.astype(f32)
    lam = jnp.exp(jnp.sum(lv[0] * lv[1])) - jnp.exp(jnp.sum(lv[2] * lv[3])) + lambda_init
    qk_heads = lambda t, g: rms_norm(t.reshape(t.shape[0], t.shape[1], H, 2, d), g)
    v_heads = lambda t: t.reshape(t.shape[0], t.shape[1], H, 2 * d)
    q_l, k_l, v_l = jnp.split(h_lat @ w_qkv, 3, axis=-1)
    cos, sin = axial_rope(L)
    q_l = apply_rope(qk_heads(q_l, qk_g[0]), cos, sin)
    k_l = apply_rope(qk_heads(k_l, qk_g[1]), cos, sin)
    k_c, v_c = jnp.split(h_ctx @ w_qkv[:, D_MODEL:], 2, axis=-1)
    k_c = qk_heads(k_c, qk_g[1])
    v_c = v_heads(v_c)
    k_all = jnp.concatenate([k_c, k_l], axis=1)
    v_all = jnp.concatenate([v_c, v_heads(v_l)], axis=1)
    nb = L // Q_BLOCK
    q_blocks = jnp.moveaxis(q_l.reshape(B, nb, Q_BLOCK, H, 2, d), 1, 0)
    o_blocks = lax.map(lambda qb: diff_attend(qb, k_all, v_all, lam), q_blocks)
    o_lat = jnp.moveaxis(o_blocks, 0, 1).reshape(B, L, H, 2 * d)

    def finish(o):
        o = rms_norm(o, subln_g) * (1.0 - lambda_init)
        return o.reshape(o.shape[0], o.shape[1], D_MODEL) @ w_o

    y_lat = finish(o_lat)
    y_ctx = None
    if ctx_out:
        q_c = qk_heads(h_ctx @ w_qkv[:, :D_MODEL], qk_g[0])
        y_ctx = finish(diff_attend(q_c, k_c, v_c, lam))
    return y_ctx, y_lat


def setup_inputs(seed: int = 0) -> dict:
    key = jax.random.key(seed)
    keys = iter(list(jax.random.split(key, 40)))

    def nrm(shape, scale):
        return jax.random.normal(next(keys), shape, f32) * scale

    def unif(shape, lo, hi):
        return jax.random.uniform(next(keys), shape, f32, lo, hi)

    D, F = D_MODEL, FFN_HIDDEN
    NP, NR, ND = n_layers_of_kind(0), n_layers_of_kind(1), n_layers_of_kind(2)
    H, N, dh = RWKV_HEADS, RWKV_HEAD, DIFF_HEAD
    inputs = {}
    inputs['x'] = nrm((BATCH, SEQ, D), 1.0)
    inputs['c'] = nrm((BATCH, D), 1.0)
    inputs['ctx'] = nrm((BATCH, CTX_LEN, D), 1.0)
    inputs['c_ctx'] = nrm((D,), 1.0)
    inputs['ada_w'] = nrm((DEPTH, D, 6 * D), 0.5 * D ** -0.5)
    inputs['ada_b'] = nrm((DEPTH, 6 * D), 0.02)
    inputs['norm_g'] = 1.0 + nrm((DEPTH, 2, D), 0.1)
    inputs['ffn_w_in'] = nrm((DEPTH, D, 2 * F), D ** -0.5)
    inputs['ffn_w_out'] = nrm((DEPTH, F, D), F ** -0.5)
    inputs['pool_w'] = nrm((NP, N_POOL_GROUPS, POOL_GROUP, POOL_GROUP), POOL_GROUP ** -0.5)
    inputs['pool_scale'] = 0.5 + nrm((NP, D), 0.1)
    inputs['rwkv_mu'] = unif((NR, 6, D), 0.0, 1.0)
    inputs['rwkv_w_rkv'] = nrm((NR, 3, D, D), D ** -0.5)
    inputs['rwkv_w_o'] = nrm((NR, D, D), D ** -0.5)
    inputs['rwkv_dir_vec'] = jnp.stack([unif((NR, 2, D), -6.0, -1.0), nrm((NR, 2, D), 0.1),
                                        0.85 + nrm((NR, 2, D), 0.05), 1.0 + nrm((NR, 2, D), 0.05)], axis=2)
    inputs['rwkv_w_lora_a'] = nrm((NR, 2, D, RWKV_DECAY_LORA), D ** -0.5)
    inputs['rwkv_w_lora_b'] = nrm((NR, 2, RWKV_DECAY_LORA, D), 0.5 * RWKV_DECAY_LORA ** -0.5)
    inputs['rwkv_a_lora_a'] = nrm((NR, 2, D, RWKV_AAA_LORA), D ** -0.5)
    inputs['rwkv_a_lora_b'] = nrm((NR, 2, RWKV_AAA_LORA, D), 0.5 * RWKV_AAA_LORA ** -0.5)
    inputs['rwkv_g_lora_a'] = nrm((NR, D, RWKV_GATE_LORA), D ** -0.5)
    inputs['rwkv_g_lora_b'] = nrm((NR, RWKV_GATE_LORA, D), RWKV_GATE_LORA ** -0.5)
    inputs['rwkv_r_k'] = nrm((NR, H, N), 0.1)
    inputs['rwkv_ln_x'] = jnp.stack([1.0 + nrm((NR, D), 0.1), nrm((NR, D), 0.02)], axis=1)
    inputs['diff_w_qkv'] = nrm((ND, D, 3 * D), D ** -0.5)
    inputs['diff_w_o'] = nrm((ND, D, D), D ** -0.5)
    inputs['diff_qk_g'] = 1.0 + nrm((ND, 2, dh), 0.1)
    inputs['diff_lambda'] = nrm((ND, 4, dh), 0.1)
    inputs['diff_subln_g'] = 1.0 + nrm((ND, 2 * dh), 0.1)
    return inputs


def reference(x, c, ctx, c_ctx, ada_w, ada_b, norm_g, ffn_w_in, ffn_w_out, pool_w, pool_scale,
              rwkv_mu, rwkv_w_rkv, rwkv_w_o, rwkv_dir_vec, rwkv_w_lora_a, rwkv_w_lora_b,
              rwkv_a_lora_a, rwkv_a_lora_b, rwkv_g_lora_a, rwkv_g_lora_b, rwkv_r_k, rwkv_ln_x,
              diff_w_qkv, diff_w_o, diff_qk_g, diff_lambda, diff_subln_g):
    s_lat = jax.nn.silu(c)
    s_ctx = jax.nn.silu(c_ctx)
    last_reader = max((i for i in range(DEPTH) if i % N_MIXERS != 0), default=-1)
    for i in range(DEPTH):
        kind, j = i % N_MIXERS, i // N_MIXERS
        ctx_in = i <= last_reader
        ctx_out = i < last_reader
        sh1, sc1, g1, sh2, sc2, g2 = jnp.split(s_lat @ ada_w[i] + ada_b[i], 6, axis=-1)
        h = modulate(rms_norm(x, norm_g[i, 0]), sh1[:, None], sc1[:, None])
        hc = None
        if ctx_in:
            csh1, csc1, cg1, csh2, csc2, cg2 = jnp.split(s_ctx @ ada_w[i] + ada_b[i], 6, axis=-1)
            hc = modulate(rms_norm(ctx, norm_g[i, 0]), csh1, csc1)
        if kind == 0:
            y = pool_mix(h, pool_w[j], pool_scale[j])
            yc = pool_mix(hc, pool_w[j], pool_scale[j]) if ctx_out else None
        elif kind == 1:
            yc, y = rwkv_mix(hc, h, rwkv_mu[j], rwkv_w_rkv[j], rwkv_w_o[j], rwkv_dir_vec[j],
                             rwkv_w_lora_a[j], rwkv_w_lora_b[j], rwkv_a_lora_a[j], rwkv_a_lora_b[j],
                             rwkv_g_lora_a[j], rwkv_g_lora_b[j], rwkv_r_k[j], rwkv_ln_x[j], ctx_out)
        else:
            lambda_init = 0.8 - 0.6 * math.exp(-0.3 * i)
            yc, y = diff_mix(hc, h, diff_w_qkv[j], diff_w_o[j], diff_qk_g[j], diff_lambda[j],
                             diff_subln_g[j], lambda_init, ctx_out)
        x = x + g1[:, None] * y
        h = modulate(rms_norm(x, norm_g[i, 1]), sh2[:, None], sc2[:, None])
        x = x + g2[:, None] * swiglu(h, ffn_w_in[i], ffn_w_out[i])
        if ctx_out:
            ctx = ctx + cg1 * yc
            hc = modulate(rms_norm(ctx, norm_g[i, 1]), csh2, csc2)
            ctx = ctx + cg2 * swiglu(hc, ffn_w_in[i], ffn_w_out[i])
    return x
```

```python
import functools
import math

import jax
import jax.numpy as jnp
from jax import lax
from jax.experimental import pallas as pl
from jax.experimental.pallas import tpu as pltpu

f32 = jnp.float32
bf16 = jnp.bfloat16

N_MIXERS = 3
EPS = 1e-6
POOL_WINDOWS = (2, 4, 8, 16)
POOL_HALO = 8
RWKV_HEAD = 64
LN_X_EPS = 64e-5
DIFF_HEAD = 128
ROPE_BASE = 10000.0
GRID_W = 64
LANES = 128
SUBLANES = 8
WKV_CHUNK = 64
ROW_CHUNK = 256
VMEM_LIMIT = 56 * 1024 * 1024

NT_DIMS = (((1,), (1,)), ((), ()))
TN_DIMS = (((0,), (0,)), ((), ()))


def _cparams(sem):
    return pltpu.CompilerParams(dimension_semantics=sem, vmem_limit_bytes=VMEM_LIMIT)


def _dot(a, b, dims=None):
    a = a.astype(bf16)
    b = b.astype(bf16)
    if dims is None:
        return jnp.dot(a, b, preferred_element_type=f32)
    return lax.dot_general(a, b, dims, preferred_element_type=f32)


def _sigmoid(x):
    return 1.0 / (1.0 + jnp.exp(-x))


def _softplus(x):
    return jnp.maximum(x, 0.0) + jnp.log(1.0 + jnp.exp(-jnp.abs(x)))


def _row_ids(shape, row0):
    return lax.broadcasted_iota(jnp.int32, shape, 0) + row0


def _mod_rows(mods_ref, chunk, b, n_batch, is_ctx, d):
    lat = mods_ref[pl.ds(b, 1), chunk * d:(chunk + 1) * d]
    if is_ctx is None:
        return lat
    ctx = mods_ref[n_batch:n_batch + 1, chunk * d:(chunk + 1) * d]
    return jnp.where(is_ctx, ctx, lat)


def _norm_mod(x, g, shift, scale):
    ms = jnp.mean(x * x, axis=-1, keepdims=True)
    h = x * lax.rsqrt(ms + EPS) * g
    return h * (1.0 + scale) + shift


def _is_ctx(shape, row0, ctx_len):
    if ctx_len == 0:
        return None
    return _row_ids(shape, row0) < ctx_len


def _mods_kernel(s_ref, w_ref, b_ref, o_ref):
    s = s_ref[...]
    s = s * _sigmoid(s)
    o_ref[...] = _dot(s, w_ref[...]) + b_ref[...]


def _mods_call(cs, ada_w, ada_b):
    depth, d, n6 = ada_w.shape
    tn = 1024
    return pl.pallas_call(
        _mods_kernel,
        grid=(depth, n6 // tn),
        in_specs=[
            pl.BlockSpec((SUBLANES, d), lambda l, n: (0, 0)),
            pl.BlockSpec((None, d, tn), lambda l, n: (l, 0, n)),
            pl.BlockSpec((None, 1, tn), lambda l, n: (l, 0, n)),
        ],
        out_specs=pl.BlockSpec((None, SUBLANES, tn), lambda l, n: (l, 0, n)),
        out_shape=jax.ShapeDtypeStruct((depth, SUBLANES, n6), f32),
        compiler_params=_cparams(("parallel", "parallel")),
        name="adaln_mods",
    )(cs, ada_w, ada_b.reshape(depth, 1, n6))


def _halo_specs(tq, d, n_tiles):
    per = tq // POOL_HALO
    last = n_tiles * per - 1
    cur = pl.BlockSpec((None, tq, d), lambda b, t: (b, t, 0))
    prev = pl.BlockSpec((None, POOL_HALO, d), lambda b, t: (b, jnp.maximum(t * per - 1, 0), 0))
    nxt = pl.BlockSpec((None, POOL_HALO, d), lambda b, t: (b, jnp.minimum((t + 1) * per, last), 0))
    return cur, prev, nxt


def _stream_edges(t, tq, ctx_len, n_tiles):
    ctx_tiles = ctx_len // tq
    first = t == 0
    last = t == n_tiles - 1
    if ctx_tiles:
        first = first | (t == ctx_tiles)
        last = last | (t == ctx_tiles - 1)
    return jnp.logical_not(first), jnp.logical_not(last)


def _pool_kernel(mods_ref, g_ref, w_ref, ls_ref, zc_ref, zp_ref, zn_ref, o_ref, ext_ref,
                 *, tq, d, ctx_len, n_tiles, n_batch, seq_len):
    b = pl.program_id(0)
    t = pl.program_id(1)
    row0 = t * tq
    g = g_ref[...]

    def hmod(x, r0):
        ic = _is_ctx((x.shape[0], 1), r0, ctx_len)
        sh = _mod_rows(mods_ref, 0, b, n_batch, ic, d)
        sc = _mod_rows(mods_ref, 1, b, n_batch, ic, d)
        return _norm_mod(x, g, sh, sc)

    has_prev, has_next = _stream_edges(t, tq, ctx_len, n_tiles)
    zc = zc_ref[...]
    hc = hmod(zc, row0)
    hp = jnp.where(has_prev, hmod(zp_ref[...], row0 - POOL_HALO), 0.0)
    hn = jnp.where(has_next, hmod(zn_ref[...], row0 + tq), 0.0)
    ext_ref[0:POOL_HALO, :] = hp
    ext_ref[POOL_HALO:POOL_HALO + tq, :] = hc
    ext_ref[POOL_HALO + tq:, :] = hn

    rows = _row_ids((tq, 1), row0)
    if ctx_len:
        in_ctx = rows < ctx_len
        pos = jnp.where(in_ctx, rows, rows - ctx_len)
        slen = jnp.where(in_ctx, ctx_len, seq_len)
    else:
        pos, slen = rows, seq_len

    cg = d // len(POOL_WINDOWS)
    ys = []
    for gi, win in enumerate(POOL_WINDOWS):
        lo_off, hi_off = win // 2, win - win // 2
        c0 = gi * cg
        acc = None
        for off in range(-lo_off, hi_off):
            piece = ext_ref[POOL_HALO + off:POOL_HALO + off + tq, c0:c0 + cg]
            acc = piece if acc is None else acc + piece
        cnt = jnp.minimum(pos + hi_off, slen) - jnp.maximum(pos - lo_off, 0)
        p = acc / cnt.astype(f32) - hc[:, c0:c0 + cg]
        ys.append(_dot(p, w_ref[gi]))
    y = jnp.concatenate(ys, axis=1) * ls_ref[...]
    ic = _is_ctx((tq, 1), row0, ctx_len)
    gate = _mod_rows(mods_ref, 2, b, n_batch, ic, d)
    o_ref[...] = zc + gate * y


def _pool_call(z, mods, norm_g, pool_w, pool_scale, ctx_len):
    n_batch, ltot, d = z.shape
    tq = 256
    n_tiles = ltot // tq
    cur, prev, nxt = _halo_specs(tq, d, n_tiles)
    ng, cg, _ = pool_w.shape
    kern = functools.partial(_pool_kernel, tq=tq, d=d, ctx_len=ctx_len, n_tiles=n_tiles,
                             n_batch=n_batch, seq_len=ltot - ctx_len)
    return pl.pallas_call(
        kern,
        grid=(n_batch, n_tiles),
        in_specs=[
            pl.BlockSpec(mods.shape, lambda b, t: (0, 0)),
            pl.BlockSpec((1, d), lambda b, t: (0, 0)),
            pl.BlockSpec((ng, cg, cg), lambda b, t: (0, 0, 0)),
            pl.BlockSpec((1, d), lambda b, t: (0, 0)),
            cur, prev, nxt,
        ],
        out_specs=pl.BlockSpec((None, tq, d), lambda b, t: (b, t, 0)),
        out_shape=jax.ShapeDtypeStruct(z.shape, f32),
        scratch_shapes=[pltpu.VMEM((tq + 2 * POOL_HALO, d), f32)],
        compiler_params=_cparams(("parallel", "parallel")),
        name="pool_mix",
    )(mods, norm_g.reshape(1, d), pool_w.astype(bf16), pool_scale.reshape(1, d), z, z, z)


def _for_rows(n_rows, body):
    def step(i, carry):
        body(pl.multiple_of(i * ROW_CHUNK, ROW_CHUNK))
        return carry
    lax.fori_loop(0, n_rows // ROW_CHUNK, step, 0)


def _ffn_kernel(mods_ref, g_ref, z_ref, wg_ref, wu_ref, wo_ref, o_ref, h_ref,
                *, tm, d, ctx_len, n_batch, n_f):
    b = pl.program_id(0)
    m = pl.program_id(1)
    f = pl.program_id(2)

    @pl.when(f == 0)
    def _():
        def pro(r0):
            rows = pl.ds(r0, ROW_CHUNK)
            ic = _is_ctx((ROW_CHUNK, 1), m * tm + r0, ctx_len)
            sh = _mod_rows(mods_ref, 3, b, n_batch, ic, d)
            sc = _mod_rows(mods_ref, 4, b, n_batch, ic, d)
            h_ref[rows, :] = _norm_mod(z_ref[rows, :], g_ref[...], sh, sc).astype(bf16)
            o_ref[rows, :] = jnp.zeros((ROW_CHUNK, d), f32)
        _for_rows(tm, pro)

    def main(r0):
        rows = pl.ds(r0, ROW_CHUNK)
        h = h_ref[rows, :]
        gate = jnp.dot(h, wg_ref[...], preferred_element_type=f32)
        up = jnp.dot(h, wu_ref[...], preferred_element_type=f32)
        act = (gate * _sigmoid(gate) * up).astype(bf16)
        o_ref[rows, :] += jnp.dot(act, wo_ref[...], preferred_element_type=f32)
    _for_rows(tm, main)

    @pl.when(f == n_f - 1)
    def _():
        def epi(r0):
            rows = pl.ds(r0, ROW_CHUNK)
            ic = _is_ctx((ROW_CHUNK, 1), m * tm + r0, ctx_len)
            g2 = _mod_rows(mods_ref, 5, b, n_batch, ic, d)
            o_ref[rows, :] = z_ref[rows, :] + g2 * o_ref[rows, :]
        _for_rows(tm, epi)


def _ffn_call(z, mods, norm_g, w_in, w_out, ctx_len, tm, out_rows=None, row_off=0):
    n_batch, ltot, d = z.shape
    fh = w_out.shape[0]
    tf = 512
    n_f = fh // tf
    out_rows = ltot if out_rows is None else out_rows
    n_m = out_rows // tm
    m_off = row_off // tm
    kern = functools.partial(_ffn_kernel, tm=tm, d=d, ctx_len=ctx_len, n_batch=n_batch, n_f=n_f)
    return pl.pallas_call(
        kern,
        grid=(n_batch, n_m, n_f),
        in_specs=[
            pl.BlockSpec(mods.shape, lambda b, m, f: (0, 0)),
            pl.BlockSpec((1, d), lambda b, m, f: (0, 0)),
            pl.BlockSpec((None, tm, d), lambda b, m, f: (b, m + m_off, 0)),
            pl.BlockSpec((d, tf), lambda b, m, f: (0, f)),
            pl.BlockSpec((d, tf), lambda b, m, f: (0, f + n_f)),
            pl.BlockSpec((tf, d), lambda b, m, f: (f, 0)),
        ],
        out_specs=pl.BlockSpec((None, tm, d), lambda b, m, f: (b, m, 0)),
        out_shape=jax.ShapeDtypeStruct((n_batch, out_rows, d), f32),
        scratch_shapes=[pltpu.VMEM((tm, d), bf16)],
        compiler_params=_cparams(("parallel", "parallel", "arbitrary")),
        name="swiglu_ffn",
    )(mods, norm_g.reshape(1, d), z, w_in, w_in, w_out)


def _mm_kernel(x_ref, w_ref, o_ref):
    o_ref[...] = jnp.dot(x_ref[...], w_ref[...], preferred_element_type=f32).astype(o_ref.dtype)


def _mm_resid_kernel(mods_ref, x_ref, w_ref, z_ref, o_ref, *, tm, ctx_len, n_batch, row_off):
    b = pl.program_id(0)
    m = pl.program_id(1)
    ic = _is_ctx((tm, 1), m * tm + row_off, ctx_len)
    lat = mods_ref[pl.ds(b, 1), :]
    gate = lat if ic is None else jnp.where(ic, mods_ref[n_batch:n_batch + 1, :], lat)
    y = jnp.dot(x_ref[...], w_ref[...], preferred_element_type=f32)
    o_ref[...] = z_ref[...] + gate * y


def _mm_resid_call(x, w, z, mods, ctx_len, tm, chunk, row_off=0):
    n_batch, rows, k = x.shape
    d = w.shape[1]
    tn = 1024
    assert row_off % tm == 0 and rows % tm == 0
    m_off = row_off // tm
    g_off = chunk * d // tn
    kern = functools.partial(_mm_resid_kernel, tm=tm, ctx_len=ctx_len, n_batch=n_batch, row_off=row_off)
    return pl.pallas_call(
        kern,
        grid=(n_batch, rows // tm, d // tn),
        in_specs=[
            pl.BlockSpec((SUBLANES, tn), lambda b, m, n: (0, g_off + n)),
            pl.BlockSpec((None, tm, k), lambda b, m, n: (b, m, 0)),
            pl.BlockSpec((k, tn), lambda b, m, n: (0, n)),
            pl.BlockSpec((None, tm, tn), lambda b, m, n: (b, m + m_off, n)),
        ],
        out_specs=pl.BlockSpec((None, tm, tn), lambda b, m, n: (b, m, n)),
        out_shape=jax.ShapeDtypeStruct((n_batch, rows, d), f32),
        compiler_params=_cparams(("parallel", "parallel", "arbitrary")),
        name="proj_residual",
    )(mods, x, w, z)


def _rwkv_mix_kernel(mods_ref, g_ref, mu_ref, zc_ref, zp_ref, zn_ref, o_ref, ext_ref,
                     *, tq, d, ctx_len, n_tiles, n_batch):
    b = pl.program_id(0)
    t = pl.program_id(1)
    row0 = t * tq
    g = g_ref[...]

    def hmod(x, r0):
        ic = _is_ctx((x.shape[0], 1), r0, ctx_len)
        sh = _mod_rows(mods_ref, 0, b, n_batch, ic, d)
        sc = _mod_rows(mods_ref, 1, b, n_batch, ic, d)
        return _norm_mod(x, g, sh, sc)

    has_prev, has_next = _stream_edges(t, tq, ctx_len, n_tiles)
    hc = hmod(zc_ref[...], row0)
    ext_ref[0:POOL_HALO, :] = jnp.where(has_prev, hmod(zp_ref[...], row0 - POOL_HALO), 0.0)
    ext_ref[POOL_HALO:POOL_HALO + tq, :] = hc
    ext_ref[POOL_HALO + tq:, :] = jnp.where(has_next, hmod(zn_ref[...], row0 + tq), 0.0)
    up = ext_ref[POOL_HALO - 1:POOL_HALO - 1 + tq, :]
    dn = ext_ref[POOL_HALO + 1:POOL_HALO + 1 + tq, :]
    xx = 0.5 * (up + dn) - hc
    for mi in range(6):
        o_ref[mi] = (hc + xx * mu_ref[mi:mi + 1, :]).astype(bf16)


def _rwkv_mix_call(z, mods, norm_g, mu, ctx_len):
    n_batch, ltot, d = z.shape
    tq = 256
    n_tiles = ltot // tq
    cur, prev, nxt = _halo_specs(tq, d, n_tiles)
    kern = functools.partial(_rwkv_mix_kernel, tq=tq, d=d, ctx_len=ctx_len, n_tiles=n_tiles, n_batch=n_batch)
    return pl.pallas_call(
        kern,
        grid=(n_batch, n_tiles),
        in_specs=[
            pl.BlockSpec(mods.shape, lambda b, t: (0, 0)),
            pl.BlockSpec((1, d), lambda b, t: (0, 0)),
            pl.BlockSpec((SUBLANES, d), lambda b, t: (0, 0)),
            cur, prev, nxt,
        ],
        out_specs=pl.BlockSpec((6, None, tq, d), lambda b, t: (0, b, t, 0)),
        out_shape=jax.ShapeDtypeStruct((6, n_batch, ltot, d), bf16),
        scratch_shapes=[pltpu.VMEM((tq + 2 * POOL_HALO, d), f32)],
        compiler_params=_cparams(("parallel", "parallel")),
        name="rwkv_shift_mix",
    )(mods, norm_g.reshape(1, d), jnp.pad(mu, ((0, SUBLANES - mu.shape[0]), (0, 0))), z, z, z)


def _rkv_kernel(x_ref, w_ref, o_ref, *, tn):
    y = jnp.dot(x_ref[...], w_ref[...], preferred_element_type=f32)
    for gi in range(tn // LANES):
        o_ref[gi] = y[:, gi * LANES:(gi + 1) * LANES]


def _rkv_call(mixes, w_rkv, tm):
    _, n_batch, ltot, d = mixes.shape
    tn = 1024
    src = (0, 2, 3)

    def x_map(b, m, p, n):
        return (jnp.where(p == 0, src[0], jnp.where(p == 1, src[1], src[2])), b, m, 0)

    return pl.pallas_call(
        functools.partial(_rkv_kernel, tn=tn),
        grid=(n_batch, ltot // tm, 3, d // tn),
        in_specs=[
            pl.BlockSpec((None, None, tm, d), x_map),
            pl.BlockSpec((None, d, tn), lambda b, m, p, n: (p, 0, n)),
        ],
        out_specs=pl.BlockSpec((None, None, tn // LANES, tm, LANES), lambda b, m, p, n: (p, b, n, m, 0)),
        out_shape=jax.ShapeDtypeStruct((3, n_batch, d // LANES, ltot, LANES), f32),
        compiler_params=_cparams(("parallel", "parallel", "arbitrary", "arbitrary")),
        name="rwkv_rkv_proj",
    )(mixes, w_rkv)


def _lora_a_kernel(x_ref, w_ref, o_ref):
    o_ref[...] = jnp.dot(x_ref[...], w_ref[...], preferred_element_type=f32)


def _lora_a_call(mixes, w_a, tm):
    _, n_batch, ltot, d = mixes.shape
    nh = w_a.shape[2]
    src = (1, 4, 5)

    def x_map(b, m, p):
        return (jnp.where(p == 0, src[0], jnp.where(p == 1, src[1], src[2])), b, m, 0)

    return pl.pallas_call(
        _lora_a_kernel,
        grid=(n_batch, ltot // tm, 3),
        in_specs=[
            pl.BlockSpec((None, None, tm, d), x_map),
            pl.BlockSpec((None, d, nh), lambda b, m, p: (p, 0, 0)),
        ],
        out_specs=pl.BlockSpec((None, None, tm, nh), lambda b, m, p: (p, b, m, 0)),
        out_shape=jax.ShapeDtypeStruct((3, n_batch, ltot, nh), f32),
        compiler_params=_cparams(("parallel", "parallel", "arbitrary")),
        name="rwkv_lora_a",
    )(mixes, w_a)


def _lora_b_kernel(h_ref, wb_ref, ab_ref, gb_ref, bias_ref, o_ref, *, d):
    n_pairs = d // LANES
    hw = jnp.tanh(h_ref[0])
    ha = h_ref[1]
    hg = _sigmoid(h_ref[2])
    outs = []
    for di in range(2):
        outs.append(_dot(hw[:, di * LANES:(di + 1) * LANES], wb_ref[di]) + bias_ref[di:di + 1, :])
    for di in range(2):
        outs.append(_dot(ha[:, di * LANES:(di + 1) * LANES], ab_ref[di]) + bias_ref[2 + di:3 + di, :])
    outs.append(_dot(hg, gb_ref[...]))
    for oi, y in enumerate(outs):
        for p in range(n_pairs):
            o_ref[oi, p] = y[:, p * LANES:(p + 1) * LANES]


def _lora_b_call(h, w_b, a_b, g_b, bias, tm):
    _, n_batch, ltot, nh = h.shape
    d = g_b.shape[1]
    n_pairs = d // LANES
    return pl.pallas_call(
        functools.partial(_lora_b_kernel, d=d),
        grid=(n_batch, ltot // tm),
        in_specs=[
            pl.BlockSpec((3, None, tm, nh), lambda b, m: (0, b, m, 0)),
            pl.BlockSpec(w_b.shape, lambda b, m: (0, 0, 0)),
            pl.BlockSpec(a_b.shape, lambda b, m: (0, 0, 0)),
            pl.BlockSpec(g_b.shape, lambda b, m: (0, 0)),
            pl.BlockSpec(bias.shape, lambda b, m: (0, 0)),
        ],
        out_specs=pl.BlockSpec((5, None, n_pairs, tm, LANES), lambda b, m: (0, b, 0, m, 0)),
        out_shape=jax.ShapeDtypeStruct((5, n_batch, n_pairs, ltot, LANES), f32),
        compiler_params=_cparams(("parallel", "parallel")),
        name="rwkv_lora_b",
    )(h, w_b, a_b, g_b, bias)


def _seg_sum(x, lo):
    s_lo = jnp.sum(jnp.where(lo, x, 0.0), axis=1, keepdims=True)
    s_all = jnp.sum(x, axis=1, keepdims=True)
    return jnp.where(lo, s_lo, s_all - s_lo)


def _wkv_kernel(r_ref, k_ref, v_ref, w_ref, a_ref, par_ref, y_ref, bon_ref, s_ref,
                *, n_pairs, reverse):
    c = pl.program_id(1)
    L = WKV_CHUNK
    P = 2 * L

    @pl.when(c == 0)
    def _():
        s_ref[...] = jnp.zeros_like(s_ref)

    lane = lax.broadcasted_iota(jnp.int32, (L, LANES), 1)
    lo = lane < RWKV_HEAD
    ri = lax.broadcasted_iota(jnp.int32, (P, P), 0)
    ci = lax.broadcasted_iota(jnp.int32, (P, P), 1)
    same = (ri // L) == (ci // L)
    ii, jj = ri % L, ci % L
    if reverse:
        strict, incl = same & (jj > ii), same & (jj >= ii)
    else:
        strict, incl = same & (jj < ii), same & (jj <= ii)
    eye = (ri == ci).astype(f32)
    ti = lax.broadcasted_iota(jnp.int32, (L, L), 0)
    tj = lax.broadcasted_iota(jnp.int32, (L, L), 1)
    tri = ((tj >= ti) if reverse else (tj <= ti)).astype(bf16)
    diag_p = ((lax.broadcasted_iota(jnp.int32, (P, LANES), 0) // L)
              == (lax.broadcasted_iota(jnp.int32, (P, LANES), 1) // RWKV_HEAD))

    def blockdiag(x):
        return jnp.where(diag_p, jnp.concatenate([x, x], axis=0), 0.0)

    def stack(x):
        return jnp.concatenate([x, x], axis=0)

    def body(p, carry):
        r, k, v = r_ref[p], k_ref[p], v_ref[p]
        k_k = par_ref[p, 0:1, :]
        k_a = par_ref[p, 1:2, :]
        r_k = par_ref[p, 2:3, :]
        lw = -jnp.exp(-_softplus(-w_ref[p]) - 0.5)
        a = _sigmoid(a_ref[p])
        kn = k * k_k
        kk = kn / jnp.maximum(jnp.sqrt(_seg_sum(kn * kn, lo)), 1e-12)
        kd = k * (1.0 + (a - 1.0) * k_a)
        bon_ref[p] = _seg_sum(r * kd * r_k, lo) * v

        lw_hi = lw.astype(bf16)
        lw_lo = (lw - lw_hi.astype(f32)).astype(bf16)
        cum = (jnp.dot(tri, lw_hi, preferred_element_type=f32)
               + jnp.dot(tri, lw_lo, preferred_element_type=f32))
        tot = cum[0:1, :] if reverse else cum[L - 1:L, :]
        e_neg = jnp.exp(-cum)
        e_end = jnp.exp(tot - cum)
        b_in = kk * a
        at = -kk * jnp.exp(cum - lw)
        rt = r * jnp.exp(cum)

        lhs = jnp.concatenate([blockdiag(at), blockdiag(rt)], axis=0).astype(bf16)
        rhs = jnp.concatenate([stack(b_in * e_neg), stack(kd * e_neg)], axis=0)
        sc = _dot(lhs, rhs, NT_DIMS)
        m_ab = jnp.where(strict, sc[0:P, 0:P], 0.0)
        m_ak = jnp.where(strict, sc[0:P, P:2 * P], 0.0)
        a_rb = jnp.where(incl, sc[P:2 * P, 0:P], 0.0)
        a_rk = jnp.where(incl, sc[P:2 * P, P:2 * P], 0.0)

        t_inv = eye + jnp.where((ri // 2) == (ci // 2), m_ab, 0.0)
        s = 4
        while s <= L:
            e = jnp.where(((ri // s) == (ci // s)) & ((ri // (s // 2)) != (ci // (s // 2))), m_ab, 0.0)
            t_inv = t_inv + _dot(_dot(t_inv, e), t_inv)
            s *= 2

        s0 = s_ref[p]
        g1 = _dot(lhs, s0, NT_DIMS)
        v_bd = blockdiag(v)
        u = _dot(t_inv, g1[0:P] + _dot(m_ak, v_bd))
        uv = jnp.concatenate([u, v_bd], axis=0)
        y = g1[P:2 * P] + _dot(jnp.concatenate([a_rb, a_rk], axis=1), uv)
        y_ref[p] = y[0:L] + y[L:P]
        upd = _dot(uv, jnp.concatenate([stack(b_in * e_end), stack(kd * e_end)], axis=0), TN_DIMS)
        s_ref[p] = s0 * jnp.exp(tot) + jnp.where(diag_p, upd, 0.0)
        return carry

    lax.fori_loop(0, n_pairs, body, 0)


def _wkv_call(rkv, pre, params, ctx_len, direction):
    _, n_batch, n_pairs, ltot, _ = rkv.shape
    L = WKV_CHUNK
    n_chunks = ltot // L
    ctx_chunks = ctx_len // L
    reverse = direction == 1

    def chunk_of(c):
        if not reverse:
            return c
        return jnp.where(c < ctx_chunks, ctx_chunks - 1 - c, n_chunks - 1 - (c - ctx_chunks))

    def spec(lead):
        return pl.BlockSpec((None, None, n_pairs, L, LANES), lambda b, c: (lead, b, 0, chunk_of(c), 0))

    out_spec = pl.BlockSpec((None, n_pairs, L, LANES), lambda b, c: (b, 0, chunk_of(c), 0))
    out_sds = jax.ShapeDtypeStruct((n_batch, n_pairs, ltot, LANES), f32)
    return pl.pallas_call(
        functools.partial(_wkv_kernel, n_pairs=n_pairs, reverse=reverse),
        grid=(n_batch, n_chunks),
        in_specs=[spec(0), spec(1), spec(2),
                  pl.BlockSpec((None, None, n_pairs, L, LANES), lambda b, c: (direction, b, 0, chunk_of(c), 0)),
                  pl.BlockSpec((None, None, n_pairs, L, LANES), lambda b, c: (2 + direction, b, 0, chunk_of(c), 0)),
                  pl.BlockSpec((None, n_pairs, SUBLANES, LANES), lambda b, c: (direction, 0, 0, 0))],
        out_specs=[out_spec, out_spec],
        out_shape=[out_sds, out_sds],
        scratch_shapes=[pltpu.VMEM((n_pairs, 2 * L, LANES), f32)],
        compiler_params=_cparams(("parallel", "arbitrary")),
        name="rwkv_wkv_fwd" if not reverse else "rwkv_wkv_bwd",
    )(rkv, rkv, rkv, pre, pre, params)


def _rwkv_out_kernel(y0_ref, y1_ref, b0_ref, b1_ref, g_ref, ln_ref, o_ref, *, n_pairs):
    tq = y0_ref.shape[1]
    lo = lax.broadcasted_iota(jnp.int32, (tq, LANES), 1) < RWKV_HEAD
    inv_n = 1.0 / RWKV_HEAD
    for p in range(n_pairs):
        y = y0_ref[p] + y1_ref[p]
        mean = _seg_sum(y, lo) * inv_n
        yc = y - mean
        var = _seg_sum(yc * yc, lo) * inv_n
        yn = yc * lax.rsqrt(var + LN_X_EPS)
        sl = slice(p * LANES, (p + 1) * LANES)
        o = yn * ln_ref[0:1, sl] + ln_ref[1:2, sl] + b0_ref[p] + b1_ref[p]
        o_ref[:, sl] = (o * g_ref[p]).astype(bf16)


def _rwkv_out_call(y0, y1, bon0, bon1, pre, ln_x):
    n_batch, n_pairs, ltot, _ = y0.shape
    d = n_pairs * LANES
    tq = 256
    spec = pl.BlockSpec((None, n_pairs, tq, LANES), lambda b, t: (b, 0, t, 0))
    return pl.pallas_call(
        functools.partial(_rwkv_out_kernel, n_pairs=n_pairs),
        grid=(n_batch, ltot // tq),
        in_specs=[spec, spec, spec, spec,
                  pl.BlockSpec((None, None, n_pairs, tq, LANES), lambda b, t: (4, b, 0, t, 0)),
                  pl.BlockSpec((SUBLANES, d), lambda b, t: (0, 0))],
        out_specs=pl.BlockSpec((None, tq, d), lambda b, t: (b, t, 0)),
        out_shape=jax.ShapeDtypeStruct((n_batch, ltot, d), bf16),
        compiler_params=_cparams(("parallel", "parallel")),
        name="rwkv_groupnorm_gate",
    )(y0, y1, bon0, bon1, pre, jnp.pad(ln_x, ((0, SUBLANES - ln_x.shape[0]), (0, 0))))


def _rwkv_layer(z, mods, norm_g, ctx_len, mu, w_rkv, w_o, dir_vec, w_la, w_lb, a_la, a_lb, g_la, g_lb, r_k, ln_x):
    n_batch, ltot, d = z.shape
    n_pairs = d // LANES
    tm = 768 if ltot % 768 == 0 else 512
    mixes = _rwkv_mix_call(z, mods, norm_g, mu, ctx_len)
    rkv = _rkv_call(mixes, w_rkv.astype(bf16), tm)

    def pad_cols(w):
        return jnp.concatenate([jnp.pad(w[i], ((0, 0), (0, LANES - w.shape[2]))) for i in range(2)], axis=1)

    def pad_rows(w):
        return jnp.pad(w, ((0, 0), (0, LANES - w.shape[1]), (0, 0)))

    w_a = jnp.stack([pad_cols(w_la), pad_cols(a_la), g_la]).astype(bf16)
    h = _lora_a_call(mixes, w_a, tm)
    bias = jnp.pad(jnp.stack([dir_vec[0, 0], dir_vec[1, 0], dir_vec[0, 1], dir_vec[1, 1]]), ((0, 4), (0, 0)))
    pre = _lora_b_call(h, pad_rows(w_lb).astype(bf16), pad_rows(a_lb).astype(bf16), g_lb.astype(bf16), bias, 256)
    rk_row = r_k.reshape(d)
    params = jnp.stack([jnp.stack([dir_vec[di, 2], dir_vec[di, 3], rk_row]) for di in range(2)])
    params = jnp.pad(params, ((0, 0), (0, SUBLANES - 3), (0, 0)))
    params = params.reshape(2, SUBLANES, n_pairs, LANES).transpose(0, 2, 1, 3)
    y0, bon0 = _wkv_call(rkv, pre, params, ctx_len, 0)
    y1, bon1 = _wkv_call(rkv, pre, params, ctx_len, 1)
    og = _rwkv_out_call(y0, y1, bon0, bon1, pre, ln_x)
    return _mm_resid_call(og, w_o.astype(bf16), z, mods, ctx_len, tm, chunk=2)


def _rope_tables(ctx_len, seq_len):
    rows = seq_len // GRID_W
    row = jnp.repeat(jnp.arange(rows, dtype=f32), GRID_W)
    col = jnp.tile(jnp.arange(GRID_W, dtype=f32), rows)
    n_freq = DIFF_HEAD // 4
    inv = ROPE_BASE ** (-jnp.arange(n_freq, dtype=f32) / n_freq)
    ang = jnp.concatenate([row[:, None] * inv, col[:, None] * inv], axis=-1)
    cos = jnp.repeat(jnp.cos(ang), 2, axis=1)
    sin = jnp.repeat(jnp.sin(ang), 2, axis=1) * jnp.tile(jnp.array([-1.0, 1.0], f32), DIFF_HEAD // 2)
    cos = jnp.concatenate([jnp.ones((ctx_len, DIFF_HEAD), f32), cos], axis=0)
    sin = jnp.concatenate([jnp.zeros((ctx_len, DIFF_HEAD), f32), sin], axis=0)
    return cos, sin


def _qkv_kernel(mods_ref, g_ref, qkg_ref, cos_ref, sin_ref, z_ref, w_ref, o_ref, h_ref,
                *, tm, tn, d, ctx_len, n_batch):
    b = pl.program_id(0)
    m = pl.program_id(1)
    n = pl.program_id(2)
    n_reg = d // tn

    @pl.when(n == 0)
    def _():
        def pro(r0):
            rows = pl.ds(r0, ROW_CHUNK)
            ic = _is_ctx((ROW_CHUNK, 1), m * tm + r0, ctx_len)
            sh = _mod_rows(mods_ref, 0, b, n_batch, ic, d)
            sc = _mod_rows(mods_ref, 1, b, n_batch, ic, d)
            h_ref[rows, :] = _norm_mod(z_ref[rows, :], g_ref[...], sh, sc).astype(bf16)
        _for_rows(tm, pro)

    y = jnp.dot(h_ref[...], w_ref[...], preferred_element_type=f32)

    def qk_epilogue(gain, scale):
        cos = cos_ref[...]
        sin = sin_ref[...]
        even = (lax.broadcasted_iota(jnp.int32, (tm, LANES), 1) % 2) == 0
        for gi in range(tn // LANES):
            x = y[:, gi * LANES:(gi + 1) * LANES]
            ms = jnp.mean(x * x, axis=-1, keepdims=True)
            x = x * lax.rsqrt(ms + EPS) * gain
            partner = jnp.where(even, pltpu.roll(x, LANES - 1, 1), pltpu.roll(x, 1, 1))
            x = x * cos + partner * sin
            o_ref[:, gi * LANES:(gi + 1) * LANES] = (x * scale).astype(bf16)

    @pl.when(n < n_reg)
    def _():
        qk_epilogue(qkg_ref[0:1, :], DIFF_HEAD ** -0.5)

    @pl.when((n >= n_reg) & (n < 2 * n_reg))
    def _():
        qk_epilogue(qkg_ref[1:2, :], 1.0)

    @pl.when(n >= 2 * n_reg)
    def _():
        o_ref[...] = y.astype(bf16)


def _qkv_call(z, mods, norm_g, w_qkv, qk_g, cos, sin, ctx_len, tm):
    n_batch, ltot, d = z.shape
    tn = 512
    kern = functools.partial(_qkv_kernel, tm=tm, tn=tn, d=d, ctx_len=ctx_len, n_batch=n_batch)
    return pl.pallas_call(
        kern,
        grid=(n_batch, ltot // tm, 3 * d // tn),
        in_specs=[
            pl.BlockSpec(mods.shape, lambda b, m, n: (0, 0)),
            pl.BlockSpec((1, d), lambda b, m, n: (0, 0)),
            pl.BlockSpec((SUBLANES, DIFF_HEAD), lambda b, m, n: (0, 0)),
            pl.BlockSpec((tm, DIFF_HEAD), lambda b, m, n: (m, 0)),
            pl.BlockSpec((tm, DIFF_HEAD), lambda b, m, n: (m, 0)),
            pl.BlockSpec((None, tm, d), lambda b, m, n: (b, m, 0)),
            pl.BlockSpec((d, tn), lambda b, m, n: (0, n)),
        ],
        out_specs=pl.BlockSpec((None, tm, tn), lambda b, m, n: (b, m, n)),
        out_shape=jax.ShapeDtypeStruct((n_batch, ltot, 3 * d), bf16),
        scratch_shapes=[pltpu.VMEM((tm, d), bf16)],
        compiler_params=_cparams(("parallel", "parallel", "arbitrary")),
        name="diff_qkv_proj",
    )(mods, norm_g.reshape(1, d), jnp.pad(qk_g, ((0, SUBLANES - qk_g.shape[0]), (0, 0))), cos, sin, z, w_qkv)


def _attn_kernel(lam_ref, sg_ref, q_ref, k_ref, v_ref, o_ref, *, tk, lambda_init):
    tq = q_ref.shape[0]
    n_k = k_ref.shape[0] // tk
    hd = DIFF_HEAD
    lv = lam_ref[...]
    lam = (jnp.exp(jnp.sum(lv[0:1] * lv[1:2], axis=1, keepdims=True))
           - jnp.exp(jnp.sum(lv[2:3] * lv[3:4], axis=1, keepdims=True)) + lambda_init)
    q = q_ref[...]

    def step(j, carry):
        k0 = pl.multiple_of(j * tk, tk)
        kb = k_ref[pl.ds(k0, tk), :]
        vb = v_ref[pl.ds(k0, tk), :]
        new = []
        for i in range(2):
            m_i, l_i, acc_i = carry[3 * i:3 * i + 3]
            s = lax.dot_general(q[:, i * hd:(i + 1) * hd], kb[:, i * hd:(i + 1) * hd], NT_DIMS,
                                preferred_element_type=f32)
            m_new = jnp.maximum(m_i, jnp.max(s, axis=1, keepdims=True))
            alpha = jnp.exp(m_i - m_new)
            p = jnp.exp(s - m_new)
            l_new = alpha * l_i + jnp.sum(p, axis=1, keepdims=True)
            acc_new = alpha * acc_i + jnp.dot(p.astype(bf16), vb, preferred_element_type=f32)
            new += [m_new, l_new, acc_new]
        return tuple(new)

    init = []
    for _ in range(2):
        init += [jnp.full((tq, 1), -jnp.inf, f32), jnp.zeros((tq, 1), f32), jnp.zeros((tq, 2 * hd), f32)]
    m0, l0, a0, m1, l1, a1 = lax.fori_loop(0, n_k, step, tuple(init))
    o = a0 / l0 - lam * (a1 / l1)
    ms = jnp.mean(o * o, axis=-1, keepdims=True)
    o_ref[...] = (o * lax.rsqrt(ms + EPS) * (sg_ref[0:1, :] * (1.0 - lambda_init))).astype(bf16)


def _attn_call(qkv, lam_vec, subln_g, ctx_len, lambda_init):
    n_batch, ltot, d3 = qkv.shape
    d = d3 // 3
    hw = 2 * DIFF_HEAD
    n_heads = d // hw
    seq_len = ltot - ctx_len
    tq = 256
    tk = 768 if ltot % 768 == 0 else 512
    assert ctx_len % tq == 0 and seq_len % tq == 0 and ltot % tk == 0
    q_off = ctx_len // tq
    kern = functools.partial(_attn_kernel, tk=tk, lambda_init=lambda_init)
    return pl.pallas_call(
        kern,
        grid=(n_batch, n_heads, seq_len // tq),
        in_specs=[
            pl.BlockSpec((SUBLANES, DIFF_HEAD), lambda b, h, t: (0, 0)),
            pl.BlockSpec((SUBLANES, hw), lambda b, h, t: (0, 0)),
            pl.BlockSpec((None, tq, hw), lambda b, h, t: (b, t + q_off, h)),
            pl.BlockSpec((None, ltot, hw), lambda b, h, t: (b, 0, n_heads + h)),
            pl.BlockSpec((None, ltot, hw), lambda b, h, t: (b, 0, 2 * n_heads + h)),
        ],
        out_specs=pl.BlockSpec((None, tq, hw), lambda b, h, t: (b, t, h)),
        out_shape=jax.ShapeDtypeStruct((n_batch, seq_len, d), bf16),
        compiler_params=_cparams(("parallel", "parallel", "parallel")),
        name="diff_attention",
    )(jnp.pad(lam_vec, ((0, SUBLANES - lam_vec.shape[0]), (0, 0))),
      jnp.pad(subln_g.reshape(1, hw), ((0, SUBLANES - 1), (0, 0))), qkv, qkv, qkv)


def _diff_layer(z, mods, norm_g, ctx_len, w_qkv, w_o, qk_g, lam_vec, subln_g, lambda_init):
    n_batch, ltot, d = z.shape
    seq_len = ltot - ctx_len
    tm = 768 if ltot % 768 == 0 else 512
    cos, sin = _rope_tables(ctx_len, seq_len)
    qkv = _qkv_call(z, mods, norm_g, w_qkv.astype(bf16), qk_g, cos, sin, ctx_len, tm)
    o = _attn_call(qkv, lam_vec, subln_g, ctx_len, lambda_init)
    return _mm_resid_call(o, w_o.astype(bf16), z, mods, 0, 256, chunk=2, row_off=ctx_len)


def kernel(x, c, ctx, c_ctx, ada_w, ada_b, norm_g, ffn_w_in, ffn_w_out, pool_w, pool_scale, rwkv_mu, rwkv_w_rkv, rwkv_w_o, rwkv_dir_vec, rwkv_w_lora_a, rwkv_w_lora_b, rwkv_a_lora_a, rwkv_a_lora_b, rwkv_g_lora_a, rwkv_g_lora_b, rwkv_r_k, rwkv_ln_x, diff_w_qkv, diff_w_o, diff_qk_g, diff_lambda, diff_subln_g):
    n_batch, seq_len, d = x.shape
    depth = ada_w.shape[0]
    ctx_len = ctx.shape[1]
    assert n_batch + 1 <= SUBLANES
    cs = jnp.concatenate([c, c_ctx[None, :], jnp.zeros((SUBLANES - n_batch - 1, d), f32)], axis=0)
    mods_all = _mods_call(cs, ada_w, ada_b)

    last_reader = max((i for i in range(depth) if i % N_MIXERS != 0), default=-1)
    z = jnp.concatenate([ctx, x], axis=1) if last_reader >= 0 else x
    cur_ctx = ctx_len if last_reader >= 0 else 0
    for i in range(depth):
        kind, j = i % N_MIXERS, i // N_MIXERS
        mods = mods_all[i]
        if kind == 0:
            z = _pool_call(z, mods, norm_g[i, 0], pool_w[j], pool_scale[j], cur_ctx)
        elif kind == 1:
            z = _rwkv_layer(z, mods, norm_g[i, 0], cur_ctx, rwkv_mu[j], rwkv_w_rkv[j], rwkv_w_o[j], rwkv_dir_vec[j],
                            rwkv_w_lora_a[j], rwkv_w_lora_b[j], rwkv_a_lora_a[j], rwkv_a_lora_b[j],
                            rwkv_g_lora_a[j], rwkv_g_lora_b[j], rwkv_r_k[j], rwkv_ln_x[j])
        else:
            lambda_init = 0.8 - 0.6 * math.exp(-0.3 * i)
            z = _diff_layer(z, mods, norm_g[i, 0], cur_ctx, diff_w_qkv[j], diff_w_o[j], diff_qk_g[j],
                            diff_lambda[j], diff_subln_g[j], lambda_init)
            cur_ctx = 0
        if cur_ctx and i >= last_reader:
            z = z[:, cur_ctx:]
            cur_ctx = 0
        ltot = z.shape[1]
        tm = 768 if ltot % 768 == 0 else 512
        z = _ffn_call(z, mods, norm_g[i, 1], ffn_w_in[i].astype(bf16), ffn_w_out[i].astype(bf16), cur_ctx, tm)
    return z[:, cur_ctx:] if cur_ctx else z
```

```python
import functools
import math

import jax
import jax.numpy as jnp
from jax import lax
from jax.experimental import pallas as pl
from jax.experimental.pallas import tpu as pltpu

f32 = jnp.float32
bf16 = jnp.bfloat16

N_MIXERS = 3
EPS = 1e-6
POOL_WINDOWS = (2, 4, 8, 16)
POOL_HALO = 8
RWKV_HEAD = 64
LN_X_EPS = 64e-5
DIFF_HEAD = 128
ROPE_BASE = 10000.0
LOG2E = 1.4426950408889634
GRID_W = 64
LANES = 128
SUBLANES = 8
WKV_CHUNK = 64
ROW_CHUNK = 256
VMEM_LIMIT = 56 * 1024 * 1024

NT_DIMS = (((1,), (1,)), ((), ()))
TN_DIMS = (((0,), (0,)), ((), ()))


def _cparams(sem):
    return pltpu.CompilerParams(dimension_semantics=sem, vmem_limit_bytes=VMEM_LIMIT)


def _dot(a, b, dims=None):
    a = a.astype(bf16)
    b = b.astype(bf16)
    if dims is None:
        return jnp.dot(a, b, preferred_element_type=f32)
    return lax.dot_general(a, b, dims, preferred_element_type=f32)


def _sigmoid(x):
    return 1.0 / (1.0 + jnp.exp(-x))


def _softplus(x):
    return jnp.maximum(x, 0.0) + jnp.log(1.0 + jnp.exp(-jnp.abs(x)))


def _row_ids(shape, row0):
    return lax.broadcasted_iota(jnp.int32, shape, 0) + row0


def _mod_rows(mods_ref, chunk, b, n_batch, is_ctx, d):
    lat = mods_ref[pl.ds(b, 1), chunk * d:(chunk + 1) * d]
    if is_ctx is None:
        return lat
    ctx = mods_ref[n_batch:n_batch + 1, chunk * d:(chunk + 1) * d]
    return jnp.where(is_ctx, ctx, lat)


def _norm_mod(x, g, shift, scale):
    ms = jnp.mean(x * x, axis=-1, keepdims=True)
    h = x * lax.rsqrt(ms + EPS) * g
    return h * (1.0 + scale) + shift


def _is_ctx(shape, row0, ctx_len):
    if ctx_len == 0:
        return None
    return _row_ids(shape, row0) < ctx_len


def _mods_kernel(s_ref, w_ref, b_ref, o_ref):
    s = s_ref[...]
    s = s * _sigmoid(s)
    o_ref[...] = _dot(s, w_ref[...]) + b_ref[...]


def _mods_call(cs, ada_w, ada_b):
    depth, d, n6 = ada_w.shape
    tn = 1024
    return pl.pallas_call(
        _mods_kernel,
        grid=(depth, n6 // tn),
        in_specs=[
            pl.BlockSpec((SUBLANES, d), lambda l, n: (0, 0)),
            pl.BlockSpec((None, d, tn), lambda l, n: (l, 0, n)),
            pl.BlockSpec((None, 1, tn), lambda l, n: (l, 0, n)),
        ],
        out_specs=pl.BlockSpec((None, SUBLANES, tn), lambda l, n: (l, 0, n)),
        out_shape=jax.ShapeDtypeStruct((depth, SUBLANES, n6), f32),
        compiler_params=_cparams(("parallel", "parallel")),
        name="adaln_mods",
    )(cs, ada_w, ada_b.reshape(depth, 1, n6))


def _halo_specs(tq, d, n_tiles):
    per = tq // POOL_HALO
    last = n_tiles * per - 1
    cur = pl.BlockSpec((None, tq, d), lambda b, t: (b, t, 0))
    prev = pl.BlockSpec((None, POOL_HALO, d), lambda b, t: (b, jnp.maximum(t * per - 1, 0), 0))
    nxt = pl.BlockSpec((None, POOL_HALO, d), lambda b, t: (b, jnp.minimum((t + 1) * per, last), 0))
    return cur, prev, nxt


def _stream_edges(t, tq, ctx_len, n_tiles):
    ctx_tiles = ctx_len // tq
    first = t == 0
    last = t == n_tiles - 1
    if ctx_tiles:
        first = first | (t == ctx_tiles)
        last = last | (t == ctx_tiles - 1)
    return jnp.logical_not(first), jnp.logical_not(last)


def _pool_kernel(mods_ref, g_ref, w_ref, ls_ref, zc_ref, zp_ref, zn_ref, o_ref, ext_ref,
                 *, tq, d, ctx_len, n_tiles, n_batch, seq_len):
    b = pl.program_id(0)
    t = pl.program_id(1)
    row0 = t * tq
    g = g_ref[...]

    def hmod(x, r0):
        ic = _is_ctx((x.shape[0], 1), r0, ctx_len)
        sh = _mod_rows(mods_ref, 0, b, n_batch, ic, d)
        sc = _mod_rows(mods_ref, 1, b, n_batch, ic, d)
        return _norm_mod(x, g, sh, sc)

    has_prev, has_next = _stream_edges(t, tq, ctx_len, n_tiles)
    zc = zc_ref[...]
    hc = hmod(zc, row0)
    hp = jnp.where(has_prev, hmod(zp_ref[...], row0 - POOL_HALO), 0.0)
    hn = jnp.where(has_next, hmod(zn_ref[...], row0 + tq), 0.0)
    ext_ref[0:POOL_HALO, :] = hp
    ext_ref[POOL_HALO:POOL_HALO + tq, :] = hc
    ext_ref[POOL_HALO + tq:, :] = hn

    rows = _row_ids((tq, 1), row0)
    if ctx_len:
        in_ctx = rows < ctx_len
        pos = jnp.where(in_ctx, rows, rows - ctx_len)
        slen = jnp.where(in_ctx, ctx_len, seq_len)
    else:
        pos, slen = rows, seq_len

    cg = d // len(POOL_WINDOWS)
    ys = []
    for gi, win in enumerate(POOL_WINDOWS):
        lo_off, hi_off = win // 2, win - win // 2
        c0 = gi * cg
        acc = None
        for off in range(-lo_off, hi_off):
            piece = ext_ref[POOL_HALO + off:POOL_HALO + off + tq, c0:c0 + cg]
            acc = piece if acc is None else acc + piece
        cnt = jnp.minimum(pos + hi_off, slen) - jnp.maximum(pos - lo_off, 0)
        p = acc / cnt.astype(f32) - hc[:, c0:c0 + cg]
        ys.append(_dot(p, w_ref[gi]))
    y = jnp.concatenate(ys, axis=1) * ls_ref[...]
    ic = _is_ctx((tq, 1), row0, ctx_len)
    gate = _mod_rows(mods_ref, 2, b, n_batch, ic, d)
    o_ref[...] = zc + gate * y


def _pool_call(z, mods, norm_g, pool_w, pool_scale, ctx_len):
    n_batch, ltot, d = z.shape
    tq = 256
    n_tiles = ltot // tq
    cur, prev, nxt = _halo_specs(tq, d, n_tiles)
    ng, cg, _ = pool_w.shape
    kern = functools.partial(_pool_kernel, tq=tq, d=d, ctx_len=ctx_len, n_tiles=n_tiles,
                             n_batch=n_batch, seq_len=ltot - ctx_len)
    return pl.pallas_call(
        kern,
        grid=(n_batch, n_tiles),
        in_specs=[
            pl.BlockSpec(mods.shape, lambda b, t: (0, 0)),
            pl.BlockSpec((1, d), lambda b, t: (0, 0)),
            pl.BlockSpec((ng, cg, cg), lambda b, t: (0, 0, 0)),
            pl.BlockSpec((1, d), lambda b, t: (0, 0)),
            cur, prev, nxt,
        ],
        out_specs=pl.BlockSpec((None, tq, d), lambda b, t: (b, t, 0)),
        out_shape=jax.ShapeDtypeStruct(z.shape, f32),
        scratch_shapes=[pltpu.VMEM((tq + 2 * POOL_HALO, d), f32)],
        compiler_params=_cparams(("parallel", "parallel")),
        name="pool_mix",
    )(mods, norm_g.reshape(1, d), pool_w.astype(bf16), pool_scale.reshape(1, d), z, z, z)


def _for_rows(n_rows, body):
    def step(i, carry):
        body(pl.multiple_of(i * ROW_CHUNK, ROW_CHUNK))
        return carry
    lax.fori_loop(0, n_rows // ROW_CHUNK, step, 0)


def _ffn_kernel(mods_ref, g_ref, z_ref, wg_ref, wu_ref, wo_ref, o_ref, h_ref,
                *, tm, d, ctx_len, n_batch, n_f):
    b = pl.program_id(0)
    m = pl.program_id(1)
    f = pl.program_id(2)

    @pl.when(f == 0)
    def _():
        def pro(r0):
            rows = pl.ds(r0, ROW_CHUNK)
            ic = _is_ctx((ROW_CHUNK, 1), m * tm + r0, ctx_len)
            sh = _mod_rows(mods_ref, 3, b, n_batch, ic, d)
            sc = _mod_rows(mods_ref, 4, b, n_batch, ic, d)
            h_ref[rows, :] = _norm_mod(z_ref[rows, :], g_ref[...], sh, sc).astype(bf16)
            o_ref[rows, :] = jnp.zeros((ROW_CHUNK, d), f32)
        _for_rows(tm, pro)

    def main(r0):
        rows = pl.ds(r0, ROW_CHUNK)
        h = h_ref[rows, :]
        gate = jnp.dot(h, wg_ref[...], preferred_element_type=f32)
        up = jnp.dot(h, wu_ref[...], preferred_element_type=f32)
        act = (gate * _sigmoid(gate) * up).astype(bf16)
        o_ref[rows, :] += jnp.dot(act, wo_ref[...], preferred_element_type=f32)
    _for_rows(tm, main)

    @pl.when(f == n_f - 1)
    def _():
        def epi(r0):
            rows = pl.ds(r0, ROW_CHUNK)
            ic = _is_ctx((ROW_CHUNK, 1), m * tm + r0, ctx_len)
            g2 = _mod_rows(mods_ref, 5, b, n_batch, ic, d)
            o_ref[rows, :] = z_ref[rows, :] + g2 * o_ref[rows, :]
        _for_rows(tm, epi)


def _ffn_call(z, mods, norm_g, w_in, w_out, ctx_len, tm, out_rows=None, row_off=0):
    n_batch, ltot, d = z.shape
    fh = w_out.shape[0]
    tf = 512
    n_f = fh // tf
    out_rows = ltot if out_rows is None else out_rows
    n_m = out_rows // tm
    m_off = row_off // tm
    kern = functools.partial(_ffn_kernel, tm=tm, d=d, ctx_len=ctx_len, n_batch=n_batch, n_f=n_f)
    return pl.pallas_call(
        kern,
        grid=(n_batch, n_m, n_f),
        in_specs=[
            pl.BlockSpec(mods.shape, lambda b, m, f: (0, 0)),
            pl.BlockSpec((1, d), lambda b, m, f: (0, 0)),
            pl.BlockSpec((None, tm, d), lambda b, m, f: (b, m + m_off, 0)),
            pl.BlockSpec((d, tf), lambda b, m, f: (0, f)),
            pl.BlockSpec((d, tf), lambda b, m, f: (0, f + n_f)),
            pl.BlockSpec((tf, d), lambda b, m, f: (f, 0)),
        ],
        out_specs=pl.BlockSpec((None, tm, d), lambda b, m, f: (b, m, 0)),
        out_shape=jax.ShapeDtypeStruct((n_batch, out_rows, d), f32),
        scratch_shapes=[pltpu.VMEM((tm, d), bf16)],
        compiler_params=_cparams(("parallel", "parallel", "arbitrary")),
        name="swiglu_ffn",
    )(mods, norm_g.reshape(1, d), z, w_in, w_in, w_out)


def _mm_kernel(x_ref, w_ref, o_ref):
    o_ref[...] = jnp.dot(x_ref[...], w_ref[...], preferred_element_type=f32).astype(o_ref.dtype)


def _mm_resid_kernel(mods_ref, x_ref, w_ref, z_ref, o_ref, *, tm, ctx_len, n_batch, row_off):
    b = pl.program_id(0)
    m = pl.program_id(1)
    ic = _is_ctx((tm, 1), m * tm + row_off, ctx_len)
    lat = mods_ref[pl.ds(b, 1), :]
    gate = lat if ic is None else jnp.where(ic, mods_ref[n_batch:n_batch + 1, :], lat)
    y = jnp.dot(x_ref[...], w_ref[...], preferred_element_type=f32)
    o_ref[...] = z_ref[...] + gate * y


def _mm_resid_call(x, w, z, mods, ctx_len, tm, chunk, row_off=0):
    n_batch, rows, k = x.shape
    d = w.shape[1]
    tn = 1024
    assert row_off % tm == 0 and rows % tm == 0
    m_off = row_off // tm
    g_off = chunk * d // tn
    kern = functools.partial(_mm_resid_kernel, tm=tm, ctx_len=ctx_len, n_batch=n_batch, row_off=row_off)
    return pl.pallas_call(
        kern,
        grid=(n_batch, rows // tm, d // tn),
        in_specs=[
            pl.BlockSpec((SUBLANES, tn), lambda b, m, n: (0, g_off + n)),
            pl.BlockSpec((None, tm, k), lambda b, m, n: (b, m, 0)),
            pl.BlockSpec((k, tn), lambda b, m, n: (0, n)),
            pl.BlockSpec((None, tm, tn), lambda b, m, n: (b, m + m_off, n)),
        ],
        out_specs=pl.BlockSpec((None, tm, tn), lambda b, m, n: (b, m, n)),
        out_shape=jax.ShapeDtypeStruct((n_batch, rows, d), f32),
        compiler_params=_cparams(("parallel", "parallel", "arbitrary")),
        name="proj_residual",
    )(mods, x, w, z)


def _rwkv_mix_kernel(mods_ref, g_ref, mu_ref, zc_ref, zp_ref, zn_ref, o_ref, ext_ref,
                     *, tq, d, ctx_len, n_tiles, n_batch):
    b = pl.program_id(0)
    t = pl.program_id(1)
    row0 = t * tq
    g = g_ref[...]

    def hmod(x, r0):
        ic = _is_ctx((x.shape[0], 1), r0, ctx_len)
        sh = _mod_rows(mods_ref, 0, b, n_batch, ic, d)
        sc = _mod_rows(mods_ref, 1, b, n_batch, ic, d)
        return _norm_mod(x, g, sh, sc)

    has_prev, has_next = _stream_edges(t, tq, ctx_len, n_tiles)
    hc = hmod(zc_ref[...], row0)
    ext_ref[0:POOL_HALO, :] = jnp.where(has_prev, hmod(zp_ref[...], row0 - POOL_HALO), 0.0)
    ext_ref[POOL_HALO:POOL_HALO + tq, :] = hc
    ext_ref[POOL_HALO + tq:, :] = jnp.where(has_next, hmod(zn_ref[...], row0 + tq), 0.0)
    up = ext_ref[POOL_HALO - 1:POOL_HALO - 1 + tq, :]
    dn = ext_ref[POOL_HALO + 1:POOL_HALO + 1 + tq, :]
    xx = 0.5 * (up + dn) - hc
    for mi in range(6):
        o_ref[mi] = (hc + xx * mu_ref[mi:mi + 1, :]).astype(bf16)


def _rwkv_mix_call(z, mods, norm_g, mu, ctx_len):
    n_batch, ltot, d = z.shape
    tq = 256
    n_tiles = ltot // tq
    cur, prev, nxt = _halo_specs(tq, d, n_tiles)
    kern = functools.partial(_rwkv_mix_kernel, tq=tq, d=d, ctx_len=ctx_len, n_tiles=n_tiles, n_batch=n_batch)
    return pl.pallas_call(
        kern,
        grid=(n_batch, n_tiles),
        in_specs=[
            pl.BlockSpec(mods.shape, lambda b, t: (0, 0)),
            pl.BlockSpec((1, d), lambda b, t: (0, 0)),
            pl.BlockSpec((SUBLANES, d), lambda b, t: (0, 0)),
            cur, prev, nxt,
        ],
        out_specs=pl.BlockSpec((6, None, tq, d), lambda b, t: (0, b, t, 0)),
        out_shape=jax.ShapeDtypeStruct((6, n_batch, ltot, d), bf16),
        scratch_shapes=[pltpu.VMEM((tq + 2 * POOL_HALO, d), f32)],
        compiler_params=_cparams(("parallel", "parallel")),
        name="rwkv_shift_mix",
    )(mods, norm_g.reshape(1, d), jnp.pad(mu, ((0, SUBLANES - mu.shape[0]), (0, 0))), z, z, z)


def _rkv_kernel(x_ref, w_ref, o_ref, *, tn):
    y = jnp.dot(x_ref[...], w_ref[...], preferred_element_type=f32)
    for gi in range(tn // LANES):
        o_ref[gi] = y[:, gi * LANES:(gi + 1) * LANES]


def _rkv_call(mixes, w_rkv, tm):
    _, n_batch, ltot, d = mixes.shape
    tn = 1024
    src = (0, 2, 3)

    def x_map(b, m, p, n):
        return (jnp.where(p == 0, src[0], jnp.where(p == 1, src[1], src[2])), b, m, 0)

    return pl.pallas_call(
        functools.partial(_rkv_kernel, tn=tn),
        grid=(n_batch, ltot // tm, 3, d // tn),
        in_specs=[
            pl.BlockSpec((None, None, tm, d), x_map),
            pl.BlockSpec((None, d, tn), lambda b, m, p, n: (p, 0, n)),
        ],
        out_specs=pl.BlockSpec((None, None, tn // LANES, tm, LANES), lambda b, m, p, n: (p, b, n, m, 0)),
        out_shape=jax.ShapeDtypeStruct((3, n_batch, d // LANES, ltot, LANES), f32),
        compiler_params=_cparams(("parallel", "parallel", "arbitrary", "arbitrary")),
        name="rwkv_rkv_proj",
    )(mixes, w_rkv)


def _lora_a_kernel(x_ref, w_ref, o_ref):
    o_ref[...] = jnp.dot(x_ref[...], w_ref[...], preferred_element_type=f32)


def _lora_a_call(mixes, w_a, tm):
    _, n_batch, ltot, d = mixes.shape
    nh = w_a.shape[2]
    src = (1, 4, 5)

    def x_map(b, m, p):
        return (jnp.where(p == 0, src[0], jnp.where(p == 1, src[1], src[2])), b, m, 0)

    return pl.pallas_call(
        _lora_a_kernel,
        grid=(n_batch, ltot // tm, 3),
        in_specs=[
            pl.BlockSpec((None, None, tm, d), x_map),
            pl.BlockSpec((None, d, nh), lambda b, m, p: (p, 0, 0)),
        ],
        out_specs=pl.BlockSpec((None, None, tm, nh), lambda b, m, p: (p, b, m, 0)),
        out_shape=jax.ShapeDtypeStruct((3, n_batch, ltot, nh), f32),
        compiler_params=_cparams(("parallel", "parallel", "arbitrary")),
        name="rwkv_lora_a",
    )(mixes, w_a)


def _lora_b_kernel(h_ref, wb_ref, ab_ref, gb_ref, bias_ref, o_ref, *, d):
    n_pairs = d // LANES
    hw = jnp.tanh(h_ref[0])
    ha = h_ref[1]
    hg = _sigmoid(h_ref[2])
    outs = []
    for di in range(2):
        outs.append(_dot(hw[:, di * LANES:(di + 1) * LANES], wb_ref[di]) + bias_ref[di:di + 1, :])
    for di in range(2):
        outs.append(_dot(ha[:, di * LANES:(di + 1) * LANES], ab_ref[di]) + bias_ref[2 + di:3 + di, :])
    outs.append(_dot(hg, gb_ref[...]))
    for oi, y in enumerate(outs):
        for p in range(n_pairs):
            o_ref[oi, p] = y[:, p * LANES:(p + 1) * LANES]


def _lora_b_call(h, w_b, a_b, g_b, bias, tm):
    _, n_batch, ltot, nh = h.shape
    d = g_b.shape[1]
    n_pairs = d // LANES
    return pl.pallas_call(
        functools.partial(_lora_b_kernel, d=d),
        grid=(n_batch, ltot // tm),
        in_specs=[
            pl.BlockSpec((3, None, tm, nh), lambda b, m: (0, b, m, 0)),
            pl.BlockSpec(w_b.shape, lambda b, m: (0, 0, 0)),
            pl.BlockSpec(a_b.shape, lambda b, m: (0, 0, 0)),
            pl.BlockSpec(g_b.shape, lambda b, m: (0, 0)),
            pl.BlockSpec(bias.shape, lambda b, m: (0, 0)),
        ],
        out_specs=pl.BlockSpec((5, None, n_pairs, tm, LANES), lambda b, m: (0, b, 0, m, 0)),
        out_shape=jax.ShapeDtypeStruct((5, n_batch, n_pairs, ltot, LANES), f32),
        compiler_params=_cparams(("parallel", "parallel")),
        name="rwkv_lora_b",
    )(h, w_b, a_b, g_b, bias)


def _seg_sum(x, lo):
    s_lo = jnp.sum(jnp.where(lo, x, 0.0), axis=1, keepdims=True)
    s_all = jnp.sum(x, axis=1, keepdims=True)
    return jnp.where(lo, s_lo, s_all - s_lo)


def _wkv_kernel(r_ref, k_ref, v_ref, w_ref, a_ref, par_ref, y_ref, bon_ref, s_ref,
                lhs_s, rhs1_s, rhs2_s, vbd_s, dec_s, mab_s, mak_s, arbk_s, tinv_s, tmp_s, w_s, g1r_s, uv_s,
                *, n_pairs, reverse):
    c = pl.program_id(1)
    L = WKV_CHUNK
    P = 2 * L

    @pl.when(c == 0)
    def _():
        s_ref[...] = jnp.zeros_like(s_ref)

    lane = lax.broadcasted_iota(jnp.int32, (L, LANES), 1)
    lo = lane < RWKV_HEAD
    ri = lax.broadcasted_iota(jnp.int32, (P, P), 0)
    ci = lax.broadcasted_iota(jnp.int32, (P, P), 1)
    same = (ri // L) == (ci // L)
    ii, jj = ri % L, ci % L
    if reverse:
        strict, incl = same & (jj > ii), same & (jj >= ii)
    else:
        strict, incl = same & (jj < ii), same & (jj <= ii)
    eye = (ri == ci).astype(f32)
    ti = lax.broadcasted_iota(jnp.int32, (L, L), 0)
    tj = lax.broadcasted_iota(jnp.int32, (L, L), 1)
    tri = ((tj >= ti) if reverse else (tj <= ti)).astype(bf16)
    diag_p = ((lax.broadcasted_iota(jnp.int32, (P, LANES), 0) // L)
              == (lax.broadcasted_iota(jnp.int32, (P, LANES), 1) // RWKV_HEAD))

    def blockdiag(x):
        return jnp.where(diag_p, jnp.concatenate([x, x], axis=0), 0.0)

    def stack(x):
        return jnp.concatenate([x, x], axis=0)

    pairs = range(n_pairs)

    for p in pairs:
        r, k, v = r_ref[p], k_ref[p], v_ref[p]
        k_k = par_ref[p, 0:1, :]
        k_a = par_ref[p, 1:2, :]
        r_k = par_ref[p, 2:3, :]
        lw = -jnp.exp(-_softplus(-w_ref[p]) - 0.5)
        a = _sigmoid(a_ref[p])
        kn = k * k_k
        kk = kn / jnp.maximum(jnp.sqrt(_seg_sum(kn * kn, lo)), 1e-12)
        kd = k * (1.0 + (a - 1.0) * k_a)
        bon_ref[p] = _seg_sum(r * kd * r_k, lo) * v

        lw_hi = lw.astype(bf16)
        lw_lo = (lw - lw_hi.astype(f32)).astype(bf16)
        cum = (jnp.dot(tri, lw_hi, preferred_element_type=f32)
               + jnp.dot(tri, lw_lo, preferred_element_type=f32))
        tot = cum[0:1, :] if reverse else cum[L - 1:L, :]
        e_neg = jnp.exp(-cum)
        e_end = jnp.exp(tot - cum)
        b_in = kk * a
        at = -kk * jnp.exp(cum - lw)
        rt = r * jnp.exp(cum)
        lhs_s[p] = jnp.concatenate([blockdiag(at), blockdiag(rt)], axis=0).astype(bf16)
        rhs1_s[p] = jnp.concatenate([stack(b_in * e_neg), stack(kd * e_neg)], axis=0).astype(bf16)
        rhs2_s[p] = jnp.concatenate([stack(b_in * e_end), stack(kd * e_end)], axis=0).astype(bf16)
        vbd_s[p] = blockdiag(v).astype(bf16)
        dec_s[p] = jnp.broadcast_to(jnp.exp(tot), (SUBLANES, LANES))

    for p in pairs:
        sc = lax.dot_general(lhs_s[p], rhs1_s[p], NT_DIMS, preferred_element_type=f32)
        m_ab = jnp.where(strict, sc[0:P, 0:P], 0.0)
        mab_s[p] = m_ab.astype(bf16)
        mak_s[p] = jnp.where(strict, sc[0:P, P:2 * P], 0.0).astype(bf16)
        arbk_s[p, :, 0:P] = jnp.where(incl, sc[P:2 * P, 0:P], 0.0).astype(bf16)
        arbk_s[p, :, P:2 * P] = jnp.where(incl, sc[P:2 * P, P:2 * P], 0.0).astype(bf16)
        tinv_s[p] = eye + jnp.where((ri // 2) == (ci // 2), m_ab, 0.0)

    s = 4
    while s <= L:
        level = ((ri // s) == (ci // s)) & ((ri // (s // 2)) != (ci // (s // 2)))
        for p in pairs:
            e = jnp.where(level, mab_s[p], jnp.zeros((P, P), bf16))
            tmp_s[p] = _dot(tinv_s[p], e).astype(bf16)
        for p in pairs:
            t_inv = tinv_s[p]
            tinv_s[p] = t_inv + _dot(tmp_s[p], t_inv)
        s *= 2

    for p in pairs:
        g1 = _dot(lhs_s[p], s_ref[p], NT_DIMS)
        z = jnp.dot(mak_s[p], vbd_s[p], preferred_element_type=f32)
        w_s[p] = (g1[0:P] + z).astype(bf16)
        g1r_s[p] = g1[P:2 * P]

    for p in pairs:
        uv_s[p, 0:P, :] = _dot(tinv_s[p], w_s[p]).astype(bf16)
        uv_s[p, P:2 * P, :] = vbd_s[p]

    for p in pairs:
        y = g1r_s[p] + jnp.dot(arbk_s[p], uv_s[p], preferred_element_type=f32)
        y_ref[p] = y[0:L] + y[L:P]
        upd = lax.dot_general(uv_s[p], rhs2_s[p], TN_DIMS, preferred_element_type=f32)
        s_ref[p] = s_ref[p] * dec_s[p, 0:1, :] + jnp.where(diag_p, upd, 0.0)


def _wkv_call(rkv, pre, params, ctx_len, direction):
    _, n_batch, n_pairs, ltot, _ = rkv.shape
    L = WKV_CHUNK
    P = 2 * L
    n_chunks = ltot // L
    ctx_chunks = ctx_len // L
    reverse = direction == 1

    def chunk_of(c):
        if not reverse:
            return c
        return jnp.where(c < ctx_chunks, ctx_chunks - 1 - c, n_chunks - 1 - (c - ctx_chunks))

    def spec(lead):
        return pl.BlockSpec((None, None, n_pairs, L, LANES), lambda b, c: (lead, b, 0, chunk_of(c), 0))

    out_spec = pl.BlockSpec((None, n_pairs, L, LANES), lambda b, c: (b, 0, chunk_of(c), 0))
    out_sds = jax.ShapeDtypeStruct((n_batch, n_pairs, ltot, LANES), f32)
    return pl.pallas_call(
        functools.partial(_wkv_kernel, n_pairs=n_pairs, reverse=reverse),
        grid=(n_batch, n_chunks),
        in_specs=[spec(0), spec(1), spec(2),
                  pl.BlockSpec((None, None, n_pairs, L, LANES), lambda b, c: (direction, b, 0, chunk_of(c), 0)),
                  pl.BlockSpec((None, None, n_pairs, L, LANES), lambda b, c: (2 + direction, b, 0, chunk_of(c), 0)),
                  pl.BlockSpec((None, n_pairs, SUBLANES, LANES), lambda b, c: (direction, 0, 0, 0))],
        out_specs=[out_spec, out_spec],
        out_shape=[out_sds, out_sds],
        scratch_shapes=[
            pltpu.VMEM((n_pairs, P, LANES), f32),
            pltpu.VMEM((n_pairs, 2 * P, LANES), bf16),
            pltpu.VMEM((n_pairs, 2 * P, LANES), bf16),
            pltpu.VMEM((n_pairs, 2 * P, LANES), bf16),
            pltpu.VMEM((n_pairs, P, LANES), bf16),
            pltpu.VMEM((n_pairs, SUBLANES, LANES), f32),
            pltpu.VMEM((n_pairs, P, P), bf16),
            pltpu.VMEM((n_pairs, P, P), bf16),
            pltpu.VMEM((n_pairs, P, 2 * P), bf16),
            pltpu.VMEM((n_pairs, P, P), f32),
            pltpu.VMEM((n_pairs, P, P), bf16),
            pltpu.VMEM((n_pairs, P, LANES), bf16),
            pltpu.VMEM((n_pairs, P, LANES), f32),
            pltpu.VMEM((n_pairs, 2 * P, LANES), bf16),
        ],
        compiler_params=_cparams(("parallel", "arbitrary")),
        name="rwkv_wkv_fwd" if not reverse else "rwkv_wkv_bwd",
    )(rkv, rkv, rkv, pre, pre, params)


def _rwkv_out_kernel(y0_ref, y1_ref, b0_ref, b1_ref, g_ref, ln_ref, o_ref, *, n_pairs):
    tq = y0_ref.shape[1]
    lo = lax.broadcasted_iota(jnp.int32, (tq, LANES), 1) < RWKV_HEAD
    inv_n = 1.0 / RWKV_HEAD
    for p in range(n_pairs):
        y = y0_ref[p] + y1_ref[p]
        mean = _seg_sum(y, lo) * inv_n
        yc = y - mean
        var = _seg_sum(yc * yc, lo) * inv_n
        yn = yc * lax.rsqrt(var + LN_X_EPS)
        sl = slice(p * LANES, (p + 1) * LANES)
        o = yn * ln_ref[0:1, sl] + ln_ref[1:2, sl] + b0_ref[p] + b1_ref[p]
        o_ref[:, sl] = (o * g_ref[p]).astype(bf16)


def _rwkv_out_call(y0, y1, bon0, bon1, pre, ln_x):
    n_batch, n_pairs, ltot, _ = y0.shape
    d = n_pairs * LANES
    tq = 256
    spec = pl.BlockSpec((None, n_pairs, tq, LANES), lambda b, t: (b, 0, t, 0))
    return pl.pallas_call(
        functools.partial(_rwkv_out_kernel, n_pairs=n_pairs),
        grid=(n_batch, ltot // tq),
        in_specs=[spec, spec, spec, spec,
                  pl.BlockSpec((None, None, n_pairs, tq, LANES), lambda b, t: (4, b, 0, t, 0)),
                  pl.BlockSpec((SUBLANES, d), lambda b, t: (0, 0))],
        out_specs=pl.BlockSpec((None, tq, d), lambda b, t: (b, t, 0)),
        out_shape=jax.ShapeDtypeStruct((n_batch, ltot, d), bf16),
        compiler_params=_cparams(("parallel", "parallel")),
        name="rwkv_groupnorm_gate",
    )(y0, y1, bon0, bon1, pre, jnp.pad(ln_x, ((0, SUBLANES - ln_x.shape[0]), (0, 0))))


def _rwkv_layer(z, mods, norm_g, ctx_len, mu, w_rkv, w_o, dir_vec, w_la, w_lb, a_la, a_lb, g_la, g_lb, r_k, ln_x):
    n_batch, ltot, d = z.shape
    n_pairs = d // LANES
    tm = 768 if ltot % 768 == 0 else 512
    mixes = _rwkv_mix_call(z, mods, norm_g, mu, ctx_len)
    rkv = _rkv_call(mixes, w_rkv.astype(bf16), tm)

    def pad_cols(w):
        return jnp.concatenate([jnp.pad(w[i], ((0, 0), (0, LANES - w.shape[2]))) for i in range(2)], axis=1)

    def pad_rows(w):
        return jnp.pad(w, ((0, 0), (0, LANES - w.shape[1]), (0, 0)))

    w_a = jnp.stack([pad_cols(w_la), pad_cols(a_la), g_la]).astype(bf16)
    h = _lora_a_call(mixes, w_a, tm)
    bias = jnp.pad(jnp.stack([dir_vec[0, 0], dir_vec[1, 0], dir_vec[0, 1], dir_vec[1, 1]]), ((0, 4), (0, 0)))
    pre = _lora_b_call(h, pad_rows(w_lb).astype(bf16), pad_rows(a_lb).astype(bf16), g_lb.astype(bf16), bias, 256)
    rk_row = r_k.reshape(d)
    params = jnp.stack([jnp.stack([dir_vec[di, 2], dir_vec[di, 3], rk_row]) for di in range(2)])
    params = jnp.pad(params, ((0, 0), (0, SUBLANES - 3), (0, 0)))
    params = params.reshape(2, SUBLANES, n_pairs, LANES).transpose(0, 2, 1, 3)
    y0, bon0 = _wkv_call(rkv, pre, params, ctx_len, 0)
    y1, bon1 = _wkv_call(rkv, pre, params, ctx_len, 1)
    og = _rwkv_out_call(y0, y1, bon0, bon1, pre, ln_x)
    return _mm_resid_call(og, w_o.astype(bf16), z, mods, ctx_len, tm, chunk=2)


def _rope_tables(ctx_len, seq_len):
    rows = seq_len // GRID_W
    row = jnp.repeat(jnp.arange(rows, dtype=f32), GRID_W)
    col = jnp.tile(jnp.arange(GRID_W, dtype=f32), rows)
    n_freq = DIFF_HEAD // 4
    inv = ROPE_BASE ** (-jnp.arange(n_freq, dtype=f32) / n_freq)
    ang = jnp.concatenate([row[:, None] * inv, col[:, None] * inv], axis=-1)
    cos = jnp.repeat(jnp.cos(ang), 2, axis=1)
    sin = jnp.repeat(jnp.sin(ang), 2, axis=1) * jnp.tile(jnp.array([-1.0, 1.0], f32), DIFF_HEAD // 2)
    cos = jnp.concatenate([jnp.ones((ctx_len, DIFF_HEAD), f32), cos], axis=0)
    sin = jnp.concatenate([jnp.zeros((ctx_len, DIFF_HEAD), f32), sin], axis=0)
    return cos, sin


def _qkv_kernel(mods_ref, g_ref, qkg_ref, cos_ref, sin_ref, z_ref, w_ref, o_ref, vt_ref, h_ref,
                *, tm, tn, d, ctx_len, n_batch):
    b = pl.program_id(0)
    m = pl.program_id(1)
    n = pl.program_id(2)
    n_reg = d // tn

    @pl.when(n == 0)
    def _():
        def pro(r0):
            rows = pl.ds(r0, ROW_CHUNK)
            ic = _is_ctx((ROW_CHUNK, 1), m * tm + r0, ctx_len)
            sh = _mod_rows(mods_ref, 0, b, n_batch, ic, d)
            sc = _mod_rows(mods_ref, 1, b, n_batch, ic, d)
            h_ref[rows, :] = _norm_mod(z_ref[rows, :], g_ref[...], sh, sc).astype(bf16)
        _for_rows(tm, pro)

    y = jnp.dot(h_ref[...], w_ref[...], preferred_element_type=f32)

    def qk_epilogue(gain, scale):
        cos = cos_ref[...]
        sin = sin_ref[...]
        even = (lax.broadcasted_iota(jnp.int32, (tm, LANES), 1) % 2) == 0
        for gi in range(tn // LANES):
            x = y[:, gi * LANES:(gi + 1) * LANES]
            ms = jnp.mean(x * x, axis=-1, keepdims=True)
            x = x * lax.rsqrt(ms + EPS) * gain
            partner = jnp.where(even, pltpu.roll(x, LANES - 1, 1), pltpu.roll(x, 1, 1))
            x = x * cos + partner * sin
            o_ref[:, gi * LANES:(gi + 1) * LANES] = (x * scale).astype(bf16)

    @pl.when(n < n_reg)
    def _():
        qk_epilogue(qkg_ref[0:1, :], DIFF_HEAD ** -0.5 * LOG2E)

    @pl.when((n >= n_reg) & (n < 2 * n_reg))
    def _():
        qk_epilogue(qkg_ref[1:2, :], 1.0)

    @pl.when(n >= 2 * n_reg)
    def _():
        vt_ref[...] = y.T.astype(bf16)


def _qkv_call(z, mods, norm_g, w_qkv, qk_g, cos, sin, ctx_len, tm):
    n_batch, ltot, d = z.shape
    tn = 512
    n_reg = d // tn
    kern = functools.partial(_qkv_kernel, tm=tm, tn=tn, d=d, ctx_len=ctx_len, n_batch=n_batch)
    return pl.pallas_call(
        kern,
        grid=(n_batch, ltot // tm, 3 * d // tn),
        in_specs=[
            pl.BlockSpec(mods.shape, lambda b, m, n: (0, 0)),
            pl.BlockSpec((1, d), lambda b, m, n: (0, 0)),
            pl.BlockSpec((SUBLANES, DIFF_HEAD), lambda b, m, n: (0, 0)),
            pl.BlockSpec((tm, DIFF_HEAD), lambda b, m, n: (m, 0)),
            pl.BlockSpec((tm, DIFF_HEAD), lambda b, m, n: (m, 0)),
            pl.BlockSpec((None, tm, d), lambda b, m, n: (b, m, 0)),
            pl.BlockSpec((d, tn), lambda b, m, n: (0, n)),
        ],
        out_specs=[
            pl.BlockSpec((None, tm, tn), lambda b, m, n: (b, m, jnp.minimum(n, 2 * n_reg - 1))),
            pl.BlockSpec((None, None, tn, tm), lambda b, m, n: (b, m, jnp.maximum(n - 2 * n_reg, 0), 0)),
        ],
        out_shape=[jax.ShapeDtypeStruct((n_batch, ltot, 2 * d), bf16),
                   jax.ShapeDtypeStruct((n_batch, ltot // tm, d, tm), bf16)],
        scratch_shapes=[pltpu.VMEM((tm, d), bf16)],
        compiler_params=_cparams(("parallel", "parallel", "arbitrary")),
        name="diff_qkv_proj",
    )(mods, norm_g.reshape(1, d), jnp.pad(qk_g, ((0, SUBLANES - qk_g.shape[0]), (0, 0))), cos, sin, z, w_qkv)


def _attn_kernel(lam_ref, sg_ref, q_ref, k_ref, vt_ref, o_ref, m_s, l_s, acc_s, sa_s, sb_s, *, lambda_init):
    tq = q_ref.shape[0]
    n_k, _, tk = vt_ref.shape
    hd = DIFF_HEAD
    q = q_ref[...]
    m_s[...] = jnp.full(m_s.shape, -jnp.inf, f32)
    l_s[...] = jnp.zeros(l_s.shape, f32)
    acc_s[...] = jnp.zeros(acc_s.shape, f32)

    def scores(j, s_buf):
        kb = k_ref[pl.ds(pl.multiple_of(j * tk, tk), tk), :]
        for i in range(2):
            s_buf[i] = lax.dot_general(kb[:, i * hd:(i + 1) * hd], q[:, i * hd:(i + 1) * hd], NT_DIMS,
                                       preferred_element_type=f32)

    def absorb(j, s_buf):
        ps = []
        for i in range(2):
            s = s_buf[i]
            m_old = m_s[i, 0:1, :]
            m_new = jnp.maximum(m_old, jnp.max(s, axis=0, keepdims=True))
            alpha = jnp.exp2(m_old - m_new)
            p = jnp.exp2(s - m_new)
            l_s[i, 0:1, :] = alpha * l_s[i, 0:1, :] + jnp.sum(p, axis=0, keepdims=True)
            m_s[i, 0:1, :] = m_new
            ps.append((alpha, p.astype(bf16)))
        vt = vt_ref[j]
        for i, (alpha, p) in enumerate(ps):
            acc_s[i] = alpha * acc_s[i] + jnp.dot(vt, p, preferred_element_type=f32)

    scores(0, sa_s)

    def two_blocks(jj, carry):
        j = 2 * jj
        scores(j + 1, sb_s)
        absorb(j, sa_s)
        scores(j + 2, sa_s)
        absorb(j + 1, sb_s)
        return carry

    lax.fori_loop(0, (n_k - 1) // 2, two_blocks, 0)
    if n_k % 2 == 1:
        absorb(n_k - 1, sa_s)
    else:
        scores(n_k - 1, sb_s)
        absorb(n_k - 2, sa_s)
        absorb(n_k - 1, sb_s)
    lv = lam_ref[...]
    lam = (jnp.exp(jnp.sum(lv[0:1] * lv[1:2], axis=1, keepdims=True))
           - jnp.exp(jnp.sum(lv[2:3] * lv[3:4], axis=1, keepdims=True)) + lambda_init)
    o = acc_s[0] / l_s[0, 0:1, :] - lam * (acc_s[1] / l_s[1, 0:1, :])
    ms = jnp.mean(o * o, axis=0, keepdims=True)
    o = o * lax.rsqrt(ms + EPS) * sg_ref[...]
    o_ref[...] = o.T.astype(bf16)


def _attn_call(qk, vt, lam_vec, subln_g, ctx_len, lambda_init):
    n_batch, ltot, d2 = qk.shape
    d = d2 // 2
    _, n_kb, _, tk = vt.shape
    hw = 2 * DIFF_HEAD
    n_heads = d // hw
    seq_len = ltot - ctx_len
    tq = 256
    assert ctx_len % tq == 0 and seq_len % tq == 0
    q_off = ctx_len // tq
    gain = jnp.broadcast_to((subln_g * (1.0 - lambda_init)).reshape(hw, 1), (hw, tq))
    kern = functools.partial(_attn_kernel, lambda_init=lambda_init)
    return pl.pallas_call(
        kern,
        grid=(n_batch, n_heads, seq_len // tq),
        in_specs=[
            pl.BlockSpec((SUBLANES, DIFF_HEAD), lambda b, h, t: (0, 0)),
            pl.BlockSpec((hw, tq), lambda b, h, t: (0, 0)),
            pl.BlockSpec((None, tq, hw), lambda b, h, t: (b, t + q_off, h)),
            pl.BlockSpec((None, ltot, hw), lambda b, h, t: (b, 0, n_heads + h)),
            pl.BlockSpec((None, n_kb, hw, tk), lambda b, h, t: (b, 0, h, 0)),
        ],
        out_specs=pl.BlockSpec((None, tq, hw), lambda b, h, t: (b, t, h)),
        out_shape=jax.ShapeDtypeStruct((n_batch, seq_len, d), bf16),
        scratch_shapes=[pltpu.VMEM((2, SUBLANES, tq), f32), pltpu.VMEM((2, SUBLANES, tq), f32),
                        pltpu.VMEM((2, hw, tq), f32),
                        pltpu.VMEM((2, tk, tq), f32), pltpu.VMEM((2, tk, tq), f32)],
        compiler_params=_cparams(("parallel", "parallel", "parallel")),
        name="diff_attention",
    )(jnp.pad(lam_vec, ((0, SUBLANES - lam_vec.shape[0]), (0, 0))), gain, qk, qk, vt)


def _diff_layer(z, mods, norm_g, ctx_len, w_qkv, w_o, qk_g, lam_vec, subln_g, lambda_init):
    n_batch, ltot, d = z.shape
    seq_len = ltot - ctx_len
    tm = 768 if ltot % 768 == 0 else 512
    cos, sin = _rope_tables(ctx_len, seq_len)
    qk, vt = _qkv_call(z, mods, norm_g, w_qkv.astype(bf16), qk_g, cos, sin, ctx_len, tm)
    o = _attn_call(qk, vt, lam_vec, subln_g, ctx_len, lambda_init)
    return _mm_resid_call(o, w_o.astype(bf16), z, mods, 0, 256, chunk=2, row_off=ctx_len)


def kernel(x, c, ctx, c_ctx, ada_w, ada_b, norm_g, ffn_w_in, ffn_w_out, pool_w, pool_scale, rwkv_mu, rwkv_w_rkv, rwkv_w_o, rwkv_dir_vec, rwkv_w_lora_a, rwkv_w_lora_b, rwkv_a_lora_a, rwkv_a_lora_b, rwkv_g_lora_a, rwkv_g_lora_b, rwkv_r_k, rwkv_ln_x, diff_w_qkv, diff_w_o, diff_qk_g, diff_lambda, diff_subln_g):
    n_batch, seq_len, d = x.shape
    depth = ada_w.shape[0]
    ctx_len = ctx.shape[1]
    assert n_batch + 1 <= SUBLANES
    cs = jnp.concatenate([c, c_ctx[None, :], jnp.zeros((SUBLANES - n_batch - 1, d), f32)], axis=0)
    mods_all = _mods_call(cs, ada_w, ada_b)

    last_reader = max((i for i in range(depth) if i % N_MIXERS != 0), default=-1)
    z = jnp.concatenate([ctx, x], axis=1) if last_reader >= 0 else x
    cur_ctx = ctx_len if last_reader >= 0 else 0
    for i in range(depth):
        kind, j = i % N_MIXERS, i // N_MIXERS
        mods = mods_all[i]
        if kind == 0:
            z = _pool_call(z, mods, norm_g[i, 0], pool_w[j], pool_scale[j], cur_ctx)
        elif kind == 1:
            z = _rwkv_layer(z, mods, norm_g[i, 0], cur_ctx, rwkv_mu[j], rwkv_w_rkv[j], rwkv_w_o[j], rwkv_dir_vec[j],
                            rwkv_w_lora_a[j], rwkv_w_lora_b[j], rwkv_a_lora_a[j], rwkv_a_lora_b[j],
                            rwkv_g_lora_a[j], rwkv_g_lora_b[j], rwkv_r_k[j], rwkv_ln_x[j])
        else:
            lambda_init = 0.8 - 0.6 * math.exp(-0.3 * i)
            z = _diff_layer(z, mods, norm_g[i, 0], cur_ctx, diff_w_qkv[j], diff_w_o[j], diff_qk_g[j],
                            diff_lambda[j], diff_subln_g[j], lambda_init)
            cur_ctx = 0
        if cur_ctx and i >= last_reader:
            z = z[:, cur_ctx:]
            cur_ctx = 0
        ltot = z.shape[1]
        tm = 768 if ltot % 768 == 0 else 512
        z = _ffn_call(z, mods, norm_g[i, 1], ffn_w_in[i].astype(bf16), ffn_w_out[i].astype(bf16), cur_ctx, tm)
    return z[:, cur_ctx:] if cur_ctx else z
```

```python
import functools
import math

import jax
import jax.numpy as jnp
from jax import lax
from jax.experimental import pallas as pl
from jax.experimental.pallas import tpu as pltpu

f32 = jnp.float32
bf16 = jnp.bfloat16

N_MIXERS = 3
EPS = 1e-6
POOL_WINDOWS = (2, 4, 8, 16)
POOL_HALO = 8
RWKV_HEAD = 64
LN_X_EPS = 64e-5
DIFF_HEAD = 128
ROPE_BASE = 10000.0
LOG2E = 1.4426950408889634
GRID_W = 64
LANES = 128
SUBLANES = 8
WKV_CHUNK = 64
ROW_CHUNK = 256
NORM_ROWS = 16
ATTN_GROUP = 2
VMEM_LIMIT = 56 * 1024 * 1024

NT_DIMS = (((1,), (1,)), ((), ()))
TN_DIMS = (((0,), (0,)), ((), ()))


def _cparams(sem):
    return pltpu.CompilerParams(dimension_semantics=sem, vmem_limit_bytes=VMEM_LIMIT)


def _dot(a, b, dims=None):
    a = a.astype(bf16)
    b = b.astype(bf16)
    if dims is None:
        return jnp.dot(a, b, preferred_element_type=f32)
    return lax.dot_general(a, b, dims, preferred_element_type=f32)


def _sigmoid(x):
    return 1.0 / (1.0 + jnp.exp(-x))


def _softplus(x):
    return jnp.maximum(x, 0.0) + jnp.log(1.0 + jnp.exp(-jnp.abs(x)))


def _row_ids(shape, row0):
    return lax.broadcasted_iota(jnp.int32, shape, 0) + row0


def _mod_rows(mods_ref, chunk, b, n_batch, is_ctx, d):
    lat = mods_ref[pl.ds(b, 1), chunk * d:(chunk + 1) * d]
    if is_ctx is None:
        return lat
    ctx = mods_ref[n_batch:n_batch + 1, chunk * d:(chunk + 1) * d]
    return jnp.where(is_ctx, ctx, lat)


def _norm_mod(x, g, shift, scale):
    ms = jnp.mean(x * x, axis=-1, keepdims=True)
    h = x * lax.rsqrt(ms + EPS) * g
    return h * (1.0 + scale) + shift


def _is_ctx(shape, row0, ctx_len):
    if ctx_len == 0:
        return None
    return _row_ids(shape, row0) < ctx_len


def _mods_kernel(s_ref, w_ref, b_ref, o_ref):
    s = s_ref[...]
    s = s * _sigmoid(s)
    o_ref[...] = _dot(s, w_ref[...]) + b_ref[...]


def _mods_call(cs, ada_w, ada_b):
    depth, d, n6 = ada_w.shape
    tn = 1024
    return pl.pallas_call(
        _mods_kernel,
        grid=(depth, n6 // tn),
        in_specs=[
            pl.BlockSpec((SUBLANES, d), lambda l, n: (0, 0)),
            pl.BlockSpec((None, d, tn), lambda l, n: (l, 0, n)),
            pl.BlockSpec((None, 1, tn), lambda l, n: (l, 0, n)),
        ],
        out_specs=pl.BlockSpec((None, SUBLANES, tn), lambda l, n: (l, 0, n)),
        out_shape=jax.ShapeDtypeStruct((depth, SUBLANES, n6), f32),
        compiler_params=_cparams(("parallel", "parallel")),
        name="adaln_mods",
    )(cs, ada_w, ada_b.reshape(depth, 1, n6))


def _halo_specs(tq, d, n_tiles):
    per = tq // POOL_HALO
    last = n_tiles * per - 1
    cur = pl.BlockSpec((None, tq, d), lambda b, t: (b, t, 0))
    prev = pl.BlockSpec((None, POOL_HALO, d), lambda b, t: (b, jnp.maximum(t * per - 1, 0), 0))
    nxt = pl.BlockSpec((None, POOL_HALO, d), lambda b, t: (b, jnp.minimum((t + 1) * per, last), 0))
    return cur, prev, nxt


def _stream_edges(t, tq, ctx_len, n_tiles):
    ctx_tiles = ctx_len // tq
    first = t == 0
    last = t == n_tiles - 1
    if ctx_tiles:
        first = first | (t == ctx_tiles)
        last = last | (t == ctx_tiles - 1)
    return jnp.logical_not(first), jnp.logical_not(last)


def _pool_kernel(mods_ref, g_ref, w_ref, ls_ref, zc_ref, zp_ref, zn_ref, o_ref, ext_ref,
                 *, tq, d, ctx_len, n_tiles, n_batch, seq_len):
    b = pl.program_id(0)
    t = pl.program_id(1)
    row0 = t * tq
    g = g_ref[...]

    def hmod(x, r0):
        ic = _is_ctx((x.shape[0], 1), r0, ctx_len)
        sh = _mod_rows(mods_ref, 0, b, n_batch, ic, d)
        sc = _mod_rows(mods_ref, 1, b, n_batch, ic, d)
        return _norm_mod(x, g, sh, sc)

    has_prev, has_next = _stream_edges(t, tq, ctx_len, n_tiles)
    zc = zc_ref[...]
    hc = hmod(zc, row0)
    hp = jnp.where(has_prev, hmod(zp_ref[...], row0 - POOL_HALO), 0.0)
    hn = jnp.where(has_next, hmod(zn_ref[...], row0 + tq), 0.0)
    ext_ref[0:POOL_HALO, :] = hp
    ext_ref[POOL_HALO:POOL_HALO + tq, :] = hc
    ext_ref[POOL_HALO + tq:, :] = hn

    rows = _row_ids((tq, 1), row0)
    if ctx_len:
        in_ctx = rows < ctx_len
        pos = jnp.where(in_ctx, rows, rows - ctx_len)
        slen = jnp.where(in_ctx, ctx_len, seq_len)
    else:
        pos, slen = rows, seq_len

    cg = d // len(POOL_WINDOWS)
    ys = []
    for gi, win in enumerate(POOL_WINDOWS):
        lo_off, hi_off = win // 2, win - win // 2
        c0 = gi * cg
        acc = None
        for off in range(-lo_off, hi_off):
            piece = ext_ref[POOL_HALO + off:POOL_HALO + off + tq, c0:c0 + cg]
            acc = piece if acc is None else acc + piece
        cnt = jnp.minimum(pos + hi_off, slen) - jnp.maximum(pos - lo_off, 0)
        p = acc / cnt.astype(f32) - hc[:, c0:c0 + cg]
        ys.append(_dot(p, w_ref[gi]))
    y = jnp.concatenate(ys, axis=1) * ls_ref[...]
    ic = _is_ctx((tq, 1), row0, ctx_len)
    gate = _mod_rows(mods_ref, 2, b, n_batch, ic, d)
    o_ref[...] = zc + gate * y


def _pool_call(z, mods, norm_g, pool_w, pool_scale, ctx_len):
    n_batch, ltot, d = z.shape
    tq = 256
    n_tiles = ltot // tq
    cur, prev, nxt = _halo_specs(tq, d, n_tiles)
    ng, cg, _ = pool_w.shape
    kern = functools.partial(_pool_kernel, tq=tq, d=d, ctx_len=ctx_len, n_tiles=n_tiles,
                             n_batch=n_batch, seq_len=ltot - ctx_len)
    return pl.pallas_call(
        kern,
        grid=(n_batch, n_tiles),
        in_specs=[
            pl.BlockSpec(mods.shape, lambda b, t: (0, 0)),
            pl.BlockSpec((1, d), lambda b, t: (0, 0)),
            pl.BlockSpec((ng, cg, cg), lambda b, t: (0, 0, 0)),
            pl.BlockSpec((1, d), lambda b, t: (0, 0)),
            cur, prev, nxt,
        ],
        out_specs=pl.BlockSpec((None, tq, d), lambda b, t: (b, t, 0)),
        out_shape=jax.ShapeDtypeStruct(z.shape, f32),
        scratch_shapes=[pltpu.VMEM((tq + 2 * POOL_HALO, d), f32)],
        compiler_params=_cparams(("parallel", "parallel")),
        name="pool_mix",
    )(mods, norm_g.reshape(1, d), pool_w.astype(bf16), pool_scale.reshape(1, d), z, z, z)


def _for_rows(n_rows, body):
    def step(i, carry):
        body(pl.multiple_of(i * ROW_CHUNK, ROW_CHUNK))
        return carry
    lax.fori_loop(0, n_rows // ROW_CHUNK, step, 0)


def _norm_mod_tile(z_ref, h_ref, mods_ref, g_ref, chunk, b, row0, tm, d, ctx_len, n_batch):
    assert ctx_len % NORM_ROWS == 0 and tm % (4 * NORM_ROWS) == 0

    def step(i, carry):
        r0 = pl.multiple_of(i * NORM_ROWS, NORM_ROWS)
        rows = pl.ds(r0, NORM_ROWS)
        ic = None if ctx_len == 0 else (row0 + r0) < ctx_len
        sh = _mod_rows(mods_ref, chunk, b, n_batch, ic, d)
        sc = _mod_rows(mods_ref, chunk + 1, b, n_batch, ic, d)
        h_ref[rows, :] = _norm_mod(z_ref[rows, :], g_ref[...], sh, sc).astype(bf16)
        return carry
    lax.fori_loop(0, tm // NORM_ROWS, step, 0, unroll=4)


def _ffn_kernel(mods_ref, g_ref, z_ref, wg_ref, wu_ref, wo_ref, o_ref, h_ref,
                *, tm, d, ctx_len, n_batch, n_f):
    b = pl.program_id(0)
    m = pl.program_id(1)
    f = pl.program_id(2)

    @pl.when(f == 0)
    def _():
        _norm_mod_tile(z_ref, h_ref, mods_ref, g_ref, 3, b, m * tm, tm, d, ctx_len, n_batch)
        o_ref[...] = jnp.zeros((tm, d), f32)

    def gate_up(c):
        h = h_ref[c * ROW_CHUNK:(c + 1) * ROW_CHUNK, :]
        return (jnp.dot(h, wg_ref[...], preferred_element_type=f32),
                jnp.dot(h, wu_ref[...], preferred_element_type=f32))

    n_slabs = tm // ROW_CHUNK
    nxt = gate_up(0)
    for c in range(n_slabs):
        gate, up = nxt
        if c + 1 < n_slabs:
            nxt = gate_up(c + 1)
        act = (gate * _sigmoid(gate) * up).astype(bf16)
        o_ref[c * ROW_CHUNK:(c + 1) * ROW_CHUNK, :] += jnp.dot(act, wo_ref[...], preferred_element_type=f32)

    @pl.when(f == n_f - 1)
    def _():
        def epi(r0):
            rows = pl.ds(r0, ROW_CHUNK)
            ic = _is_ctx((ROW_CHUNK, 1), m * tm + r0, ctx_len)
            g2 = _mod_rows(mods_ref, 5, b, n_batch, ic, d)
            o_ref[rows, :] = z_ref[rows, :] + g2 * o_ref[rows, :]
        _for_rows(tm, epi)


def _ffn_call(z, mods, norm_g, w_in, w_out, ctx_len, tm, out_rows=None, row_off=0):
    n_batch, ltot, d = z.shape
    fh = w_out.shape[0]
    tf = 512
    n_f = fh // tf
    out_rows = ltot if out_rows is None else out_rows
    n_m = out_rows // tm
    m_off = row_off // tm
    kern = functools.partial(_ffn_kernel, tm=tm, d=d, ctx_len=ctx_len, n_batch=n_batch, n_f=n_f)
    return pl.pallas_call(
        kern,
        grid=(n_batch, n_m, n_f),
        in_specs=[
            pl.BlockSpec(mods.shape, lambda b, m, f: (0, 0)),
            pl.BlockSpec((1, d), lambda b, m, f: (0, 0)),
            pl.BlockSpec((None, tm, d), lambda b, m, f: (b, m + m_off, 0)),
            pl.BlockSpec((d, tf), lambda b, m, f: (0, f)),
            pl.BlockSpec((d, tf), lambda b, m, f: (0, f + n_f)),
            pl.BlockSpec((tf, d), lambda b, m, f: (f, 0)),
        ],
        out_specs=pl.BlockSpec((None, tm, d), lambda b, m, f: (b, m, 0)),
        out_shape=jax.ShapeDtypeStruct((n_batch, out_rows, d), f32),
        scratch_shapes=[pltpu.VMEM((tm, d), bf16)],
        compiler_params=_cparams(("parallel", "parallel", "arbitrary")),
        name="swiglu_ffn",
    )(mods, norm_g.reshape(1, d), z, w_in, w_in, w_out)


def _mm_kernel(x_ref, w_ref, o_ref):
    o_ref[...] = jnp.dot(x_ref[...], w_ref[...], preferred_element_type=f32).astype(o_ref.dtype)


def _mm_resid_kernel(mods_ref, x_ref, w_ref, z_ref, o_ref, *, tm, ctx_len, n_batch, row_off):
    b = pl.program_id(0)
    m = pl.program_id(1)
    ic = _is_ctx((tm, 1), m * tm + row_off, ctx_len)
    lat = mods_ref[pl.ds(b, 1), :]
    gate = lat if ic is None else jnp.where(ic, mods_ref[n_batch:n_batch + 1, :], lat)
    y = jnp.dot(x_ref[...], w_ref[...], preferred_element_type=f32)
    o_ref[...] = z_ref[...] + gate * y


def _mm_resid_call(x, w, z, mods, ctx_len, tm, chunk, row_off=0):
    n_batch, rows, k = x.shape
    d = w.shape[1]
    tn = 1024
    assert row_off % tm == 0 and rows % tm == 0
    m_off = row_off // tm
    g_off = chunk * d // tn
    kern = functools.partial(_mm_resid_kernel, tm=tm, ctx_len=ctx_len, n_batch=n_batch, row_off=row_off)
    return pl.pallas_call(
        kern,
        grid=(n_batch, rows // tm, d // tn),
        in_specs=[
            pl.BlockSpec((SUBLANES, tn), lambda b, m, n: (0, g_off + n)),
            pl.BlockSpec((None, tm, k), lambda b, m, n: (b, m, 0)),
            pl.BlockSpec((k, tn), lambda b, m, n: (0, n)),
            pl.BlockSpec((None, tm, tn), lambda b, m, n: (b, m + m_off, n)),
        ],
        out_specs=pl.BlockSpec((None, tm, tn), lambda b, m, n: (b, m, n)),
        out_shape=jax.ShapeDtypeStruct((n_batch, rows, d), f32),
        compiler_params=_cparams(("parallel", "parallel", "arbitrary")),
        name="proj_residual",
    )(mods, x, w, z)


def _rwkv_mix_kernel(mods_ref, g_ref, mu_ref, zc_ref, zp_ref, zn_ref, o_ref, ext_ref,
                     *, tq, d, ctx_len, n_tiles, n_batch):
    b = pl.program_id(0)
    t = pl.program_id(1)
    row0 = t * tq
    g = g_ref[...]

    def hmod(x, r0):
        ic = _is_ctx((x.shape[0], 1), r0, ctx_len)
        sh = _mod_rows(mods_ref, 0, b, n_batch, ic, d)
        sc = _mod_rows(mods_ref, 1, b, n_batch, ic, d)
        return _norm_mod(x, g, sh, sc)

    has_prev, has_next = _stream_edges(t, tq, ctx_len, n_tiles)
    hc = hmod(zc_ref[...], row0)
    ext_ref[0:POOL_HALO, :] = jnp.where(has_prev, hmod(zp_ref[...], row0 - POOL_HALO), 0.0)
    ext_ref[POOL_HALO:POOL_HALO + tq, :] = hc
    ext_ref[POOL_HALO + tq:, :] = jnp.where(has_next, hmod(zn_ref[...], row0 + tq), 0.0)
    up = ext_ref[POOL_HALO - 1:POOL_HALO - 1 + tq, :]
    dn = ext_ref[POOL_HALO + 1:POOL_HALO + 1 + tq, :]
    xx = 0.5 * (up + dn) - hc
    for mi in range(6):
        o_ref[mi] = (hc + xx * mu_ref[mi:mi + 1, :]).astype(bf16)


def _rwkv_mix_call(z, mods, norm_g, mu, ctx_len):
    n_batch, ltot, d = z.shape
    tq = 256
    n_tiles = ltot // tq
    cur, prev, nxt = _halo_specs(tq, d, n_tiles)
    kern = functools.partial(_rwkv_mix_kernel, tq=tq, d=d, ctx_len=ctx_len, n_tiles=n_tiles, n_batch=n_batch)
    return pl.pallas_call(
        kern,
        grid=(n_batch, n_tiles),
        in_specs=[
            pl.BlockSpec(mods.shape, lambda b, t: (0, 0)),
            pl.BlockSpec((1, d), lambda b, t: (0, 0)),
            pl.BlockSpec((SUBLANES, d), lambda b, t: (0, 0)),
            cur, prev, nxt,
        ],
        out_specs=pl.BlockSpec((6, None, tq, d), lambda b, t: (0, b, t, 0)),
        out_shape=jax.ShapeDtypeStruct((6, n_batch, ltot, d), bf16),
        scratch_shapes=[pltpu.VMEM((tq + 2 * POOL_HALO, d), f32)],
        compiler_params=_cparams(("parallel", "parallel")),
        name="rwkv_shift_mix",
    )(mods, norm_g.reshape(1, d), jnp.pad(mu, ((0, SUBLANES - mu.shape[0]), (0, 0))), z, z, z)


def _rkv_kernel(x_ref, w_ref, o_ref, *, tn):
    y = jnp.dot(x_ref[...], w_ref[...], preferred_element_type=f32)
    for gi in range(tn // LANES):
        o_ref[gi] = y[:, gi * LANES:(gi + 1) * LANES]


def _rkv_call(mixes, w_rkv, tm):
    _, n_batch, ltot, d = mixes.shape
    tn = 1024
    src = (0, 2, 3)

    def x_map(b, m, p, n):
        return (jnp.where(p == 0, src[0], jnp.where(p == 1, src[1], src[2])), b, m, 0)

    return pl.pallas_call(
        functools.partial(_rkv_kernel, tn=tn),
        grid=(n_batch, ltot // tm, 3, d // tn),
        in_specs=[
            pl.BlockSpec((None, None, tm, d), x_map),
            pl.BlockSpec((None, d, tn), lambda b, m, p, n: (p, 0, n)),
        ],
        out_specs=pl.BlockSpec((None, None, tn // LANES, tm, LANES), lambda b, m, p, n: (p, b, n, m, 0)),
        out_shape=jax.ShapeDtypeStruct((3, n_batch, d // LANES, ltot, LANES), f32),
        compiler_params=_cparams(("parallel", "parallel", "arbitrary", "arbitrary")),
        name="rwkv_rkv_proj",
    )(mixes, w_rkv)


def _lora_a_kernel(x_ref, w_ref, o_ref):
    o_ref[...] = jnp.dot(x_ref[...], w_ref[...], preferred_element_type=f32)


def _lora_a_call(mixes, w_a, tm):
    _, n_batch, ltot, d = mixes.shape
    nh = w_a.shape[2]
    src = (1, 4, 5)

    def x_map(b, m, p):
        return (jnp.where(p == 0, src[0], jnp.where(p == 1, src[1], src[2])), b, m, 0)

    return pl.pallas_call(
        _lora_a_kernel,
        grid=(n_batch, ltot // tm, 3),
        in_specs=[
            pl.BlockSpec((None, None, tm, d), x_map),
            pl.BlockSpec((None, d, nh), lambda b, m, p: (p, 0, 0)),
        ],
        out_specs=pl.BlockSpec((None, None, tm, nh), lambda b, m, p: (p, b, m, 0)),
        out_shape=jax.ShapeDtypeStruct((3, n_batch, ltot, nh), f32),
        compiler_params=_cparams(("parallel", "parallel", "arbitrary")),
        name="rwkv_lora_a",
    )(mixes, w_a)


def _lora_b_kernel(h_ref, wb_ref, ab_ref, gb_ref, bias_ref, o_ref, *, d):
    n_pairs = d // LANES
    hw = jnp.tanh(h_ref[0])
    ha = h_ref[1]
    hg = _sigmoid(h_ref[2])
    outs = []
    for di in range(2):
        outs.append(_dot(hw[:, di * LANES:(di + 1) * LANES], wb_ref[di]) + bias_ref[di:di + 1, :])
    for di in range(2):
        outs.append(_dot(ha[:, di * LANES:(di + 1) * LANES], ab_ref[di]) + bias_ref[2 + di:3 + di, :])
    outs.append(_dot(hg, gb_ref[...]))
    for oi, y in enumerate(outs):
        for p in range(n_pairs):
            o_ref[oi, p] = y[:, p * LANES:(p + 1) * LANES]


def _lora_b_call(h, w_b, a_b, g_b, bias, tm):
    _, n_batch, ltot, nh = h.shape
    d = g_b.shape[1]
    n_pairs = d // LANES
    return pl.pallas_call(
        functools.partial(_lora_b_kernel, d=d),
        grid=(n_batch, ltot // tm),
        in_specs=[
            pl.BlockSpec((3, None, tm, nh), lambda b, m: (0, b, m, 0)),
            pl.BlockSpec(w_b.shape, lambda b, m: (0, 0, 0)),
            pl.BlockSpec(a_b.shape, lambda b, m: (0, 0, 0)),
            pl.BlockSpec(g_b.shape, lambda b, m: (0, 0)),
            pl.BlockSpec(bias.shape, lambda b, m: (0, 0)),
        ],
        out_specs=pl.BlockSpec((5, None, n_pairs, tm, LANES), lambda b, m: (0, b, 0, m, 0)),
        out_shape=jax.ShapeDtypeStruct((5, n_batch, n_pairs, ltot, LANES), f32),
        compiler_params=_cparams(("parallel", "parallel")),
        name="rwkv_lora_b",
    )(h, w_b, a_b, g_b, bias)


def _seg_sum(x, lo):
    s_lo = jnp.sum(jnp.where(lo, x, 0.0), axis=1, keepdims=True)
    s_all = jnp.sum(x, axis=1, keepdims=True)
    return jnp.where(lo, s_lo, s_all - s_lo)


def _wkv_kernel(*refs, n_pairs, reverse, fused):
    if fused:
        (r_ref, k_ref, v_ref, w_ref, a_ref, par_ref, y0_ref, bon0_ref, gate_ref, ln_ref, og_ref, s_ref,
         lhs_s, rhs1_s, rhs2_s, vbd_s, dec_s, mab_s, mak_s, arbk_s, tinv_s, tmp_s, w_s, g1r_s, uv_s, bon_ref) = refs
    else:
        (r_ref, k_ref, v_ref, w_ref, a_ref, par_ref, y_ref, bon_ref, s_ref,
         lhs_s, rhs1_s, rhs2_s, vbd_s, dec_s, mab_s, mak_s, arbk_s, tinv_s, tmp_s, w_s, g1r_s, uv_s) = refs
    _wkv_body(r_ref, k_ref, v_ref, w_ref, a_ref, par_ref, bon_ref, s_ref,
              lhs_s, rhs1_s, rhs2_s, vbd_s, dec_s, mab_s, mak_s, arbk_s, tinv_s, tmp_s, w_s, g1r_s, uv_s,
              (y0_ref, bon0_ref, gate_ref, ln_ref, og_ref) if fused else (y_ref,),
              n_pairs=n_pairs, reverse=reverse)


def _wkv_body(r_ref, k_ref, v_ref, w_ref, a_ref, par_ref, bon_ref, s_ref,
              lhs_s, rhs1_s, rhs2_s, vbd_s, dec_s, mab_s, mak_s, arbk_s, tinv_s, tmp_s, w_s, g1r_s, uv_s,
              out_refs, *, n_pairs, reverse):
    c = pl.program_id(1)
    L = WKV_CHUNK
    P = 2 * L

    @pl.when(c == 0)
    def _():
        s_ref[...] = jnp.zeros_like(s_ref)

    lane = lax.broadcasted_iota(jnp.int32, (L, LANES), 1)
    lo = lane < RWKV_HEAD
    ri = lax.broadcasted_iota(jnp.int32, (P, P), 0)
    ci = lax.broadcasted_iota(jnp.int32, (P, P), 1)
    same = (ri // L) == (ci // L)
    ii, jj = ri % L, ci % L
    if reverse:
        strict, incl = same & (jj > ii), same & (jj >= ii)
    else:
        strict, incl = same & (jj < ii), same & (jj <= ii)
    eye = (ri == ci).astype(f32)
    ti = lax.broadcasted_iota(jnp.int32, (L, L), 0)
    tj = lax.broadcasted_iota(jnp.int32, (L, L), 1)
    tri = ((tj >= ti) if reverse else (tj <= ti)).astype(bf16)
    diag_p = ((lax.broadcasted_iota(jnp.int32, (P, LANES), 0) // L)
              == (lax.broadcasted_iota(jnp.int32, (P, LANES), 1) // RWKV_HEAD))

    def blockdiag(x):
        return jnp.where(diag_p, jnp.concatenate([x, x], axis=0), 0.0)

    def stack(x):
        return jnp.concatenate([x, x], axis=0)

    pairs = range(n_pairs)

    for p in pairs:
        r, k, v = r_ref[p], k_ref[p], v_ref[p]
        k_k = par_ref[p, 0:1, :]
        k_a = par_ref[p, 1:2, :]
        r_k = par_ref[p, 2:3, :]
        lw = -jnp.exp(-_softplus(-w_ref[p]) - 0.5)
        a = _sigmoid(a_ref[p])
        kn = k * k_k
        kk = kn / jnp.maximum(jnp.sqrt(_seg_sum(kn * kn, lo)), 1e-12)
        kd = k * (1.0 + (a - 1.0) * k_a)
        bon_ref[p] = _seg_sum(r * kd * r_k, lo) * v

        lw_hi = lw.astype(bf16)
        lw_lo = (lw - lw_hi.astype(f32)).astype(bf16)
        cum = (jnp.dot(tri, lw_hi, preferred_element_type=f32)
               + jnp.dot(tri, lw_lo, preferred_element_type=f32))
        tot = cum[0:1, :] if reverse else cum[L - 1:L, :]
        e_neg = jnp.exp(-cum)
        e_end = jnp.exp(tot - cum)
        b_in = kk * a
        at = -kk * jnp.exp(cum - lw)
        rt = r * jnp.exp(cum)
        lhs_s[p] = jnp.concatenate([blockdiag(at), blockdiag(rt)], axis=0).astype(bf16)
        rhs1_s[p] = jnp.concatenate([stack(b_in * e_neg), stack(kd * e_neg)], axis=0).astype(bf16)
        rhs2_s[p] = jnp.concatenate([stack(b_in * e_end), stack(kd * e_end)], axis=0).astype(bf16)
        vbd_s[p] = blockdiag(v).astype(bf16)
        dec_s[p] = jnp.broadcast_to(jnp.exp(tot), (SUBLANES, LANES))

    for p in pairs:
        sc = lax.dot_general(lhs_s[p], rhs1_s[p], NT_DIMS, preferred_element_type=f32)
        m_ab = jnp.where(strict, sc[0:P, 0:P], 0.0)
        mab_s[p] = m_ab.astype(bf16)
        mak_s[p] = jnp.where(strict, sc[0:P, P:2 * P], 0.0).astype(bf16)
        arbk_s[p, :, 0:P] = jnp.where(incl, sc[P:2 * P, 0:P], 0.0).astype(bf16)
        arbk_s[p, :, P:2 * P] = jnp.where(incl, sc[P:2 * P, P:2 * P], 0.0).astype(bf16)
        tinv_s[p] = eye + jnp.where((ri // 2) == (ci // 2), m_ab, 0.0)

    s = 4
    while s <= L:
        level = ((ri // s) == (ci // s)) & ((ri // (s // 2)) != (ci // (s // 2)))
        for p in pairs:
            e = jnp.where(level, mab_s[p], jnp.zeros((P, P), bf16))
            tmp_s[p] = _dot(tinv_s[p], e).astype(bf16)
        for p in pairs:
            t_inv = tinv_s[p]
            tinv_s[p] = t_inv + _dot(tmp_s[p], t_inv)
        s *= 2

    for p in pairs:
        g1 = _dot(lhs_s[p], s_ref[p], NT_DIMS)
        z = jnp.dot(mak_s[p], vbd_s[p], preferred_element_type=f32)
        w_s[p] = (g1[0:P] + z).astype(bf16)
        g1r_s[p] = g1[P:2 * P]

    for p in pairs:
        uv_s[p, 0:P, :] = _dot(tinv_s[p], w_s[p]).astype(bf16)
        uv_s[p, P:2 * P, :] = vbd_s[p]

    for p in pairs:
        y = g1r_s[p] + jnp.dot(arbk_s[p], uv_s[p], preferred_element_type=f32)
        y = y[0:L] + y[L:P]
        if len(out_refs) == 1:
            out_refs[0][p] = y
        else:
            y0_ref, bon0_ref, gate_ref, ln_ref, og_ref = out_refs
            y = y + y0_ref[p]
            mean = _seg_sum(y, lo) * (1.0 / RWKV_HEAD)
            yc = y - mean
            var = _seg_sum(yc * yc, lo) * (1.0 / RWKV_HEAD)
            sl = slice(p * LANES, (p + 1) * LANES)
            o = yc * lax.rsqrt(var + LN_X_EPS) * ln_ref[0:1, sl] + ln_ref[1:2, sl] + bon0_ref[p] + bon_ref[p]
            og_ref[:, sl] = (o * gate_ref[p]).astype(bf16)
        upd = lax.dot_general(uv_s[p], rhs2_s[p], TN_DIMS, preferred_element_type=f32)
        s_ref[p] = s_ref[p] * dec_s[p, 0:1, :] + jnp.where(diag_p, upd, 0.0)


def _wkv_call(rkv, pre, params, ctx_len, direction, first=None, ln_x=None):
    _, n_batch, n_pairs, ltot, _ = rkv.shape
    fused = first is not None
    L = WKV_CHUNK
    P = 2 * L
    n_chunks = ltot // L
    ctx_chunks = ctx_len // L
    reverse = direction == 1

    def chunk_of(c):
        if not reverse:
            return c
        return jnp.where(c < ctx_chunks, ctx_chunks - 1 - c, n_chunks - 1 - (c - ctx_chunks))

    def spec(lead):
        return pl.BlockSpec((None, None, n_pairs, L, LANES), lambda b, c: (lead, b, 0, chunk_of(c), 0))

    pair_spec = pl.BlockSpec((None, n_pairs, L, LANES), lambda b, c: (b, 0, chunk_of(c), 0))
    pair_sds = jax.ShapeDtypeStruct((n_batch, n_pairs, ltot, LANES), f32)
    in_specs = [spec(0), spec(1), spec(2),
                pl.BlockSpec((None, None, n_pairs, L, LANES), lambda b, c: (direction, b, 0, chunk_of(c), 0)),
                pl.BlockSpec((None, None, n_pairs, L, LANES), lambda b, c: (2 + direction, b, 0, chunk_of(c), 0)),
                pl.BlockSpec((None, n_pairs, SUBLANES, LANES), lambda b, c: (direction, 0, 0, 0))]
    operands = [rkv, rkv, rkv, pre, pre, params]
    extra_scratch = []
    if fused:
        d = n_pairs * LANES
        in_specs += [pair_spec, pair_spec,
                     pl.BlockSpec((None, None, n_pairs, L, LANES), lambda b, c: (4, b, 0, chunk_of(c), 0)),
                     pl.BlockSpec((SUBLANES, d), lambda b, c: (0, 0))]
        operands += [first[0], first[1], pre, jnp.pad(ln_x, ((0, SUBLANES - ln_x.shape[0]), (0, 0)))]
        out_specs = pl.BlockSpec((None, L, d), lambda b, c: (b, chunk_of(c), 0))
        out_shape = jax.ShapeDtypeStruct((n_batch, ltot, d), bf16)
        extra_scratch = [pltpu.VMEM((n_pairs, L, LANES), f32)]
    else:
        out_specs = [pair_spec, pair_spec]
        out_shape = [pair_sds, pair_sds]
    return pl.pallas_call(
        functools.partial(_wkv_kernel, n_pairs=n_pairs, reverse=reverse, fused=fused),
        grid=(n_batch, n_chunks),
        in_specs=in_specs,
        out_specs=out_specs,
        out_shape=out_shape,
        scratch_shapes=[
            pltpu.VMEM((n_pairs, P, LANES), f32),
            pltpu.VMEM((n_pairs, 2 * P, LANES), bf16),
            pltpu.VMEM((n_pairs, 2 * P, LANES), bf16),
            pltpu.VMEM((n_pairs, 2 * P, LANES), bf16),
            pltpu.VMEM((n_pairs, P, LANES), bf16),
            pltpu.VMEM((n_pairs, SUBLANES, LANES), f32),
            pltpu.VMEM((n_pairs, P, P), bf16),
            pltpu.VMEM((n_pairs, P, P), bf16),
            pltpu.VMEM((n_pairs, P, 2 * P), bf16),
            pltpu.VMEM((n_pairs, P, P), f32),
            pltpu.VMEM((n_pairs, P, P), bf16),
            pltpu.VMEM((n_pairs, P, LANES), bf16),
            pltpu.VMEM((n_pairs, P, LANES), f32),
            pltpu.VMEM((n_pairs, 2 * P, LANES), bf16),
        ] + extra_scratch,
        compiler_params=_cparams(("parallel", "arbitrary")),
        name="rwkv_wkv_fwd" if not reverse else "rwkv_wkv_bwd",
    )(*operands)


def _rwkv_layer(z, mods, norm_g, ctx_len, mu, w_rkv, w_o, dir_vec, w_la, w_lb, a_la, a_lb, g_la, g_lb, r_k, ln_x):
    n_batch, ltot, d = z.shape
    n_pairs = d // LANES
    tm = 768 if ltot % 768 == 0 else 512
    mixes = _rwkv_mix_call(z, mods, norm_g, mu, ctx_len)
    rkv = _rkv_call(mixes, w_rkv.astype(bf16), tm)

    def pad_cols(w):
        return jnp.concatenate([jnp.pad(w[i], ((0, 0), (0, LANES - w.shape[2]))) for i in range(2)], axis=1)

    def pad_rows(w):
        return jnp.pad(w, ((0, 0), (0, LANES - w.shape[1]), (0, 0)))

    w_a = jnp.stack([pad_cols(w_la), pad_cols(a_la), g_la]).astype(bf16)
    h = _lora_a_call(mixes, w_a, tm)
    bias = jnp.pad(jnp.stack([dir_vec[0, 0], dir_vec[1, 0], dir_vec[0, 1], dir_vec[1, 1]]), ((0, 4), (0, 0)))
    pre = _lora_b_call(h, pad_rows(w_lb).astype(bf16), pad_rows(a_lb).astype(bf16), g_lb.astype(bf16), bias, 256)
    rk_row = r_k.reshape(d)
    params = jnp.stack([jnp.stack([dir_vec[di, 2], dir_vec[di, 3], rk_row]) for di in range(2)])
    params = jnp.pad(params, ((0, 0), (0, SUBLANES - 3), (0, 0)))
    params = params.reshape(2, SUBLANES, n_pairs, LANES).transpose(0, 2, 1, 3)
    first = _wkv_call(rkv, pre, params, ctx_len, 0)
    og = _wkv_call(rkv, pre, params, ctx_len, 1, first=first, ln_x=ln_x)
    return _mm_resid_call(og, w_o.astype(bf16), z, mods, ctx_len, tm, chunk=2)


def _rope_tables(ctx_len, seq_len):
    rows = seq_len // GRID_W
    row = jnp.repeat(jnp.arange(rows, dtype=f32), GRID_W)
    col = jnp.tile(jnp.arange(GRID_W, dtype=f32), rows)
    n_freq = DIFF_HEAD // 4
    inv = ROPE_BASE ** (-jnp.arange(n_freq, dtype=f32) / n_freq)
    ang = jnp.concatenate([row[:, None] * inv, col[:, None] * inv], axis=-1)
    cos = jnp.repeat(jnp.cos(ang), 2, axis=1)
    sin = jnp.repeat(jnp.sin(ang), 2, axis=1) * jnp.tile(jnp.array([-1.0, 1.0], f32), DIFF_HEAD // 2)
    cos = jnp.concatenate([jnp.ones((ctx_len, DIFF_HEAD), f32), cos], axis=0)
    sin = jnp.concatenate([jnp.zeros((ctx_len, DIFF_HEAD), f32), sin], axis=0)
    return cos, sin


def _qkv_kernel(mods_ref, g_ref, qkg_ref, cos_ref, sin_ref, z_ref, w_ref, o_ref, vt_ref, h_ref,
                *, tm, tn, d, ctx_len, n_batch):
    b = pl.program_id(0)
    m = pl.program_id(1)
    n = pl.program_id(2)
    n_reg = d // tn

    @pl.when(n == 0)
    def _():
        _norm_mod_tile(z_ref, h_ref, mods_ref, g_ref, 0, b, m * tm, tm, d, ctx_len, n_batch)

    y = jnp.dot(h_ref[...], w_ref[...], preferred_element_type=f32)

    def qk_epilogue(gain, scale):
        cos = cos_ref[...]
        sin = sin_ref[...]
        even = (lax.broadcasted_iota(jnp.int32, (tm, LANES), 1) % 2) == 0
        for gi in range(tn // LANES):
            x = y[:, gi * LANES:(gi + 1) * LANES]
            ms = jnp.mean(x * x, axis=-1, keepdims=True)
            x = x * lax.rsqrt(ms + EPS) * gain
            partner = jnp.where(even, pltpu.roll(x, LANES - 1, 1), pltpu.roll(x, 1, 1))
            x = x * cos + partner * sin
            o_ref[:, gi * LANES:(gi + 1) * LANES] = (x * scale).astype(bf16)

    @pl.when(n < n_reg)
    def _():
        qk_epilogue(qkg_ref[0:1, :], DIFF_HEAD ** -0.5 * LOG2E)

    @pl.when((n >= n_reg) & (n < 2 * n_reg))
    def _():
        qk_epilogue(qkg_ref[1:2, :], 1.0)

    @pl.when(n >= 2 * n_reg)
    def _():
        vt_ref[...] = y.T.astype(bf16)


def _qkv_call(z, mods, norm_g, w_qkv, qk_g, cos, sin, ctx_len, tm):
    n_batch, ltot, d = z.shape
    tn = 512
    n_reg = d // tn
    kern = functools.partial(_qkv_kernel, tm=tm, tn=tn, d=d, ctx_len=ctx_len, n_batch=n_batch)
    return pl.pallas_call(
        kern,
        grid=(n_batch, ltot // tm, 3 * d // tn),
        in_specs=[
            pl.BlockSpec(mods.shape, lambda b, m, n: (0, 0)),
            pl.BlockSpec((1, d), lambda b, m, n: (0, 0)),
            pl.BlockSpec((SUBLANES, DIFF_HEAD), lambda b, m, n: (0, 0)),
            pl.BlockSpec((tm, DIFF_HEAD), lambda b, m, n: (m, 0)),
            pl.BlockSpec((tm, DIFF_HEAD), lambda b, m, n: (m, 0)),
            pl.BlockSpec((None, tm, d), lambda b, m, n: (b, m, 0)),
            pl.BlockSpec((d, tn), lambda b, m, n: (0, n)),
        ],
        out_specs=[
            pl.BlockSpec((None, tm, tn), lambda b, m, n: (b, m, jnp.minimum(n, 2 * n_reg - 1))),
            pl.BlockSpec((None, None, tn, tm), lambda b, m, n: (b, m, jnp.maximum(n - 2 * n_reg, 0), 0)),
        ],
        out_shape=[jax.ShapeDtypeStruct((n_batch, ltot, 2 * d), bf16),
                   jax.ShapeDtypeStruct((n_batch, ltot // tm, d, tm), bf16)],
        scratch_shapes=[pltpu.VMEM((tm, d), bf16)],
        compiler_params=_cparams(("parallel", "parallel", "arbitrary")),
        name="diff_qkv_proj",
    )(mods, norm_g.reshape(1, d), jnp.pad(qk_g, ((0, SUBLANES - qk_g.shape[0]), (0, 0))), cos, sin, z, w_qkv)


def _attn_kernel(lam_ref, sg_ref, q_ref, k_ref, vt_ref, o_ref, sa_s, sb_s, *, lambda_init):
    tq = q_ref.shape[0]
    n_k, _, tk = vt_ref.shape
    hd = DIFF_HEAD
    qt = q_ref[...].astype(f32).T
    top = lax.broadcasted_iota(jnp.int32, qt.shape, 0) < hd
    w = jnp.concatenate([jnp.where(top, qt, 0.0), jnp.where(top, 0.0, qt)], axis=1).astype(bf16)
    groups = [(j0, min(ATTN_GROUP, n_k - j0)) for j0 in range(0, n_k, ATTN_GROUP)]
    bufs = (sa_s, sb_s)

    def scores(g):
        j0, nb = groups[g]
        bufs[g % 2][0:nb * tk, :] = jnp.dot(k_ref[j0 * tk:(j0 + nb) * tk, :], w, preferred_element_type=f32)

    def absorb(g, m_old, l_old, acc):
        j0, nb = groups[g]
        s = bufs[g % 2][0:nb * tk, :]
        m_new = jnp.maximum(m_old, jnp.max(s, axis=0, keepdims=True))
        alpha = jnp.exp2(m_old - m_new)
        p = jnp.exp2(s - m_new)
        l_new = alpha * l_old + jnp.sum(p, axis=0, keepdims=True)
        vt = jnp.concatenate([vt_ref[j0 + i] for i in range(nb)], axis=1) if nb > 1 else vt_ref[j0]
        return m_new, l_new, alpha * acc + jnp.dot(vt, p.astype(bf16), preferred_element_type=f32)

    m = jnp.full((1, 2 * tq), -jnp.inf, f32)
    l = jnp.zeros((1, 2 * tq), f32)
    acc = jnp.zeros((2 * hd, 2 * tq), f32)
    scores(0)
    for g in range(len(groups)):
        if g + 1 < len(groups):
            scores(g + 1)
        m, l, acc = absorb(g, m, l, acc)
    lv = lam_ref[...]
    lam = (jnp.exp(jnp.sum(lv[0:1] * lv[1:2], axis=1, keepdims=True))
           - jnp.exp(jnp.sum(lv[2:3] * lv[3:4], axis=1, keepdims=True)) + lambda_init)
    o = acc[:, 0:tq] / l[:, 0:tq] - lam * (acc[:, tq:] / l[:, tq:])
    ms = jnp.mean(o * o, axis=0, keepdims=True)
    o = o * lax.rsqrt(ms + EPS) * sg_ref[...]
    o_ref[...] = o.T.astype(bf16)


def _attn_call(qk, vt, lam_vec, subln_g, ctx_len, lambda_init):
    n_batch, ltot, d2 = qk.shape
    d = d2 // 2
    _, n_kb, _, tk = vt.shape
    hw = 2 * DIFF_HEAD
    n_heads = d // hw
    seq_len = ltot - ctx_len
    tq = 256
    assert ctx_len % tq == 0 and seq_len % tq == 0
    q_off = ctx_len // tq
    gain = jnp.broadcast_to((subln_g * (1.0 - lambda_init)).reshape(hw, 1), (hw, tq))
    kern = functools.partial(_attn_kernel, lambda_init=lambda_init)
    return pl.pallas_call(
        kern,
        grid=(n_batch, n_heads, seq_len // tq),
        in_specs=[
            pl.BlockSpec((SUBLANES, DIFF_HEAD), lambda b, h, t: (0, 0)),
            pl.BlockSpec((hw, tq), lambda b, h, t: (0, 0)),
            pl.BlockSpec((None, tq, hw), lambda b, h, t: (b, t + q_off, h)),
            pl.BlockSpec((None, ltot, hw), lambda b, h, t: (b, 0, n_heads + h)),
            pl.BlockSpec((None, n_kb, hw, tk), lambda b, h, t: (b, 0, h, 0)),
        ],
        out_specs=pl.BlockSpec((None, tq, hw), lambda b, h, t: (b, t, h)),
        out_shape=jax.ShapeDtypeStruct((n_batch, seq_len, d), bf16),
        scratch_shapes=[pltpu.VMEM((min(ATTN_GROUP, n_kb) * tk, 2 * tq), f32),
                        pltpu.VMEM((min(ATTN_GROUP, n_kb) * tk, 2 * tq), f32)],
        compiler_params=_cparams(("parallel", "parallel", "parallel")),
        name="diff_attention",
    )(jnp.pad(lam_vec, ((0, SUBLANES - lam_vec.shape[0]), (0, 0))), gain, qk, qk, vt)


def _diff_layer(z, mods, norm_g, ctx_len, w_qkv, w_o, qk_g, lam_vec, subln_g, lambda_init):
    n_batch, ltot, d = z.shape
    seq_len = ltot - ctx_len
    tm = 768 if ltot % 768 == 0 else 512
    cos, sin = _rope_tables(ctx_len, seq_len)
    qk, vt = _qkv_call(z, mods, norm_g, w_qkv.astype(bf16), qk_g, cos, sin, ctx_len, tm)
    o = _attn_call(qk, vt, lam_vec, subln_g, ctx_len, lambda_init)
    return _mm_resid_call(o, w_o.astype(bf16), z, mods, 0, 256, chunk=2, row_off=ctx_len)


def kernel(x, c, ctx, c_ctx, ada_w, ada_b, norm_g, ffn_w_in, ffn_w_out, pool_w, pool_scale, rwkv_mu, rwkv_w_rkv, rwkv_w_o, rwkv_dir_vec, rwkv_w_lora_a, rwkv_w_lora_b, rwkv_a_lora_a, rwkv_a_lora_b, rwkv_g_lora_a, rwkv_g_lora_b, rwkv_r_k, rwkv_ln_x, diff_w_qkv, diff_w_o, diff_qk_g, diff_lambda, diff_subln_g):
    n_batch, seq_len, d = x.shape
    depth = ada_w.shape[0]
    ctx_len = ctx.shape[1]
    assert n_batch + 1 <= SUBLANES
    cs = jnp.concatenate([c, c_ctx[None, :], jnp.zeros((SUBLANES - n_batch - 1, d), f32)], axis=0)
    mods_all = _mods_call(cs, ada_w, ada_b)

    last_reader = max((i for i in range(depth) if i % N_MIXERS != 0), default=-1)
    z = jnp.concatenate([ctx, x], axis=1) if last_reader >= 0 else x
    cur_ctx = ctx_len if last_reader >= 0 else 0
    for i in range(depth):
        kind, j = i % N_MIXERS, i // N_MIXERS
        mods = mods_all[i]
        if kind == 0:
            z = _pool_call(z, mods, norm_g[i, 0], pool_w[j], pool_scale[j], cur_ctx)
        elif kind == 1:
            z = _rwkv_layer(z, mods, norm_g[i, 0], cur_ctx, rwkv_mu[j], rwkv_w_rkv[j], rwkv_w_o[j], rwkv_dir_vec[j],
                            rwkv_w_lora_a[j], rwkv_w_lora_b[j], rwkv_a_lora_a[j], rwkv_a_lora_b[j],
                            rwkv_g_lora_a[j], rwkv_g_lora_b[j], rwkv_r_k[j], rwkv_ln_x[j])
        else:
            lambda_init = 0.8 - 0.6 * math.exp(-0.3 * i)
            z = _diff_layer(z, mods, norm_g[i, 0], cur_ctx, diff_w_qkv[j], diff_w_o[j], diff_qk_g[j],
                            diff_lambda[j], diff_subln_g[j], lambda_init)
            cur_ctx = 0
        if cur_ctx and i >= last_reader:
            z = z[:, cur_ctx:]
            cur_ctx = 0
        ltot = z.shape[1]
        tm = 768 if ltot % 768 == 0 else 512
        z = _ffn_call(z, mods, norm_g[i, 1], ffn_w_in[i].astype(bf16), ffn_w_out[i].astype(bf16), cur_ctx, tm)
    return z[:, cur_ctx:] if cur_ctx else z
```

```python
import functools
import math

import jax
import jax.numpy as jnp
import numpy as np
from jax import lax
from jax.experimental import pallas as pl
from jax.experimental.pallas import tpu as pltpu

f32 = jnp.float32
bf16 = jnp.bfloat16

N_MIXERS = 3
EPS = 1e-6
POOL_WINDOWS = (2, 4, 8, 16)
POOL_HALO = 8
RWKV_HEAD = 64
LN_X_EPS = 64e-5
DIFF_HEAD = 128
ROPE_BASE = 10000.0
LOG2E = 1.4426950408889634
GRID_W = 64
LANES = 128
SUBLANES = 8
WKV_CHUNK = 64
ROW_CHUNK = 256
NORM_ROWS = 16
ATTN_GROUP = 2
VMEM_LIMIT = 56 * 1024 * 1024

NT_DIMS = (((1,), (1,)), ((), ()))
TN_DIMS = (((0,), (0,)), ((), ()))


def _cparams(sem):
    return pltpu.CompilerParams(dimension_semantics=sem, vmem_limit_bytes=VMEM_LIMIT)


def _dot(a, b, dims=None):
    a = a.astype(bf16)
    b = b.astype(bf16)
    if dims is None:
        return jnp.dot(a, b, preferred_element_type=f32)
    return lax.dot_general(a, b, dims, preferred_element_type=f32)


def _sigmoid(x):
    return 1.0 / (1.0 + jnp.exp(-x))


def _softplus(x):
    return jnp.maximum(x, 0.0) + jnp.log(1.0 + jnp.exp(-jnp.abs(x)))


def _row_ids(shape, row0):
    return lax.broadcasted_iota(jnp.int32, shape, 0) + row0


def _mod_rows(mods_ref, chunk, b, n_batch, is_ctx, d):
    lat = mods_ref[pl.ds(b, 1), chunk * d:(chunk + 1) * d]
    if is_ctx is None:
        return lat
    ctx = mods_ref[n_batch:n_batch + 1, chunk * d:(chunk + 1) * d]
    return jnp.where(is_ctx, ctx, lat)


def _norm_mod(x, g, shift, scale):
    ms = jnp.mean(x * x, axis=-1, keepdims=True)
    h = x * lax.rsqrt(ms + EPS) * g
    return h * (1.0 + scale) + shift


def _is_ctx(shape, row0, ctx_len):
    if ctx_len == 0:
        return None
    return _row_ids(shape, row0) < ctx_len


def _mods_kernel(s_ref, w_ref, b_ref, o_ref):
    s = s_ref[...]
    s = s * _sigmoid(s)
    o_ref[...] = _dot(s, w_ref[...]) + b_ref[...]


def _mods_call(cs, ada_w, ada_b):
    depth, d, n6 = ada_w.shape
    tn = 1024
    return pl.pallas_call(
        _mods_kernel,
        grid=(depth, n6 // tn),
        in_specs=[
            pl.BlockSpec((SUBLANES, d), lambda l, n: (0, 0)),
            pl.BlockSpec((None, d, tn), lambda l, n: (l, 0, n)),
            pl.BlockSpec((None, 1, tn), lambda l, n: (l, 0, n)),
        ],
        out_specs=pl.BlockSpec((None, SUBLANES, tn), lambda l, n: (l, 0, n)),
        out_shape=jax.ShapeDtypeStruct((depth, SUBLANES, n6), f32),
        compiler_params=_cparams(("parallel", "parallel")),
        name="adaln_mods",
    )(cs, ada_w, ada_b.reshape(depth, 1, n6))


def _halo_specs(tq, d, n_tiles):
    per = tq // POOL_HALO
    last = n_tiles * per - 1
    cur = pl.BlockSpec((None, tq, d), lambda b, t: (b, t, 0))
    prev = pl.BlockSpec((None, POOL_HALO, d), lambda b, t: (b, jnp.maximum(t * per - 1, 0), 0))
    nxt = pl.BlockSpec((None, POOL_HALO, d), lambda b, t: (b, jnp.minimum((t + 1) * per, last), 0))
    return cur, prev, nxt


def _stream_edges(t, tq, ctx_len, n_tiles):
    ctx_tiles = ctx_len // tq
    first = t == 0
    last = t == n_tiles - 1
    if ctx_tiles:
        first = first | (t == ctx_tiles)
        last = last | (t == ctx_tiles - 1)
    return jnp.logical_not(first), jnp.logical_not(last)


def _pool_kernel(*refs, tq, d, ctx_len, n_tiles, n_batch, seq_len, split):
    if split:
        mods_ref, g_ref, w_ref, ls_ref, zc_ref, zp_ref, zn_ref, ctx_ref, o_ref, ext_ref = refs
    else:
        mods_ref, g_ref, w_ref, ls_ref, zc_ref, zp_ref, zn_ref, o_ref, ext_ref = refs
        ctx_ref = None
    b = pl.program_id(0)
    t = pl.program_id(1)
    row0 = t * tq
    g = g_ref[...]

    def hmod(x, r0):
        ic = _is_ctx((x.shape[0], 1), r0, ctx_len)
        sh = _mod_rows(mods_ref, 0, b, n_batch, ic, d)
        sc = _mod_rows(mods_ref, 1, b, n_batch, ic, d)
        return _norm_mod(x, g, sh, sc)

    has_prev, has_next = _stream_edges(t, tq, ctx_len, n_tiles)
    zc = zc_ref[...] if ctx_ref is None else jnp.where(t == 0, ctx_ref[...], zc_ref[...])
    hc = hmod(zc, row0)
    hp = jnp.where(has_prev, hmod(zp_ref[...], row0 - POOL_HALO), 0.0)
    hn = jnp.where(has_next, hmod(zn_ref[...], row0 + tq), 0.0)
    ext_ref[0:POOL_HALO, :] = hp
    ext_ref[POOL_HALO:POOL_HALO + tq, :] = hc
    ext_ref[POOL_HALO + tq:, :] = hn

    rows = _row_ids((tq, 1), row0)
    if ctx_len:
        in_ctx = rows < ctx_len
        pos = jnp.where(in_ctx, rows, rows - ctx_len)
        slen = jnp.where(in_ctx, ctx_len, seq_len)
    else:
        pos, slen = rows, seq_len

    cg = d // len(POOL_WINDOWS)
    ys = []
    for gi, win in enumerate(POOL_WINDOWS):
        lo_off, hi_off = win // 2, win - win // 2
        c0 = gi * cg
        acc = None
        for off in range(-lo_off, hi_off):
            piece = ext_ref[POOL_HALO + off:POOL_HALO + off + tq, c0:c0 + cg]
            acc = piece if acc is None else acc + piece
        cnt = jnp.minimum(pos + hi_off, slen) - jnp.maximum(pos - lo_off, 0)
        p = acc / cnt.astype(f32) - hc[:, c0:c0 + cg]
        ys.append(_dot(p, w_ref[gi]))
    y = jnp.concatenate(ys, axis=1) * ls_ref[...]
    ic = _is_ctx((tq, 1), row0, ctx_len)
    gate = _mod_rows(mods_ref, 2, b, n_batch, ic, d)
    o_ref[...] = zc + gate * y


def _pool_call(z, mods, norm_g, pool_w, pool_scale, ctx_len, ctx=None):
    n_batch, rows_z, d = z.shape
    tq = 256
    split = ctx is not None
    ltot = rows_z + (ctx_len if split else 0)
    n_tiles = ltot // tq
    ng, cg, _ = pool_w.shape
    if split:
        assert ctx_len == tq and ctx.shape[1] == tq
        per = tq // POOL_HALO
        last = (n_tiles - 1) * per - 1
        cur = pl.BlockSpec((None, tq, d), lambda b, t: (b, jnp.maximum(t - 1, 0), 0))
        prev = pl.BlockSpec((None, POOL_HALO, d), lambda b, t: (b, jnp.maximum((t - 1) * per - 1, 0), 0))
        nxt = pl.BlockSpec((None, POOL_HALO, d), lambda b, t: (b, jnp.minimum(t * per, last), 0))
        extra_specs = [pl.BlockSpec((None, tq, d), lambda b, t: (b, 0, 0))]
        extra_args = [ctx]
    else:
        cur, prev, nxt = _halo_specs(tq, d, n_tiles)
        extra_specs, extra_args = [], []
    kern = functools.partial(_pool_kernel, tq=tq, d=d, ctx_len=ctx_len, n_tiles=n_tiles,
                             n_batch=n_batch, seq_len=ltot - ctx_len, split=split)
    return pl.pallas_call(
        kern,
        grid=(n_batch, n_tiles),
        in_specs=[
            pl.BlockSpec(mods.shape, lambda b, t: (0, 0)),
            pl.BlockSpec((1, d), lambda b, t: (0, 0)),
            pl.BlockSpec((ng, cg, cg), lambda b, t: (0, 0, 0)),
            pl.BlockSpec((1, d), lambda b, t: (0, 0)),
            cur, prev, nxt,
        ] + extra_specs,
        out_specs=pl.BlockSpec((None, tq, d), lambda b, t: (b, t, 0)),
        out_shape=jax.ShapeDtypeStruct((n_batch, ltot, d), f32),
        scratch_shapes=[pltpu.VMEM((tq + 2 * POOL_HALO, d), f32)],
        compiler_params=_cparams(("parallel", "parallel")),
        name="pool_mix",
    )(mods, norm_g.reshape(1, d), pool_w.astype(bf16), pool_scale.reshape(1, d), z, z, z, *extra_args)


def _for_rows(n_rows, body):
    def step(i, carry):
        body(pl.multiple_of(i * ROW_CHUNK, ROW_CHUNK))
        return carry
    lax.fori_loop(0, n_rows // ROW_CHUNK, step, 0)


def _norm_mod_tile(z_ref, h_ref, mods_ref, g_ref, chunk, b, row0, tm, d, ctx_len, n_batch, out_off=0):
    assert ctx_len % NORM_ROWS == 0 and tm % (4 * NORM_ROWS) == 0

    def step(i, carry):
        r0 = pl.multiple_of(i * NORM_ROWS, NORM_ROWS)
        ic = None if ctx_len == 0 else (row0 + r0) < ctx_len
        sh = _mod_rows(mods_ref, chunk, b, n_batch, ic, d)
        sc = _mod_rows(mods_ref, chunk + 1, b, n_batch, ic, d)
        h = _norm_mod(z_ref[pl.ds(r0, NORM_ROWS), :], g_ref[...], sh, sc)
        h_ref[pl.ds(out_off + r0, NORM_ROWS), :] = h.astype(h_ref.dtype)
        return carry
    lax.fori_loop(0, tm // NORM_ROWS, step, 0, unroll=4)


def _ffn_kernel(mods_ref, g_ref, z_ref, wg_ref, wu_ref, wo_ref, o_ref, h_ref,
                *, tm, d, ctx_len, n_batch, n_f):
    b = pl.program_id(0)
    m = pl.program_id(1)
    f = pl.program_id(2)

    @pl.when(f == 0)
    def _():
        _norm_mod_tile(z_ref, h_ref, mods_ref, g_ref, 3, b, m * tm, tm, d, ctx_len, n_batch)
        o_ref[...] = jnp.zeros((tm, d), f32)

    def gate_up(c):
        h = h_ref[c * ROW_CHUNK:(c + 1) * ROW_CHUNK, :]
        return (jnp.dot(h, wg_ref[...], preferred_element_type=f32),
                jnp.dot(h, wu_ref[...], preferred_element_type=f32))

    n_slabs = tm // ROW_CHUNK
    nxt = gate_up(0)
    for c in range(n_slabs):
        gate, up = nxt
        if c + 1 < n_slabs:
            nxt = gate_up(c + 1)
        act = (gate * _sigmoid(gate) * up).astype(bf16)
        o_ref[c * ROW_CHUNK:(c + 1) * ROW_CHUNK, :] += jnp.dot(act, wo_ref[...], preferred_element_type=f32)

    @pl.when(f == n_f - 1)
    def _():
        def epi(r0):
            rows = pl.ds(r0, ROW_CHUNK)
            ic = _is_ctx((ROW_CHUNK, 1), m * tm + r0, ctx_len)
            g2 = _mod_rows(mods_ref, 5, b, n_batch, ic, d)
            o_ref[rows, :] = z_ref[rows, :] + g2 * o_ref[rows, :]
        _for_rows(tm, epi)


def _ffn_call(z, mods, norm_g, w_in, w_out, layer, ctx_len, tm):
    n_batch, ltot, d = z.shape
    fh = w_out.shape[1]
    tf = 512
    n_f = fh // tf
    kern = functools.partial(_ffn_kernel, tm=tm, d=d, ctx_len=ctx_len, n_batch=n_batch, n_f=n_f)
    return pl.pallas_call(
        kern,
        grid=(n_batch, ltot // tm, n_f),
        in_specs=[
            pl.BlockSpec(mods.shape, lambda b, m, f: (0, 0)),
            pl.BlockSpec((1, d), lambda b, m, f: (0, 0)),
            pl.BlockSpec((None, tm, d), lambda b, m, f: (b, m, 0)),
            pl.BlockSpec((None, d, tf), lambda b, m, f: (layer, 0, f)),
            pl.BlockSpec((None, d, tf), lambda b, m, f: (layer, 0, f + n_f)),
            pl.BlockSpec((None, tf, d), lambda b, m, f: (layer, f, 0)),
        ],
        out_specs=pl.BlockSpec((None, tm, d), lambda b, m, f: (b, m, 0)),
        out_shape=jax.ShapeDtypeStruct((n_batch, ltot, d), f32),
        scratch_shapes=[pltpu.VMEM((tm, d), bf16)],
        compiler_params=_cparams(("parallel", "parallel", "arbitrary")),
        name="swiglu_ffn",
    )(mods, norm_g.reshape(1, d), z, w_in, w_in, w_out)


def _mm_resid_kernel(mods_ref, x_ref, w_ref, z_ref, o_ref, *, tm, ctx_len, n_batch, row_off):
    b = pl.program_id(0)
    m = pl.program_id(1)
    lat = mods_ref[pl.ds(b, 1), :]

    def residual(c, y):
        rows = slice(c * ROW_CHUNK, (c + 1) * ROW_CHUNK)
        ic = _is_ctx((ROW_CHUNK, 1), m * tm + row_off + c * ROW_CHUNK, ctx_len)
        gate = lat if ic is None else jnp.where(ic, mods_ref[n_batch:n_batch + 1, :], lat)
        o_ref[rows, :] = z_ref[rows, :] + gate * y

    _slab_products(x_ref, w_ref, tm, residual)


def _mm_resid_call(x, w, z, mods, ctx_len, tm, chunk, row_off=0):
    n_batch, rows, k = x.shape
    d = w.shape[1]
    tn = 1024
    assert row_off % tm == 0 and rows % tm == 0
    m_off = row_off // tm
    g_off = chunk * d // tn
    kern = functools.partial(_mm_resid_kernel, tm=tm, ctx_len=ctx_len, n_batch=n_batch, row_off=row_off)
    return pl.pallas_call(
        kern,
        grid=(n_batch, rows // tm, d // tn),
        in_specs=[
            pl.BlockSpec((SUBLANES, tn), lambda b, m, n: (0, g_off + n)),
            pl.BlockSpec((None, tm, k), lambda b, m, n: (b, m, 0)),
            pl.BlockSpec((k, tn), lambda b, m, n: (0, n)),
            pl.BlockSpec((None, tm, tn), lambda b, m, n: (b, m + m_off, n)),
        ],
        out_specs=pl.BlockSpec((None, tm, tn), lambda b, m, n: (b, m, n)),
        out_shape=jax.ShapeDtypeStruct((n_batch, rows, d), f32),
        compiler_params=_cparams(("parallel", "parallel", "arbitrary")),
        name="proj_residual",
    )(mods, x, w, z)


def _rwkv_mix_kernel(mods_ref, g_ref, mu_ref, zc_ref, zp_ref, zn_ref, o_ref, ext_ref,
                     *, tq, d, ctx_len, n_tiles, n_batch):
    b = pl.program_id(0)
    t = pl.program_id(1)
    row0 = t * tq
    g = g_ref[...]

    def hmod(x, r0):
        ic = _is_ctx((x.shape[0], 1), r0, ctx_len)
        sh = _mod_rows(mods_ref, 0, b, n_batch, ic, d)
        sc = _mod_rows(mods_ref, 1, b, n_batch, ic, d)
        return _norm_mod(x, g, sh, sc)

    has_prev, has_next = _stream_edges(t, tq, ctx_len, n_tiles)
    ext_ref[0:POOL_HALO, :] = jnp.where(has_prev, hmod(zp_ref[...], row0 - POOL_HALO), 0.0)
    _norm_mod_tile(zc_ref, ext_ref, mods_ref, g_ref, 0, b, row0, tq, d, ctx_len, n_batch, out_off=POOL_HALO)
    ext_ref[POOL_HALO + tq:, :] = jnp.where(has_next, hmod(zn_ref[...], row0 + tq), 0.0)

    for r0 in range(0, tq, NORM_ROWS):
        hc = ext_ref[POOL_HALO + r0:POOL_HALO + r0 + NORM_ROWS, :]
        up = ext_ref[POOL_HALO - 1 + r0:POOL_HALO - 1 + r0 + NORM_ROWS, :]
        dn = ext_ref[POOL_HALO + 1 + r0:POOL_HALO + 1 + r0 + NORM_ROWS, :]
        xx = 0.5 * (up + dn) - hc
        for mi in range(6):
            o_ref[mi, r0:r0 + NORM_ROWS, :] = (hc + xx * mu_ref[mi:mi + 1, :]).astype(bf16)


def _rwkv_mix_call(z, mods, norm_g, mu, ctx_len):
    n_batch, ltot, d = z.shape
    tq = 256
    n_tiles = ltot // tq
    cur, prev, nxt = _halo_specs(tq, d, n_tiles)
    kern = functools.partial(_rwkv_mix_kernel, tq=tq, d=d, ctx_len=ctx_len, n_tiles=n_tiles, n_batch=n_batch)
    return pl.pallas_call(
        kern,
        grid=(n_batch, n_tiles),
        in_specs=[
            pl.BlockSpec(mods.shape, lambda b, t: (0, 0)),
            pl.BlockSpec((1, d), lambda b, t: (0, 0)),
            pl.BlockSpec((SUBLANES, d), lambda b, t: (0, 0)),
            cur, prev, nxt,
        ],
        out_specs=pl.BlockSpec((6, None, tq, d), lambda b, t: (0, b, t, 0)),
        out_shape=jax.ShapeDtypeStruct((6, n_batch, ltot, d), bf16),
        scratch_shapes=[pltpu.VMEM((tq + 2 * POOL_HALO, d), f32)],
        compiler_params=_cparams(("parallel", "parallel")),
        name="rwkv_shift_mix",
    )(mods, norm_g.reshape(1, d), jnp.pad(mu, ((0, SUBLANES - mu.shape[0]), (0, 0))), z, z, z)


def _rkv_kernel(x_ref, w_ref, o_ref, *, tm, tn):
    def store(c, y):
        for gi in range(tn // LANES):
            o_ref[gi, c * ROW_CHUNK:(c + 1) * ROW_CHUNK, :] = y[:, gi * LANES:(gi + 1) * LANES]

    _slab_products(x_ref, w_ref, tm, store)


def _rkv_call(mixes, w_rkv, tm):
    _, n_batch, ltot, d = mixes.shape
    tn = 1024
    src = (0, 2, 3)

    def x_map(b, m, p, n):
        return (jnp.where(p == 0, src[0], jnp.where(p == 1, src[1], src[2])), b, m, 0)

    return pl.pallas_call(
        functools.partial(_rkv_kernel, tm=tm, tn=tn),
        grid=(n_batch, ltot // tm, 3, d // tn),
        in_specs=[
            pl.BlockSpec((None, None, tm, d), x_map),
            pl.BlockSpec((None, d, tn), lambda b, m, p, n: (p, 0, n)),
        ],
        out_specs=pl.BlockSpec((None, None, tn // LANES, tm, LANES), lambda b, m, p, n: (p, b, n, m, 0)),
        out_shape=jax.ShapeDtypeStruct((3, n_batch, d // LANES, ltot, LANES), f32),
        compiler_params=_cparams(("parallel", "parallel", "arbitrary", "arbitrary")),
        name="rwkv_rkv_proj",
    )(mixes, w_rkv)


def _lora_a_kernel(x_ref, w_ref, o_ref):
    o_ref[...] = jnp.dot(x_ref[...], w_ref[...], preferred_element_type=f32)


def _lora_a_call(mixes, w_a, tm):
    _, n_batch, ltot, d = mixes.shape
    nh = w_a.shape[2]
    src = (1, 4, 5)

    def x_map(b, m, p):
        return (jnp.where(p == 0, src[0], jnp.where(p == 1, src[1], src[2])), b, m, 0)

    return pl.pallas_call(
        _lora_a_kernel,
        grid=(n_batch, ltot // tm, 3),
        in_specs=[
            pl.BlockSpec((None, None, tm, d), x_map),
            pl.BlockSpec((None, d, nh), lambda b, m, p: (p, 0, 0)),
        ],
        out_specs=pl.BlockSpec((None, None, tm, nh), lambda b, m, p: (p, b, m, 0)),
        out_shape=jax.ShapeDtypeStruct((3, n_batch, ltot, nh), f32),
        compiler_params=_cparams(("parallel", "parallel", "arbitrary")),
        name="rwkv_lora_a",
    )(mixes, w_a)


def _lora_b_kernel(h_ref, wb_ref, ab_ref, gb_ref, bias_ref, o_ref, *, d):
    n_pairs = d // LANES
    hw = jnp.tanh(h_ref[0])
    ha = h_ref[1]
    hg = _sigmoid(h_ref[2])
    outs = []
    for di in range(2):
        outs.append(_dot(hw[:, di * LANES:(di + 1) * LANES], wb_ref[di]) + bias_ref[di:di + 1, :])
    for di in range(2):
        outs.append(_dot(ha[:, di * LANES:(di + 1) * LANES], ab_ref[di]) + bias_ref[2 + di:3 + di, :])
    outs.append(_dot(hg, gb_ref[...]))
    for oi, y in enumerate(outs):
        for p in range(n_pairs):
            o_ref[oi, p] = y[:, p * LANES:(p + 1) * LANES]


def _lora_b_call(h, w_b, a_b, g_b, bias, tm):
    _, n_batch, ltot, nh = h.shape
    d = g_b.shape[1]
    n_pairs = d // LANES
    return pl.pallas_call(
        functools.partial(_lora_b_kernel, d=d),
        grid=(n_batch, ltot // tm),
        in_specs=[
            pl.BlockSpec((3, None, tm, nh), lambda b, m: (0, b, m, 0)),
            pl.BlockSpec(w_b.shape, lambda b, m: (0, 0, 0)),
            pl.BlockSpec(a_b.shape, lambda b, m: (0, 0, 0)),
            pl.BlockSpec(g_b.shape, lambda b, m: (0, 0)),
            pl.BlockSpec(bias.shape, lambda b, m: (0, 0)),
        ],
        out_specs=pl.BlockSpec((5, None, n_pairs, tm, LANES), lambda b, m: (0, b, 0, m, 0)),
        out_shape=jax.ShapeDtypeStruct((5, n_batch, n_pairs, ltot, LANES), f32),
        compiler_params=_cparams(("parallel", "parallel")),
        name="rwkv_lora_b",
    )(h, w_b, a_b, g_b, bias)


def _seg_sum(x, lo):
    s_lo = jnp.sum(jnp.where(lo, x, 0.0), axis=1, keepdims=True)
    s_all = jnp.sum(x, axis=1, keepdims=True)
    return jnp.where(lo, s_lo, s_all - s_lo)


def _wkv_kernel(*refs, n_pairs, reverse, fused):
    if fused:
        (r_ref, k_ref, v_ref, w_ref, a_ref, par_ref, y0_ref, bon0_ref, gate_ref, ln_ref, og_ref, s_ref,
         lhs_s, rhs1_s, rhs2_s, vbd_s, dec_s, mab_s, mak_s, arbk_s, tinv_s, tmp_s, w_s, g1r_s, uv_s, bon_ref) = refs
    else:
        (r_ref, k_ref, v_ref, w_ref, a_ref, par_ref, y_ref, bon_ref, s_ref,
         lhs_s, rhs1_s, rhs2_s, vbd_s, dec_s, mab_s, mak_s, arbk_s, tinv_s, tmp_s, w_s, g1r_s, uv_s) = refs
    _wkv_body(r_ref, k_ref, v_ref, w_ref, a_ref, par_ref, bon_ref, s_ref,
              lhs_s, rhs1_s, rhs2_s, vbd_s, dec_s, mab_s, mak_s, arbk_s, tinv_s, tmp_s, w_s, g1r_s, uv_s,
              (y0_ref, bon0_ref, gate_ref, ln_ref, og_ref) if fused else (y_ref,),
              n_pairs=n_pairs, reverse=reverse)


def _wkv_body(r_ref, k_ref, v_ref, w_ref, a_ref, par_ref, bon_ref, s_ref,
              lhs_s, rhs1_s, rhs2_s, vbd_s, dec_s, mab_s, mak_s, arbk_s, tinv_s, tmp_s, w_s, g1r_s, uv_s,
              out_refs, *, n_pairs, reverse):
    c = pl.program_id(1)
    L = WKV_CHUNK
    P = 2 * L

    @pl.when(c == 0)
    def _():
        s_ref[...] = jnp.zeros_like(s_ref)

    lane = lax.broadcasted_iota(jnp.int32, (L, LANES), 1)
    lo = lane < RWKV_HEAD
    ri = lax.broadcasted_iota(jnp.int32, (P, P), 0)
    ci = lax.broadcasted_iota(jnp.int32, (P, P), 1)
    same = (ri // L) == (ci // L)
    ii, jj = ri % L, ci % L
    if reverse:
        strict, incl = same & (jj > ii), same & (jj >= ii)
    else:
        strict, incl = same & (jj < ii), same & (jj <= ii)
    eye = (ri == ci).astype(f32)
    ti = lax.broadcasted_iota(jnp.int32, (L, L), 0)
    tj = lax.broadcasted_iota(jnp.int32, (L, L), 1)
    tri = ((tj >= ti) if reverse else (tj <= ti)).astype(bf16)
    diag_p = ((lax.broadcasted_iota(jnp.int32, (P, LANES), 0) // L)
              == (lax.broadcasted_iota(jnp.int32, (P, LANES), 1) // RWKV_HEAD))

    def blockdiag(x):
        return jnp.where(diag_p, jnp.concatenate([x, x], axis=0), 0.0)

    def stack(x):
        return jnp.concatenate([x, x], axis=0)

    pairs = range(n_pairs)

    for p in pairs:
        r, k, v = r_ref[p], k_ref[p], v_ref[p]
        k_k = par_ref[p, 0:1, :]
        k_a = par_ref[p, 1:2, :]
        r_k = par_ref[p, 2:3, :]
        lw = -jnp.exp(-_softplus(-w_ref[p]) - 0.5)
        a = _sigmoid(a_ref[p])
        kn = k * k_k
        kk = kn / jnp.maximum(jnp.sqrt(_seg_sum(kn * kn, lo)), 1e-12)
        kd = k * (1.0 + (a - 1.0) * k_a)
        bon_ref[p] = _seg_sum(r * kd * r_k, lo) * v

        lw_hi = lw.astype(bf16)
        lw_lo = (lw - lw_hi.astype(f32)).astype(bf16)
        cum = (jnp.dot(tri, lw_hi, preferred_element_type=f32)
               + jnp.dot(tri, lw_lo, preferred_element_type=f32))
        tot = cum[0:1, :] if reverse else cum[L - 1:L, :]
        e_neg = jnp.exp(-cum)
        e_end = jnp.exp(tot - cum)
        b_in = kk * a
        at = -kk * jnp.exp(cum - lw)
        rt = r * jnp.exp(cum)
        lhs_s[p] = jnp.concatenate([blockdiag(at), blockdiag(rt)], axis=0).astype(bf16)
        rhs1_s[p] = jnp.concatenate([stack(b_in * e_neg), stack(kd * e_neg)], axis=0).astype(bf16)
        rhs2_s[p] = jnp.concatenate([stack(b_in * e_end), stack(kd * e_end)], axis=0).astype(bf16)
        vbd_s[p] = blockdiag(v).astype(bf16)
        dec_s[p] = jnp.broadcast_to(jnp.exp(tot), (SUBLANES, LANES))

    for p in pairs:
        sc = lax.dot_general(lhs_s[p], rhs1_s[p], NT_DIMS, preferred_element_type=f32)
        m_ab = jnp.where(strict, sc[0:P, 0:P], 0.0)
        mab_s[p] = m_ab.astype(bf16)
        mak_s[p] = jnp.where(strict, sc[0:P, P:2 * P], 0.0).astype(bf16)
        arbk_s[p, :, 0:P] = jnp.where(incl, sc[P:2 * P, 0:P], 0.0).astype(bf16)
        arbk_s[p, :, P:2 * P] = jnp.where(incl, sc[P:2 * P, P:2 * P], 0.0).astype(bf16)
        tinv_s[p] = eye + jnp.where((ri // 2) == (ci // 2), m_ab, 0.0)

    s = 4
    while s <= L:
        level = ((ri // s) == (ci // s)) & ((ri // (s // 2)) != (ci // (s // 2)))
        for p in pairs:
            e = jnp.where(level, mab_s[p], jnp.zeros((P, P), bf16))
            tmp_s[p] = _dot(tinv_s[p], e).astype(bf16)
        for p in pairs:
            t_inv = tinv_s[p]
            tinv_s[p] = t_inv + _dot(tmp_s[p], t_inv)
        s *= 2

    for p in pairs:
        g1 = _dot(lhs_s[p], s_ref[p], NT_DIMS)
        z = jnp.dot(mak_s[p], vbd_s[p], preferred_element_type=f32)
        w_s[p] = (g1[0:P] + z).astype(bf16)
        g1r_s[p] = g1[P:2 * P]

    for p in pairs:
        uv_s[p, 0:P, :] = _dot(tinv_s[p], w_s[p]).astype(bf16)
        uv_s[p, P:2 * P, :] = vbd_s[p]

    for p in pairs:
        y = g1r_s[p] + jnp.dot(arbk_s[p], uv_s[p], preferred_element_type=f32)
        y = y[0:L] + y[L:P]
        if len(out_refs) == 1:
            out_refs[0][p] = y
        else:
            y0_ref, bon0_ref, gate_ref, ln_ref, og_ref = out_refs
            y = y + y0_ref[p]
            mean = _seg_sum(y, lo) * (1.0 / RWKV_HEAD)
            yc = y - mean
            var = _seg_sum(yc * yc, lo) * (1.0 / RWKV_HEAD)
            sl = slice(p * LANES, (p + 1) * LANES)
            o = yc * lax.rsqrt(var + LN_X_EPS) * ln_ref[0:1, sl] + ln_ref[1:2, sl] + bon0_ref[p] + bon_ref[p]
            og_ref[:, sl] = (o * gate_ref[p]).astype(bf16)
        upd = lax.dot_general(uv_s[p], rhs2_s[p], TN_DIMS, preferred_element_type=f32)
        s_ref[p] = s_ref[p] * dec_s[p, 0:1, :] + jnp.where(diag_p, upd, 0.0)


def _wkv_call(rkv, pre, params, ctx_len, direction, first=None, ln_x=None):
    _, n_batch, n_pairs, ltot, _ = rkv.shape
    fused = first is not None
    L = WKV_CHUNK
    P = 2 * L
    n_chunks = ltot // L
    ctx_chunks = ctx_len // L
    reverse = direction == 1

    def chunk_of(c):
        if not reverse:
            return c
        return jnp.where(c < ctx_chunks, ctx_chunks - 1 - c, n_chunks - 1 - (c - ctx_chunks))

    def spec(lead):
        return pl.BlockSpec((None, None, n_pairs, L, LANES), lambda b, c: (lead, b, 0, chunk_of(c), 0))

    pair_spec = pl.BlockSpec((None, n_pairs, L, LANES), lambda b, c: (b, 0, chunk_of(c), 0))
    pair_sds = jax.ShapeDtypeStruct((n_batch, n_pairs, ltot, LANES), f32)
    in_specs = [spec(0), spec(1), spec(2),
                pl.BlockSpec((None, None, n_pairs, L, LANES), lambda b, c: (direction, b, 0, chunk_of(c), 0)),
                pl.BlockSpec((None, None, n_pairs, L, LANES), lambda b, c: (2 + direction, b, 0, chunk_of(c), 0)),
                pl.BlockSpec((None, n_pairs, SUBLANES, LANES), lambda b, c: (direction, 0, 0, 0))]
    operands = [rkv, rkv, rkv, pre, pre, params]
    extra_scratch = []
    if fused:
        d = n_pairs * LANES
        in_specs += [pair_spec, pair_spec,
                     pl.BlockSpec((None, None, n_pairs, L, LANES), lambda b, c: (4, b, 0, chunk_of(c), 0)),
                     pl.BlockSpec((SUBLANES, d), lambda b, c: (0, 0))]
        operands += [first[0], first[1], pre, jnp.pad(ln_x, ((0, SUBLANES - ln_x.shape[0]), (0, 0)))]
        out_specs = pl.BlockSpec((None, L, d), lambda b, c: (b, chunk_of(c), 0))
        out_shape = jax.ShapeDtypeStruct((n_batch, ltot, d), bf16)
        extra_scratch = [pltpu.VMEM((n_pairs, L, LANES), f32)]
    else:
        out_specs = [pair_spec, pair_spec]
        out_shape = [pair_sds, pair_sds]
    return pl.pallas_call(
        functools.partial(_wkv_kernel, n_pairs=n_pairs, reverse=reverse, fused=fused),
        grid=(n_batch, n_chunks),
        in_specs=in_specs,
        out_specs=out_specs,
        out_shape=out_shape,
        scratch_shapes=[
            pltpu.VMEM((n_pairs, P, LANES), f32),
            pltpu.VMEM((n_pairs, 2 * P, LANES), bf16),
            pltpu.VMEM((n_pairs, 2 * P, LANES), bf16),
            pltpu.VMEM((n_pairs, 2 * P, LANES), bf16),
            pltpu.VMEM((n_pairs, P, LANES), bf16),
            pltpu.VMEM((n_pairs, SUBLANES, LANES), f32),
            pltpu.VMEM((n_pairs, P, P), bf16),
            pltpu.VMEM((n_pairs, P, P), bf16),
            pltpu.VMEM((n_pairs, P, 2 * P), bf16),
            pltpu.VMEM((n_pairs, P, P), f32),
            pltpu.VMEM((n_pairs, P, P), bf16),
            pltpu.VMEM((n_pairs, P, LANES), bf16),
            pltpu.VMEM((n_pairs, P, LANES), f32),
            pltpu.VMEM((n_pairs, 2 * P, LANES), bf16),
        ] + extra_scratch,
        compiler_params=_cparams(("parallel", "arbitrary")),
        name="rwkv_wkv_fwd" if not reverse else "rwkv_wkv_bwd",
    )(*operands)


def _rwkv_layer(z, mods, norm_g, ctx_len, mu, w_rkv, w_o, dir_vec, w_la, w_lb, a_la, a_lb, g_la, g_lb, r_k, ln_x):
    n_batch, ltot, d = z.shape
    n_pairs = d // LANES
    tm = 768 if ltot % 768 == 0 else 512
    mixes = _rwkv_mix_call(z, mods, norm_g, mu, ctx_len)
    rkv = _rkv_call(mixes, w_rkv.astype(bf16), tm)

    def pad_cols(w):
        return jnp.concatenate([jnp.pad(w[i], ((0, 0), (0, LANES - w.shape[2]))) for i in range(2)], axis=1)

    def pad_rows(w):
        return jnp.pad(w, ((0, 0), (0, LANES - w.shape[1]), (0, 0)))

    w_a = jnp.stack([pad_cols(w_la), pad_cols(a_la), g_la]).astype(bf16)
    h = _lora_a_call(mixes, w_a, tm)
    bias = jnp.pad(jnp.stack([dir_vec[0, 0], dir_vec[1, 0], dir_vec[0, 1], dir_vec[1, 1]]), ((0, 4), (0, 0)))
    pre = _lora_b_call(h, pad_rows(w_lb).astype(bf16), pad_rows(a_lb).astype(bf16), g_lb.astype(bf16), bias, 256)
    rk_row = r_k.reshape(d)
    params = jnp.stack([jnp.stack([dir_vec[di, 2], dir_vec[di, 3], rk_row]) for di in range(2)])
    params = jnp.pad(params, ((0, 0), (0, SUBLANES - 3), (0, 0)))
    params = params.reshape(2, SUBLANES, n_pairs, LANES).transpose(0, 2, 1, 3)
    first = _wkv_call(rkv, pre, params, ctx_len, 0)
    og = _wkv_call(rkv, pre, params, ctx_len, 1, first=first, ln_x=ln_x)
    return _mm_resid_call(og, w_o.astype(bf16), z, mods, ctx_len, tm, chunk=2)


def _rope_tables(ctx_len, seq_len):
    rows = seq_len // GRID_W
    row = np.repeat(np.arange(rows, dtype=np.float32), GRID_W)
    col = np.tile(np.arange(GRID_W, dtype=np.float32), rows)
    n_freq = DIFF_HEAD // 4
    inv = (np.float32(ROPE_BASE) ** (-np.arange(n_freq, dtype=np.float32) / np.float32(n_freq))).astype(np.float32)
    ang = np.concatenate([row[:, None] * inv, col[:, None] * inv], axis=-1).astype(np.float32)
    cos = np.repeat(np.cos(ang), 2, axis=1)
    sin = np.repeat(np.sin(ang), 2, axis=1) * np.tile(np.array([-1.0, 1.0], np.float32), DIFF_HEAD // 2)
    cos = np.concatenate([np.ones((ctx_len, DIFF_HEAD), np.float32), cos], axis=0)
    sin = np.concatenate([np.zeros((ctx_len, DIFF_HEAD), np.float32), sin], axis=0)
    return jnp.asarray(cos, f32), jnp.asarray(sin, f32)


def _slab_products(x_ref, w_ref, tm, consume):
    def product(c):
        return jnp.dot(x_ref[c * ROW_CHUNK:(c + 1) * ROW_CHUNK, :], w_ref[...], preferred_element_type=f32)

    n_slabs = tm // ROW_CHUNK
    nxt = product(0)
    for c in range(n_slabs):
        y = nxt
        if c + 1 < n_slabs:
            nxt = product(c + 1)
        consume(c, y)


def _qk_kernel(mods_ref, g_ref, qkg_ref, cos_ref, sin_ref, z_ref, w_ref, o_ref, h_ref,
               *, tm, tn, d, ctx_len, n_batch):
    b = pl.program_id(0)
    m = pl.program_id(1)
    n = pl.program_id(2)

    @pl.when(n == 0)
    def _():
        _norm_mod_tile(z_ref, h_ref, mods_ref, g_ref, 0, b, m * tm, tm, d, ctx_len, n_batch)

    is_q = n < d // tn
    gain = qkg_ref[pl.ds(jnp.where(is_q, 0, 1), 1), :]
    scale = jnp.where(is_q, DIFF_HEAD ** -0.5 * LOG2E, 1.0)
    even = (lax.broadcasted_iota(jnp.int32, (ROW_CHUNK, LANES), 1) % 2) == 0

    def epilogue(c, y):
        rows = slice(c * ROW_CHUNK, (c + 1) * ROW_CHUNK)
        cos = cos_ref[rows, :]
        sin = sin_ref[rows, :]
        for gi in range(tn // LANES):
            x = y[:, gi * LANES:(gi + 1) * LANES]
            ms = jnp.mean(x * x, axis=-1, keepdims=True)
            x = x * (lax.rsqrt(ms + EPS) * scale) * gain
            partner = jnp.where(even, pltpu.roll(x, LANES - 1, 1), pltpu.roll(x, 1, 1))
            o_ref[rows, gi * LANES:(gi + 1) * LANES] = (x * cos + partner * sin).astype(bf16)

    _slab_products(h_ref, w_ref, tm, epilogue)


def _qk_call(z, mods, norm_g, w_qk, qk_g, cos, sin, ctx_len, tm):
    n_batch, ltot, d = z.shape
    tn = 512
    kern = functools.partial(_qk_kernel, tm=tm, tn=tn, d=d, ctx_len=ctx_len, n_batch=n_batch)
    return pl.pallas_call(
        kern,
        grid=(n_batch, ltot // tm, 2 * d // tn),
        in_specs=[
            pl.BlockSpec(mods.shape, lambda b, m, n: (0, 0)),
            pl.BlockSpec((1, d), lambda b, m, n: (0, 0)),
            pl.BlockSpec((SUBLANES, DIFF_HEAD), lambda b, m, n: (0, 0)),
            pl.BlockSpec((tm, DIFF_HEAD), lambda b, m, n: (m, 0)),
            pl.BlockSpec((tm, DIFF_HEAD), lambda b, m, n: (m, 0)),
            pl.BlockSpec((None, tm, d), lambda b, m, n: (b, m, 0)),
            pl.BlockSpec((d, tn), lambda b, m, n: (0, n)),
        ],
        out_specs=[pl.BlockSpec((None, tm, tn), lambda b, m, n: (b, m, n)),
                   pl.BlockSpec((None, tm, d), lambda b, m, n: (b, m, 0))],
        out_shape=[jax.ShapeDtypeStruct((n_batch, ltot, 2 * d), bf16),
                   jax.ShapeDtypeStruct((n_batch, ltot, d), bf16)],
        compiler_params=_cparams(("parallel", "parallel", "arbitrary")),
        name="diff_qk_proj",
    )(mods, norm_g.reshape(1, d), jnp.pad(qk_g, ((0, SUBLANES - qk_g.shape[0]), (0, 0))), cos, sin, z, w_qk)


def _vt_kernel(h_ref, w_ref, o_ref, *, tm):
    def store(c, y):
        o_ref[:, c * ROW_CHUNK:(c + 1) * ROW_CHUNK] = y.astype(bf16).T

    _slab_products(h_ref, w_ref, tm, store)


def _vt_call(h, w_qkv, tm):
    n_batch, ltot, d = h.shape
    tn = 512
    v_off = 2 * d // tn
    return pl.pallas_call(
        functools.partial(_vt_kernel, tm=tm),
        grid=(n_batch, ltot // tm, d // tn),
        in_specs=[pl.BlockSpec((None, tm, d), lambda b, m, n: (b, m, 0)),
                  pl.BlockSpec((d, tn), lambda b, m, n: (0, v_off + n))],
        out_specs=pl.BlockSpec((None, None, tn, tm), lambda b, m, n: (b, m, n, 0)),
        out_shape=jax.ShapeDtypeStruct((n_batch, ltot // tm, d, tm), bf16),
        compiler_params=_cparams(("parallel", "parallel", "arbitrary")),
        name="diff_v_proj",
    )(h, w_qkv)


def _attn_kernel(lam_ref, sg_ref, q_ref, k_ref, vt_ref, o_ref, sa_s, sb_s, *, lambda_init):
    tq = q_ref.shape[0]
    n_k, _, tk = vt_ref.shape
    hd = DIFF_HEAD
    qt = q_ref[...].astype(f32).T
    top = lax.broadcasted_iota(jnp.int32, qt.shape, 0) < hd
    w = jnp.concatenate([jnp.where(top, qt, 0.0), jnp.where(top, 0.0, qt)], axis=1).astype(bf16)
    groups = [(j0, min(ATTN_GROUP, n_k - j0)) for j0 in range(0, n_k, ATTN_GROUP)]
    bufs = (sa_s, sb_s)

    def scores(g):
        j0, nb = groups[g]
        bufs[g % 2][0:nb * tk, :] = jnp.dot(k_ref[j0 * tk:(j0 + nb) * tk, :], w, preferred_element_type=f32)

    def absorb(g, m_old, l_old, acc):
        j0, nb = groups[g]
        s = bufs[g % 2][0:nb * tk, :]
        m_new = jnp.maximum(m_old, jnp.max(s, axis=0, keepdims=True))
        alpha = jnp.exp2(m_old - m_new)
        p = jnp.exp2(s - m_new)
        l_new = alpha * l_old + jnp.sum(p, axis=0, keepdims=True)
        vt = jnp.concatenate([vt_ref[j0 + i] for i in range(nb)], axis=1) if nb > 1 else vt_ref[j0]
        return m_new, l_new, alpha * acc + jnp.dot(vt, p.astype(bf16), preferred_element_type=f32)

    m = jnp.full((1, 2 * tq), -jnp.inf, f32)
    l = jnp.zeros((1, 2 * tq), f32)
    acc = jnp.zeros((2 * hd, 2 * tq), f32)
    scores(0)
    for g in range(len(groups)):
        if g + 1 < len(groups):
            scores(g + 1)
        m, l, acc = absorb(g, m, l, acc)
    lv = lam_ref[...]
    lam = (jnp.exp(jnp.sum(lv[0:1] * lv[1:2], axis=1, keepdims=True))
           - jnp.exp(jnp.sum(lv[2:3] * lv[3:4], axis=1, keepdims=True)) + lambda_init)
    o = acc[:, 0:tq] / l[:, 0:tq] - lam * (acc[:, tq:] / l[:, tq:])
    ms = jnp.mean(o * o, axis=0, keepdims=True)
    o = o * lax.rsqrt(ms + EPS) * sg_ref[...]
    o_ref[...] = o.T.astype(bf16)


def _attn_call(qk, vt, lam_vec, subln_g, ctx_len, lambda_init):
    n_batch, ltot, d2 = qk.shape
    d = d2 // 2
    _, n_kb, _, tk = vt.shape
    hw = 2 * DIFF_HEAD
    n_heads = d // hw
    seq_len = ltot - ctx_len
    tq = 256
    assert ctx_len % tq == 0 and seq_len % tq == 0
    q_off = ctx_len // tq
    gain = jnp.broadcast_to((subln_g * (1.0 - lambda_init)).reshape(hw, 1), (hw, tq))
    kern = functools.partial(_attn_kernel, lambda_init=lambda_init)
    return pl.pallas_call(
        kern,
        grid=(n_batch, n_heads, seq_len // tq),
        in_specs=[
            pl.BlockSpec((SUBLANES, DIFF_HEAD), lambda b, h, t: (0, 0)),
            pl.BlockSpec((hw, tq), lambda b, h, t: (0, 0)),
            pl.BlockSpec((None, tq, hw), lambda b, h, t: (b, t + q_off, h)),
            pl.BlockSpec((None, ltot, hw), lambda b, h, t: (b, 0, n_heads + h)),
            pl.BlockSpec((None, n_kb, hw, tk), lambda b, h, t: (b, 0, h, 0)),
        ],
        out_specs=pl.BlockSpec((None, tq, hw), lambda b, h, t: (b, t, h)),
        out_shape=jax.ShapeDtypeStruct((n_batch, seq_len, d), bf16),
        scratch_shapes=[pltpu.VMEM((min(ATTN_GROUP, n_kb) * tk, 2 * tq), f32),
                        pltpu.VMEM((min(ATTN_GROUP, n_kb) * tk, 2 * tq), f32)],
        compiler_params=_cparams(("parallel", "parallel", "parallel")),
        name="diff_attention",
    )(jnp.pad(lam_vec, ((0, SUBLANES - lam_vec.shape[0]), (0, 0))), gain, qk, qk, vt)


def _diff_layer(z, mods, norm_g, ctx_len, w_qkv, w_o, qk_g, lam_vec, subln_g, lambda_init):
    n_batch, ltot, d = z.shape
    seq_len = ltot - ctx_len
    tm = 768 if ltot % 768 == 0 else 512
    cos, sin = _rope_tables(ctx_len, seq_len)
    w_bf = w_qkv.astype(bf16)
    qk, h = _qk_call(z, mods, norm_g, w_bf, qk_g, cos, sin, ctx_len, tm)
    vt = _vt_call(h, w_bf, tm)
    o = _attn_call(qk, vt, lam_vec, subln_g, ctx_len, lambda_init)
    return _mm_resid_call(o, w_o.astype(bf16), z, mods, 0, 256, chunk=2, row_off=ctx_len)


def kernel(x, c, ctx, c_ctx, ada_w, ada_b, norm_g, ffn_w_in, ffn_w_out, pool_w, pool_scale, rwkv_mu, rwkv_w_rkv, rwkv_w_o, rwkv_dir_vec, rwkv_w_lora_a, rwkv_w_lora_b, rwkv_a_lora_a, rwkv_a_lora_b, rwkv_g_lora_a, rwkv_g_lora_b, rwkv_r_k, rwkv_ln_x, diff_w_qkv, diff_w_o, diff_qk_g, diff_lambda, diff_subln_g):
    n_batch, seq_len, d = x.shape
    depth = ada_w.shape[0]
    ctx_len = ctx.shape[1]
    assert n_batch + 1 <= SUBLANES
    cs = jnp.concatenate([c, c_ctx[None, :], jnp.zeros((SUBLANES - n_batch - 1, d), f32)], axis=0)
    mods_all = _mods_call(cs, ada_w, ada_b)

    last_reader = max((i for i in range(depth) if i % N_MIXERS != 0), default=-1)
    cur_ctx = ctx_len if last_reader >= 0 else 0
    join_in_pool = cur_ctx == 256 and depth > 0
    z = x if (cur_ctx == 0 or join_in_pool) else jnp.concatenate([ctx, x], axis=1)
    w_in_all = ffn_w_in.astype(bf16)
    w_out_all = ffn_w_out.astype(bf16)
    for i in range(depth):
        kind, j = i % N_MIXERS, i // N_MIXERS
        mods = mods_all[i]
        if kind == 0:
            z = _pool_call(z, mods, norm_g[i, 0], pool_w[j], pool_scale[j], cur_ctx,
                           ctx=ctx if (i == 0 and join_in_pool) else None)
        elif kind == 1:
            z = _rwkv_layer(z, mods, norm_g[i, 0], cur_ctx, rwkv_mu[j], rwkv_w_rkv[j], rwkv_w_o[j], rwkv_dir_vec[j],
                            rwkv_w_lora_a[j], rwkv_w_lora_b[j], rwkv_a_lora_a[j], rwkv_a_lora_b[j],
                            rwkv_g_lora_a[j], rwkv_g_lora_b[j], rwkv_r_k[j], rwkv_ln_x[j])
        else:
            lambda_init = 0.8 - 0.6 * math.exp(-0.3 * i)
            z = _diff_layer(z, mods, norm_g[i, 0], cur_ctx, diff_w_qkv[j], diff_w_o[j], diff_qk_g[j],
                            diff_lambda[j], diff_subln_g[j], lambda_init)
            cur_ctx = 0
        if cur_ctx and i >= last_reader:
            z = z[:, cur_ctx:]
            cur_ctx = 0
        ltot = z.shape[1]
        tm = 768 if ltot % 768 == 0 else 512
        z = _ffn_call(z, mods, norm_g[i, 1], w_in_all, w_out_all, i, cur_ctx, tm)
    return z[:, cur_ctx:] if cur_ctx else z
```

```python
import functools
import math

import jax
import jax.numpy as jnp
import numpy as np
from jax import lax
from jax.experimental import pallas as pl
from jax.experimental.pallas import tpu as pltpu

f32 = jnp.float32
bf16 = jnp.bfloat16

N_MIXERS = 3
EPS = 1e-6
POOL_WINDOWS = (2, 4, 8, 16)
POOL_HALO = 8
RWKV_HEAD = 64
LN_X_EPS = 64e-5
DIFF_HEAD = 128
ROPE_BASE = 10000.0
LOG2E = 1.4426950408889634
GRID_W = 64
LANES = 128
SUBLANES = 8
WKV_CHUNK = 64
ROW_CHUNK = 256
NORM_ROWS = 16
ATTN_GROUP = 2
VMEM_LIMIT = 60 * 1024 * 1024

NT_DIMS = (((1,), (1,)), ((), ()))
TN_DIMS = (((0,), (0,)), ((), ()))


def _cparams(sem):
    return pltpu.CompilerParams(dimension_semantics=sem, vmem_limit_bytes=VMEM_LIMIT)


def _dot(a, b, dims=None):
    a = a.astype(bf16)
    b = b.astype(bf16)
    if dims is None:
        return jnp.dot(a, b, preferred_element_type=f32)
    return lax.dot_general(a, b, dims, preferred_element_type=f32)


def _sigmoid(x):
    return 1.0 / (1.0 + jnp.exp(-x))


def _softplus(x):
    return jnp.maximum(x, 0.0) + jnp.log(1.0 + jnp.exp(-jnp.abs(x)))


def _row_ids(shape, row0):
    return lax.broadcasted_iota(jnp.int32, shape, 0) + row0


def _mod_rows(mods_ref, chunk, b, n_batch, is_ctx, d):
    lat = mods_ref[pl.ds(b, 1), chunk * d:(chunk + 1) * d]
    if is_ctx is None:
        return lat
    ctx = mods_ref[n_batch:n_batch + 1, chunk * d:(chunk + 1) * d]
    return jnp.where(is_ctx, ctx, lat)


def _norm_mod(x, g, shift, scale):
    ms = jnp.mean(x * x, axis=-1, keepdims=True)
    h = x * lax.rsqrt(ms + EPS) * g
    return h * (1.0 + scale) + shift


def _is_ctx(shape, row0, ctx_len):
    if ctx_len == 0:
        return None
    return _row_ids(shape, row0) < ctx_len


def _mods_kernel(s_ref, w_ref, b_ref, o_ref):
    s = s_ref[...]
    s = s * _sigmoid(s)
    o_ref[...] = _dot(s, w_ref[...]) + b_ref[...]


def _mods_call(cs, ada_w, ada_b):
    depth, d, n6 = ada_w.shape
    tn = 1024
    return pl.pallas_call(
        _mods_kernel,
        grid=(depth, n6 // tn),
        in_specs=[
            pl.BlockSpec((SUBLANES, d), lambda l, n: (0, 0)),
            pl.BlockSpec((None, d, tn), lambda l, n: (l, 0, n)),
            pl.BlockSpec((None, 1, tn), lambda l, n: (l, 0, n)),
        ],
        out_specs=pl.BlockSpec((None, SUBLANES, tn), lambda l, n: (l, 0, n)),
        out_shape=jax.ShapeDtypeStruct((depth, SUBLANES, n6), f32),
        compiler_params=_cparams(("parallel", "parallel")),
        name="adaln_mods",
    )(cs, ada_w, ada_b.reshape(depth, 1, n6))


def _halo_specs(tq, d, n_tiles):
    per = tq // POOL_HALO
    last = n_tiles * per - 1
    cur = pl.BlockSpec((None, tq, d), lambda b, t: (b, t, 0))
    prev = pl.BlockSpec((None, POOL_HALO, d), lambda b, t: (b, jnp.maximum(t * per - 1, 0), 0))
    nxt = pl.BlockSpec((None, POOL_HALO, d), lambda b, t: (b, jnp.minimum((t + 1) * per, last), 0))
    return cur, prev, nxt


def _stream_edges(t, tq, ctx_len, n_tiles):
    ctx_tiles = ctx_len // tq
    first = t == 0
    last = t == n_tiles - 1
    if ctx_tiles:
        first = first | (t == ctx_tiles)
        last = last | (t == ctx_tiles - 1)
    return jnp.logical_not(first), jnp.logical_not(last)


def _pool_kernel(*refs, tq, d, ctx_len, n_tiles, n_batch, seq_len, split):
    if split:
        mods_ref, g_ref, w_ref, ls_ref, zc_ref, zp_ref, zn_ref, ctx_ref, o_ref, ext_ref = refs
    else:
        mods_ref, g_ref, w_ref, ls_ref, zc_ref, zp_ref, zn_ref, o_ref, ext_ref = refs
        ctx_ref = None
    b = pl.program_id(0)
    t = pl.program_id(1)
    row0 = t * tq
    g = g_ref[...]

    def hmod(x, r0):
        ic = _is_ctx((x.shape[0], 1), r0, ctx_len)
        sh = _mod_rows(mods_ref, 0, b, n_batch, ic, d)
        sc = _mod_rows(mods_ref, 1, b, n_batch, ic, d)
        return _norm_mod(x, g, sh, sc)

    has_prev, has_next = _stream_edges(t, tq, ctx_len, n_tiles)
    zc = zc_ref[...] if ctx_ref is None else jnp.where(t == 0, ctx_ref[...], zc_ref[...])
    hc = hmod(zc, row0)
    hp = jnp.where(has_prev, hmod(zp_ref[...], row0 - POOL_HALO), 0.0)
    hn = jnp.where(has_next, hmod(zn_ref[...], row0 + tq), 0.0)
    ext_ref[0:POOL_HALO, :] = hp
    ext_ref[POOL_HALO:POOL_HALO + tq, :] = hc
    ext_ref[POOL_HALO + tq:, :] = hn

    rows = _row_ids((tq, 1), row0)
    if ctx_len:
        in_ctx = rows < ctx_len
        pos = jnp.where(in_ctx, rows, rows - ctx_len)
        slen = jnp.where(in_ctx, ctx_len, seq_len)
    else:
        pos, slen = rows, seq_len

    cg = d // len(POOL_WINDOWS)
    ys = []
    for gi, win in enumerate(POOL_WINDOWS):
        lo_off, hi_off = win // 2, win - win // 2
        c0 = gi * cg
        acc = None
        for off in range(-lo_off, hi_off):
            piece = ext_ref[POOL_HALO + off:POOL_HALO + off + tq, c0:c0 + cg]
            acc = piece if acc is None else acc + piece
        cnt = jnp.minimum(pos + hi_off, slen) - jnp.maximum(pos - lo_off, 0)
        p = acc / cnt.astype(f32) - hc[:, c0:c0 + cg]
        ys.append(_dot(p, w_ref[gi]))
    y = jnp.concatenate(ys, axis=1) * ls_ref[...]
    ic = _is_ctx((tq, 1), row0, ctx_len)
    gate = _mod_rows(mods_ref, 2, b, n_batch, ic, d)
    o_ref[...] = zc + gate * y


def _pool_call(z, mods, norm_g, pool_w, pool_scale, ctx_len, ctx=None):
    n_batch, rows_z, d = z.shape
    tq = 256
    split = ctx is not None
    ltot = rows_z + (ctx_len if split else 0)
    n_tiles = ltot // tq
    ng, cg, _ = pool_w.shape
    if split:
        assert ctx_len == tq and ctx.shape[1] == tq
        per = tq // POOL_HALO
        last = (n_tiles - 1) * per - 1
        cur = pl.BlockSpec((None, tq, d), lambda b, t: (b, jnp.maximum(t - 1, 0), 0))
        prev = pl.BlockSpec((None, POOL_HALO, d), lambda b, t: (b, jnp.maximum((t - 1) * per - 1, 0), 0))
        nxt = pl.BlockSpec((None, POOL_HALO, d), lambda b, t: (b, jnp.minimum(t * per, last), 0))
        extra_specs = [pl.BlockSpec((None, tq, d), lambda b, t: (b, 0, 0))]
        extra_args = [ctx]
    else:
        cur, prev, nxt = _halo_specs(tq, d, n_tiles)
        extra_specs, extra_args = [], []
    kern = functools.partial(_pool_kernel, tq=tq, d=d, ctx_len=ctx_len, n_tiles=n_tiles,
                             n_batch=n_batch, seq_len=ltot - ctx_len, split=split)
    return pl.pallas_call(
        kern,
        grid=(n_batch, n_tiles),
        in_specs=[
            pl.BlockSpec(mods.shape, lambda b, t: (0, 0)),
            pl.BlockSpec((1, d), lambda b, t: (0, 0)),
            pl.BlockSpec((ng, cg, cg), lambda b, t: (0, 0, 0)),
            pl.BlockSpec((1, d), lambda b, t: (0, 0)),
            cur, prev, nxt,
        ] + extra_specs,
        out_specs=pl.BlockSpec((None, tq, d), lambda b, t: (b, t, 0)),
        out_shape=jax.ShapeDtypeStruct((n_batch, ltot, d), f32),
        scratch_shapes=[pltpu.VMEM((tq + 2 * POOL_HALO, d), f32)],
        compiler_params=_cparams(("parallel", "parallel")),
        name="pool_mix",
    )(mods, norm_g.reshape(1, d), pool_w.astype(bf16), pool_scale.reshape(1, d), z, z, z, *extra_args)


def _for_rows(n_rows, body):
    def step(i, carry):
        body(pl.multiple_of(i * ROW_CHUNK, ROW_CHUNK))
        return carry
    lax.fori_loop(0, n_rows // ROW_CHUNK, step, 0)


def _norm_mod_tile(z_ref, h_ref, mods_ref, g_ref, chunk, b, row0, tm, d, ctx_len, n_batch, out_off=0):
    assert ctx_len % NORM_ROWS == 0 and tm % (4 * NORM_ROWS) == 0

    def step(i, carry):
        r0 = pl.multiple_of(i * NORM_ROWS, NORM_ROWS)
        ic = None if ctx_len == 0 else (row0 + r0) < ctx_len
        sh = _mod_rows(mods_ref, chunk, b, n_batch, ic, d)
        sc = _mod_rows(mods_ref, chunk + 1, b, n_batch, ic, d)
        h = _norm_mod(z_ref[pl.ds(r0, NORM_ROWS), :], g_ref[...], sh, sc)
        h_ref[pl.ds(out_off + r0, NORM_ROWS), :] = h.astype(h_ref.dtype)
        return carry
    lax.fori_loop(0, tm // NORM_ROWS, step, 0, unroll=4)


def _ffn_kernel(mods_ref, g_ref, z_ref, wg_ref, wu_ref, wo_ref, o_ref, h_ref,
                *, tm, d, ctx_len, n_batch, n_f):
    b = pl.program_id(0)
    m = pl.program_id(1)
    f = pl.program_id(2)

    @pl.when(f == 0)
    def _():
        _norm_mod_tile(z_ref, h_ref, mods_ref, g_ref, 3, b, m * tm, tm, d, ctx_len, n_batch)
        o_ref[...] = jnp.zeros((tm, d), f32)

    def gate_up(c):
        h = h_ref[c * ROW_CHUNK:(c + 1) * ROW_CHUNK, :]
        return (jnp.dot(h, wg_ref[...], preferred_element_type=f32),
                jnp.dot(h, wu_ref[...], preferred_element_type=f32))

    n_slabs = tm // ROW_CHUNK
    nxt = gate_up(0)
    for c in range(n_slabs):
        gate, up = nxt
        if c + 1 < n_slabs:
            nxt = gate_up(c + 1)
        act = (gate * _sigmoid(gate) * up).astype(bf16)
        o_ref[c * ROW_CHUNK:(c + 1) * ROW_CHUNK, :] += jnp.dot(act, wo_ref[...], preferred_element_type=f32)

    @pl.when(f == n_f - 1)
    def _():
        def epi(r0):
            rows = pl.ds(r0, ROW_CHUNK)
            ic = _is_ctx((ROW_CHUNK, 1), m * tm + r0, ctx_len)
            g2 = _mod_rows(mods_ref, 5, b, n_batch, ic, d)
            o_ref[rows, :] = z_ref[rows, :] + g2 * o_ref[rows, :]
        _for_rows(tm, epi)


def _ffn_call(z, mods, norm_g, w_in, w_out, layer, ctx_len, tm):
    n_batch, ltot, d = z.shape
    fh = w_out.shape[1]
    tf = 512
    n_f = fh // tf
    kern = functools.partial(_ffn_kernel, tm=tm, d=d, ctx_len=ctx_len, n_batch=n_batch, n_f=n_f)
    return pl.pallas_call(
        kern,
        grid=(n_batch, ltot // tm, n_f),
        in_specs=[
            pl.BlockSpec(mods.shape, lambda b, m, f: (0, 0)),
            pl.BlockSpec((1, d), lambda b, m, f: (0, 0)),
            pl.BlockSpec((None, tm, d), lambda b, m, f: (b, m, 0)),
            pl.BlockSpec((None, d, tf), lambda b, m, f: (layer, 0, f)),
            pl.BlockSpec((None, d, tf), lambda b, m, f: (layer, 0, f + n_f)),
            pl.BlockSpec((None, tf, d), lambda b, m, f: (layer, f, 0)),
        ],
        out_specs=pl.BlockSpec((None, tm, d), lambda b, m, f: (b, m, 0)),
        out_shape=jax.ShapeDtypeStruct((n_batch, ltot, d), f32),
        scratch_shapes=[pltpu.VMEM((tm, d), bf16)],
        compiler_params=_cparams(("parallel", "parallel", "arbitrary")),
        name="swiglu_ffn",
    )(mods, norm_g.reshape(1, d), z, w_in, w_in, w_out)


def _mm_resid_kernel(mods_ref, x_ref, w_ref, z_ref, o_ref, *, tm, ctx_len, n_batch, row_off):
    b = pl.program_id(0)
    m = pl.program_id(1)
    lat = mods_ref[pl.ds(b, 1), :]

    def residual(c, y):
        rows = slice(c * ROW_CHUNK, (c + 1) * ROW_CHUNK)
        ic = _is_ctx((ROW_CHUNK, 1), m * tm + row_off + c * ROW_CHUNK, ctx_len)
        gate = lat if ic is None else jnp.where(ic, mods_ref[n_batch:n_batch + 1, :], lat)
        o_ref[rows, :] = z_ref[rows, :] + gate * y

    _slab_products(x_ref, w_ref, tm, residual)


def _mm_resid_call(x, w, z, mods, ctx_len, tm, chunk, row_off=0):
    n_batch, rows, k = x.shape
    d = w.shape[1]
    tn = 1024
    assert row_off % tm == 0 and rows % tm == 0
    m_off = row_off // tm
    g_off = chunk * d // tn
    kern = functools.partial(_mm_resid_kernel, tm=tm, ctx_len=ctx_len, n_batch=n_batch, row_off=row_off)
    return pl.pallas_call(
        kern,
        grid=(n_batch, rows // tm, d // tn),
        in_specs=[
            pl.BlockSpec((SUBLANES, tn), lambda b, m, n: (0, g_off + n)),
            pl.BlockSpec((None, tm, k), lambda b, m, n: (b, m, 0)),
            pl.BlockSpec((k, tn), lambda b, m, n: (0, n)),
            pl.BlockSpec((None, tm, tn), lambda b, m, n: (b, m + m_off, n)),
        ],
        out_specs=pl.BlockSpec((None, tm, tn), lambda b, m, n: (b, m, n)),
        out_shape=jax.ShapeDtypeStruct((n_batch, rows, d), f32),
        compiler_params=_cparams(("parallel", "parallel", "arbitrary")),
        name="proj_residual",
    )(mods, x, w, z)


def _rwkv_mix_kernel(mods_ref, g_ref, mu_ref, zc_ref, zp_ref, zn_ref, o_ref, ext_ref,
                     *, tq, d, ctx_len, n_tiles, n_batch):
    b = pl.program_id(0)
    t = pl.program_id(1)
    row0 = t * tq
    g = g_ref[...]

    def hmod(x, r0):
        ic = _is_ctx((x.shape[0], 1), r0, ctx_len)
        sh = _mod_rows(mods_ref, 0, b, n_batch, ic, d)
        sc = _mod_rows(mods_ref, 1, b, n_batch, ic, d)
        return _norm_mod(x, g, sh, sc)

    has_prev, has_next = _stream_edges(t, tq, ctx_len, n_tiles)
    ext_ref[0:POOL_HALO, :] = jnp.where(has_prev, hmod(zp_ref[...], row0 - POOL_HALO), 0.0)
    _norm_mod_tile(zc_ref, ext_ref, mods_ref, g_ref, 0, b, row0, tq, d, ctx_len, n_batch, out_off=POOL_HALO)
    ext_ref[POOL_HALO + tq:, :] = jnp.where(has_next, hmod(zn_ref[...], row0 + tq), 0.0)

    for r0 in range(0, tq, NORM_ROWS):
        hc = ext_ref[POOL_HALO + r0:POOL_HALO + r0 + NORM_ROWS, :]
        up = ext_ref[POOL_HALO - 1 + r0:POOL_HALO - 1 + r0 + NORM_ROWS, :]
        dn = ext_ref[POOL_HALO + 1 + r0:POOL_HALO + 1 + r0 + NORM_ROWS, :]
        xx = 0.5 * (up + dn) - hc
        for mi in range(6):
            o_ref[mi, r0:r0 + NORM_ROWS, :] = (hc + xx * mu_ref[mi:mi + 1, :]).astype(bf16)


def _rwkv_mix_call(z, mods, norm_g, mu, ctx_len):
    n_batch, ltot, d = z.shape
    tq = 256
    n_tiles = ltot // tq
    cur, prev, nxt = _halo_specs(tq, d, n_tiles)
    kern = functools.partial(_rwkv_mix_kernel, tq=tq, d=d, ctx_len=ctx_len, n_tiles=n_tiles, n_batch=n_batch)
    return pl.pallas_call(
        kern,
        grid=(n_batch, n_tiles),
        in_specs=[
            pl.BlockSpec(mods.shape, lambda b, t: (0, 0)),
            pl.BlockSpec((1, d), lambda b, t: (0, 0)),
            pl.BlockSpec((SUBLANES, d), lambda b, t: (0, 0)),
            cur, prev, nxt,
        ],
        out_specs=pl.BlockSpec((6, None, tq, d), lambda b, t: (0, b, t, 0)),
        out_shape=jax.ShapeDtypeStruct((6, n_batch, ltot, d), bf16),
        scratch_shapes=[pltpu.VMEM((tq + 2 * POOL_HALO, d), f32)],
        compiler_params=_cparams(("parallel", "parallel")),
        name="rwkv_shift_mix",
    )(mods, norm_g.reshape(1, d), jnp.pad(mu, ((0, SUBLANES - mu.shape[0]), (0, 0))), z, z, z)


def _rkv_kernel(x_ref, w_ref, o_ref, *, tm, tn):
    def store(c, y):
        for gi in range(tn // LANES):
            o_ref[gi, c * ROW_CHUNK:(c + 1) * ROW_CHUNK, :] = y[:, gi * LANES:(gi + 1) * LANES]

    _slab_products(x_ref, w_ref, tm, store)


def _rkv_call(mixes, w_rkv, tm):
    _, n_batch, ltot, d = mixes.shape
    tn = 1024
    src = (0, 2, 3)

    def x_map(b, m, p, n):
        return (jnp.where(p == 0, src[0], jnp.where(p == 1, src[1], src[2])), b, m, 0)

    return pl.pallas_call(
        functools.partial(_rkv_kernel, tm=tm, tn=tn),
        grid=(n_batch, ltot // tm, 3, d // tn),
        in_specs=[
            pl.BlockSpec((None, None, tm, d), x_map),
            pl.BlockSpec((None, d, tn), lambda b, m, p, n: (p, 0, n)),
        ],
        out_specs=pl.BlockSpec((None, None, tn // LANES, tm, LANES), lambda b, m, p, n: (p, b, n, m, 0)),
        out_shape=jax.ShapeDtypeStruct((3, n_batch, d // LANES, ltot, LANES), f32),
        compiler_params=_cparams(("parallel", "parallel", "arbitrary", "arbitrary")),
        name="rwkv_rkv_proj",
    )(mixes, w_rkv)


def _lora_a_kernel(x_ref, w_ref, o_ref):
    o_ref[...] = jnp.dot(x_ref[...], w_ref[...], preferred_element_type=f32)


def _lora_a_call(mixes, w_a, tm):
    _, n_batch, ltot, d = mixes.shape
    nh = w_a.shape[2]
    src = (1, 4, 5)

    def x_map(b, m, p):
        return (jnp.where(p == 0, src[0], jnp.where(p == 1, src[1], src[2])), b, m, 0)

    return pl.pallas_call(
        _lora_a_kernel,
        grid=(n_batch, ltot // tm, 3),
        in_specs=[
            pl.BlockSpec((None, None, tm, d), x_map),
            pl.BlockSpec((None, d, nh), lambda b, m, p: (p, 0, 0)),
        ],
        out_specs=pl.BlockSpec((None, None, tm, nh), lambda b, m, p: (p, b, m, 0)),
        out_shape=jax.ShapeDtypeStruct((3, n_batch, ltot, nh), f32),
        compiler_params=_cparams(("parallel", "parallel", "arbitrary")),
        name="rwkv_lora_a",
    )(mixes, w_a)


def _lora_b_kernel(h_ref, wb_ref, ab_ref, gb_ref, bias_ref, o_ref, *, d):
    n_pairs = d // LANES
    hw = jnp.tanh(h_ref[0])
    ha = h_ref[1]
    hg = _sigmoid(h_ref[2])
    outs = []
    for di in range(2):
        outs.append(_dot(hw[:, di * LANES:(di + 1) * LANES], wb_ref[di]) + bias_ref[di:di + 1, :])
    for di in range(2):
        outs.append(_dot(ha[:, di * LANES:(di + 1) * LANES], ab_ref[di]) + bias_ref[2 + di:3 + di, :])
    outs.append(_dot(hg, gb_ref[...]))
    for oi, y in enumerate(outs):
        for p in range(n_pairs):
            o_ref[oi, p] = y[:, p * LANES:(p + 1) * LANES]


def _lora_b_call(h, w_b, a_b, g_b, bias, tm):
    _, n_batch, ltot, nh = h.shape
    d = g_b.shape[1]
    n_pairs = d // LANES
    return pl.pallas_call(
        functools.partial(_lora_b_kernel, d=d),
        grid=(n_batch, ltot // tm),
        in_specs=[
            pl.BlockSpec((3, None, tm, nh), lambda b, m: (0, b, m, 0)),
            pl.BlockSpec(w_b.shape, lambda b, m: (0, 0, 0)),
            pl.BlockSpec(a_b.shape, lambda b, m: (0, 0, 0)),
            pl.BlockSpec(g_b.shape, lambda b, m: (0, 0)),
            pl.BlockSpec(bias.shape, lambda b, m: (0, 0)),
        ],
        out_specs=pl.BlockSpec((5, None, n_pairs, tm, LANES), lambda b, m: (0, b, 0, m, 0)),
        out_shape=jax.ShapeDtypeStruct((5, n_batch, n_pairs, ltot, LANES), f32),
        compiler_params=_cparams(("parallel", "parallel")),
        name="rwkv_lora_b",
    )(h, w_b, a_b, g_b, bias)


def _seg_sum(x, lo):
    s_lo = jnp.sum(jnp.where(lo, x, 0.0), axis=1, keepdims=True)
    s_all = jnp.sum(x, axis=1, keepdims=True)
    return jnp.where(lo, s_lo, s_all - s_lo)


def _wkv_kernel(*refs, n_pairs, reverse, fused):
    if fused:
        (r_ref, k_ref, v_ref, w_ref, a_ref, par_ref, y0_ref, bon0_ref, gate_ref, ln_ref, og_ref, s_ref,
         lhs_s, rhs1_s, rhs2_s, vbd_s, dec_s, mab_s, mak_s, arbk_s, tinv_s, tmp_s, w_s, g1r_s, uv_s, bon_ref) = refs
    else:
        (r_ref, k_ref, v_ref, w_ref, a_ref, par_ref, y_ref, bon_ref, s_ref,
         lhs_s, rhs1_s, rhs2_s, vbd_s, dec_s, mab_s, mak_s, arbk_s, tinv_s, tmp_s, w_s, g1r_s, uv_s) = refs
    _wkv_body(r_ref, k_ref, v_ref, w_ref, a_ref, par_ref, bon_ref, s_ref,
              lhs_s, rhs1_s, rhs2_s, vbd_s, dec_s, mab_s, mak_s, arbk_s, tinv_s, tmp_s, w_s, g1r_s, uv_s,
              (y0_ref, bon0_ref, gate_ref, ln_ref, og_ref) if fused else (y_ref,),
              n_pairs=n_pairs, reverse=reverse)


def _wkv_body(r_ref, k_ref, v_ref, w_ref, a_ref, par_ref, bon_ref, s_ref,
              lhs_s, rhs1_s, rhs2_s, vbd_s, dec_s, mab_s, mak_s, arbk_s, tinv_s, tmp_s, w_s, g1r_s, uv_s,
              out_refs, *, n_pairs, reverse):
    c = pl.program_id(1)
    L = WKV_CHUNK
    P = 2 * L

    @pl.when(c == 0)
    def _():
        s_ref[...] = jnp.zeros_like(s_ref)

    lane = lax.broadcasted_iota(jnp.int32, (L, LANES), 1)
    lo = lane < RWKV_HEAD
    ri = lax.broadcasted_iota(jnp.int32, (P, P), 0)
    ci = lax.broadcasted_iota(jnp.int32, (P, P), 1)
    same = (ri // L) == (ci // L)
    ii, jj = ri % L, ci % L
    if reverse:
        strict, incl = same & (jj > ii), same & (jj >= ii)
    else:
        strict, incl = same & (jj < ii), same & (jj <= ii)
    eye = (ri == ci).astype(f32)
    ti = lax.broadcasted_iota(jnp.int32, (L, L), 0)
    tj = lax.broadcasted_iota(jnp.int32, (L, L), 1)
    tri = ((tj >= ti) if reverse else (tj <= ti)).astype(bf16)
    diag_p = ((lax.broadcasted_iota(jnp.int32, (P, LANES), 0) // L)
              == (lax.broadcasted_iota(jnp.int32, (P, LANES), 1) // RWKV_HEAD))

    def blockdiag(x):
        return jnp.where(diag_p, jnp.concatenate([x, x], axis=0), 0.0)

    def stack(x):
        return jnp.concatenate([x, x], axis=0)

    def prepare(p):
        r, k, v = r_ref[p], k_ref[p], v_ref[p]
        k_k = par_ref[p, 0:1, :]
        k_a = par_ref[p, 1:2, :]
        r_k = par_ref[p, 2:3, :]
        lw = -jnp.exp(-_softplus(-w_ref[p]) - 0.5)
        a = _sigmoid(a_ref[p])
        kn = k * k_k
        kk = kn / jnp.maximum(jnp.sqrt(_seg_sum(kn * kn, lo)), 1e-12)
        kd = k * (1.0 + (a - 1.0) * k_a)
        bon_ref[p] = _seg_sum(r * kd * r_k, lo) * v

        lw_hi = lw.astype(bf16)
        lw_lo = (lw - lw_hi.astype(f32)).astype(bf16)
        cum = (jnp.dot(tri, lw_hi, preferred_element_type=f32)
               + jnp.dot(tri, lw_lo, preferred_element_type=f32))
        tot = cum[0:1, :] if reverse else cum[L - 1:L, :]
        e_neg = jnp.exp(-cum)
        e_end = jnp.exp(tot - cum)
        b_in = kk * a
        at = -kk * jnp.exp(cum - lw)
        rt = r * jnp.exp(cum)
        lhs_s[p] = jnp.concatenate([blockdiag(at), blockdiag(rt)], axis=0).astype(bf16)
        rhs1_s[p] = jnp.concatenate([stack(b_in * e_neg), stack(kd * e_neg)], axis=0).astype(bf16)
        rhs2_s[p] = jnp.concatenate([stack(b_in * e_end), stack(kd * e_end)], axis=0).astype(bf16)
        vbd_s[p] = blockdiag(v).astype(bf16)
        dec_s[p] = jnp.broadcast_to(jnp.exp(tot), (SUBLANES, LANES))

    def scores(p):
        sc = lax.dot_general(lhs_s[p], rhs1_s[p], NT_DIMS, preferred_element_type=f32)
        m_ab = jnp.where(strict, sc[0:P, 0:P], 0.0)
        mab_s[p] = m_ab.astype(bf16)
        mak_s[p] = jnp.where(strict, sc[0:P, P:2 * P], 0.0).astype(bf16)
        arbk_s[p, :, 0:P] = jnp.where(incl, sc[P:2 * P, 0:P], 0.0).astype(bf16)
        arbk_s[p, :, P:2 * P] = jnp.where(incl, sc[P:2 * P, P:2 * P], 0.0).astype(bf16)
        tinv_s[p] = eye + jnp.where((ri // 2) == (ci // 2), m_ab, 0.0)

    def invert(pairs):
        s = 4
        while s <= L:
            level = ((ri // s) == (ci // s)) & ((ri // (s // 2)) != (ci // (s // 2)))
            for p in pairs:
                e = jnp.where(level, mab_s[p], jnp.zeros((P, P), bf16))
                tmp_s[p] = _dot(tinv_s[p], e).astype(bf16)
            for p in pairs:
                t_inv = tinv_s[p]
                tinv_s[p] = t_inv + _dot(tmp_s[p], t_inv)
            s *= 2

    def read_state(p):
        g1 = _dot(lhs_s[p], s_ref[p], NT_DIMS)
        z = jnp.dot(mak_s[p], vbd_s[p], preferred_element_type=f32)
        w_s[p] = (g1[0:P] + z).astype(bf16)
        g1r_s[p] = g1[P:2 * P]

    def solve(p):
        uv_s[p, 0:P, :] = _dot(tinv_s[p], w_s[p]).astype(bf16)
        uv_s[p, P:2 * P, :] = vbd_s[p]

    def emit(p):
        y = g1r_s[p] + jnp.dot(arbk_s[p], uv_s[p], preferred_element_type=f32)
        y = y[0:L] + y[L:P]
        if len(out_refs) == 1:
            out_refs[0][p] = y
        else:
            y0_ref, bon0_ref, gate_ref, ln_ref, og_ref = out_refs
            y = y + y0_ref[p]
            mean = _seg_sum(y, lo) * (1.0 / RWKV_HEAD)
            yc = y - mean
            var = _seg_sum(yc * yc, lo) * (1.0 / RWKV_HEAD)
            sl = slice(p * LANES, (p + 1) * LANES)
            o = yc * lax.rsqrt(var + LN_X_EPS) * ln_ref[0:1, sl] + ln_ref[1:2, sl] + bon0_ref[p] + bon_ref[p]
            og_ref[:, sl] = (o * gate_ref[p]).astype(bf16)
        upd = lax.dot_general(uv_s[p], rhs2_s[p], TN_DIMS, preferred_element_type=f32)
        s_ref[p] = s_ref[p] * dec_s[p, 0:1, :] + jnp.where(diag_p, upd, 0.0)

    def finish(pairs):
        for stage in (read_state, solve, emit):
            for p in pairs:
                stage(p)

    pairs = range(n_pairs)
    for p in pairs:
        prepare(p)
    for p in pairs:
        scores(p)
    invert(pairs)
    finish(pairs)


def _wkv_call(rkv, pre, params, ctx_len, direction, first=None, ln_x=None):
    _, n_batch, n_pairs, ltot, _ = rkv.shape
    fused = first is not None
    L = WKV_CHUNK
    P = 2 * L
    n_chunks = ltot // L
    ctx_chunks = ctx_len // L
    reverse = direction == 1

    def chunk_of(c):
        if not reverse:
            return c
        return jnp.where(c < ctx_chunks, ctx_chunks - 1 - c, n_chunks - 1 - (c - ctx_chunks))

    def spec(lead):
        return pl.BlockSpec((None, None, n_pairs, L, LANES), lambda b, c: (lead, b, 0, chunk_of(c), 0))

    pair_spec = pl.BlockSpec((None, n_pairs, L, LANES), lambda b, c: (b, 0, chunk_of(c), 0))
    pair_sds = jax.ShapeDtypeStruct((n_batch, n_pairs, ltot, LANES), f32)
    in_specs = [spec(0), spec(1), spec(2),
                pl.BlockSpec((None, None, n_pairs, L, LANES), lambda b, c: (direction, b, 0, chunk_of(c), 0)),
                pl.BlockSpec((None, None, n_pairs, L, LANES), lambda b, c: (2 + direction, b, 0, chunk_of(c), 0)),
                pl.BlockSpec((None, n_pairs, SUBLANES, LANES), lambda b, c: (direction, 0, 0, 0))]
    operands = [rkv, rkv, rkv, pre, pre, params]
    extra_scratch = []
    if fused:
        d = n_pairs * LANES
        in_specs += [pair_spec, pair_spec,
                     pl.BlockSpec((None, None, n_pairs, L, LANES), lambda b, c: (4, b, 0, chunk_of(c), 0)),
                     pl.BlockSpec((SUBLANES, d), lambda b, c: (0, 0))]
        operands += [first[0], first[1], pre, jnp.pad(ln_x, ((0, SUBLANES - ln_x.shape[0]), (0, 0)))]
        out_specs = pl.BlockSpec((None, L, d), lambda b, c: (b, chunk_of(c), 0))
        out_shape = jax.ShapeDtypeStruct((n_batch, ltot, d), bf16)
        extra_scratch = [pltpu.VMEM((n_pairs, L, LANES), f32)]
    else:
        out_specs = [pair_spec, pair_spec]
        out_shape = [pair_sds, pair_sds]
    return pl.pallas_call(
        functools.partial(_wkv_kernel, n_pairs=n_pairs, reverse=reverse, fused=fused),
        grid=(n_batch, n_chunks),
        in_specs=in_specs,
        out_specs=out_specs,
        out_shape=out_shape,
        scratch_shapes=[
            pltpu.VMEM((n_pairs, P, LANES), f32),
            pltpu.VMEM((n_pairs, 2 * P, LANES), bf16),
            pltpu.VMEM((n_pairs, 2 * P, LANES), bf16),
            pltpu.VMEM((n_pairs, 2 * P, LANES), bf16),
            pltpu.VMEM((n_pairs, P, LANES), bf16),
            pltpu.VMEM((n_pairs, SUBLANES, LANES), f32),
            pltpu.VMEM((n_pairs, P, P), bf16),
            pltpu.VMEM((n_pairs, P, P), bf16),
            pltpu.VMEM((n_pairs, P, 2 * P), bf16),
            pltpu.VMEM((n_pairs, P, P), f32),
            pltpu.VMEM((n_pairs, P, P), bf16),
            pltpu.VMEM((n_pairs, P, LANES), bf16),
            pltpu.VMEM((n_pairs, P, LANES), f32),
            pltpu.VMEM((n_pairs, 2 * P, LANES), bf16),
        ] + extra_scratch,
        compiler_params=_cparams(("parallel", "arbitrary")),
        name="rwkv_wkv_fwd" if not reverse else "rwkv_wkv_bwd",
    )(*operands)


def _rwkv_layer(z, mods, norm_g, ctx_len, mu, w_rkv, w_o, dir_vec, w_la, w_lb, a_la, a_lb, g_la, g_lb, r_k, ln_x):
    n_batch, ltot, d = z.shape
    n_pairs = d // LANES
    tm = 768 if ltot % 768 == 0 else 512
    mixes = _rwkv_mix_call(z, mods, norm_g, mu, ctx_len)
    rkv = _rkv_call(mixes, w_rkv.astype(bf16), tm)

    def pad_cols(w):
        return jnp.concatenate([jnp.pad(w[i], ((0, 0), (0, LANES - w.shape[2]))) for i in range(2)], axis=1)

    def pad_rows(w):
        return jnp.pad(w, ((0, 0), (0, LANES - w.shape[1]), (0, 0)))

    w_a = jnp.stack([pad_cols(w_la), pad_cols(a_la), g_la]).astype(bf16)
    h = _lora_a_call(mixes, w_a, tm)
    bias = jnp.pad(jnp.stack([dir_vec[0, 0], dir_vec[1, 0], dir_vec[0, 1], dir_vec[1, 1]]), ((0, 4), (0, 0)))
    pre = _lora_b_call(h, pad_rows(w_lb).astype(bf16), pad_rows(a_lb).astype(bf16), g_lb.astype(bf16), bias, 256)
    rk_row = r_k.reshape(d)
    params = jnp.stack([jnp.stack([dir_vec[di, 2], dir_vec[di, 3], rk_row]) for di in range(2)])
    params = jnp.pad(params, ((0, 0), (0, SUBLANES - 3), (0, 0)))
    params = params.reshape(2, SUBLANES, n_pairs, LANES).transpose(0, 2, 1, 3)
    first = _wkv_call(rkv, pre, params, ctx_len, 0)
    og = _wkv_call(rkv, pre, params, ctx_len, 1, first=first, ln_x=ln_x)
    return _mm_resid_call(og, w_o.astype(bf16), z, mods, ctx_len, tm, chunk=2)


def _rope_tables(ctx_len, seq_len):
    rows = seq_len // GRID_W
    row = np.repeat(np.arange(rows, dtype=np.float32), GRID_W)
    col = np.tile(np.arange(GRID_W, dtype=np.float32), rows)
    n_freq = DIFF_HEAD // 4
    inv = (np.float32(ROPE_BASE) ** (-np.arange(n_freq, dtype=np.float32) / np.float32(n_freq))).astype(np.float32)
    ang = np.concatenate([row[:, None] * inv, col[:, None] * inv], axis=-1).astype(np.float32)
    cos = np.repeat(np.cos(ang), 2, axis=1)
    sin = np.repeat(np.sin(ang), 2, axis=1) * np.tile(np.array([-1.0, 1.0], np.float32), DIFF_HEAD // 2)
    cos = np.concatenate([np.ones((ctx_len, DIFF_HEAD), np.float32), cos], axis=0)
    sin = np.concatenate([np.zeros((ctx_len, DIFF_HEAD), np.float32), sin], axis=0)
    return jnp.asarray(cos, f32), jnp.asarray(sin, f32)


def _slab_products(x_ref, w_ref, tm, consume):
    def product(c):
        return jnp.dot(x_ref[c * ROW_CHUNK:(c + 1) * ROW_CHUNK, :], w_ref[...], preferred_element_type=f32)

    n_slabs = tm // ROW_CHUNK
    nxt = product(0)
    for c in range(n_slabs):
        y = nxt
        if c + 1 < n_slabs:
            nxt = product(c + 1)
        consume(c, y)


def _qk_kernel(mods_ref, g_ref, qkg_ref, cos_ref, sin_ref, z_ref, w_ref, o_ref, h_ref,
               *, tm, tn, d, ctx_len, n_batch):
    b = pl.program_id(0)
    m = pl.program_id(1)
    n = pl.program_id(2)

    @pl.when(n == 0)
    def _():
        _norm_mod_tile(z_ref, h_ref, mods_ref, g_ref, 0, b, m * tm, tm, d, ctx_len, n_batch)

    is_q = n < d // tn
    gain = qkg_ref[pl.ds(jnp.where(is_q, 0, 1), 1), :]
    scale = jnp.where(is_q, DIFF_HEAD ** -0.5 * LOG2E, 1.0)
    even = (lax.broadcasted_iota(jnp.int32, (ROW_CHUNK, LANES), 1) % 2) == 0

    def epilogue(c, y):
        rows = slice(c * ROW_CHUNK, (c + 1) * ROW_CHUNK)
        cos = cos_ref[rows, :]
        sin = sin_ref[rows, :]
        for gi in range(tn // LANES):
            x = y[:, gi * LANES:(gi + 1) * LANES]
            ms = jnp.mean(x * x, axis=-1, keepdims=True)
            x = x * (lax.rsqrt(ms + EPS) * scale) * gain
            partner = jnp.where(even, pltpu.roll(x, LANES - 1, 1), pltpu.roll(x, 1, 1))
            o_ref[rows, gi * LANES:(gi + 1) * LANES] = (x * cos + partner * sin).astype(bf16)

    _slab_products(h_ref, w_ref, tm, epilogue)


def _qk_call(z, mods, norm_g, w_qk, qk_g, cos, sin, ctx_len, tm):
    n_batch, ltot, d = z.shape
    tn = 512
    kern = functools.partial(_qk_kernel, tm=tm, tn=tn, d=d, ctx_len=ctx_len, n_batch=n_batch)
    return pl.pallas_call(
        kern,
        grid=(n_batch, ltot // tm, 2 * d // tn),
        in_specs=[
            pl.BlockSpec(mods.shape, lambda b, m, n: (0, 0)),
            pl.BlockSpec((1, d), lambda b, m, n: (0, 0)),
            pl.BlockSpec((SUBLANES, DIFF_HEAD), lambda b, m, n: (0, 0)),
            pl.BlockSpec((tm, DIFF_HEAD), lambda b, m, n: (m, 0)),
            pl.BlockSpec((tm, DIFF_HEAD), lambda b, m, n: (m, 0)),
            pl.BlockSpec((None, tm, d), lambda b, m, n: (b, m, 0)),
            pl.BlockSpec((d, tn), lambda b, m, n: (0, n)),
        ],
        out_specs=[pl.BlockSpec((None, tm, tn), lambda b, m, n: (b, m, n)),
                   pl.BlockSpec((None, tm, d), lambda b, m, n: (b, m, 0))],
        out_shape=[jax.ShapeDtypeStruct((n_batch, ltot, 2 * d), bf16),
                   jax.ShapeDtypeStruct((n_batch, ltot, d), bf16)],
        compiler_params=_cparams(("parallel", "parallel", "arbitrary")),
        name="diff_qk_proj",
    )(mods, norm_g.reshape(1, d), jnp.pad(qk_g, ((0, SUBLANES - qk_g.shape[0]), (0, 0))), cos, sin, z, w_qk)


def _vt_kernel(h_ref, w_ref, o_ref, *, tm):
    def store(c, y):
        o_ref[:, c * ROW_CHUNK:(c + 1) * ROW_CHUNK] = y.astype(bf16).T

    _slab_products(h_ref, w_ref, tm, store)


def _vt_call(h, w_qkv, tm):
    n_batch, ltot, d = h.shape
    tn = 512
    v_off = 2 * d // tn
    return pl.pallas_call(
        functools.partial(_vt_kernel, tm=tm),
        grid=(n_batch, ltot // tm, d // tn),
        in_specs=[pl.BlockSpec((None, tm, d), lambda b, m, n: (b, m, 0)),
                  pl.BlockSpec((d, tn), lambda b, m, n: (0, v_off + n))],
        out_specs=pl.BlockSpec((None, None, tn, tm), lambda b, m, n: (b, m, n, 0)),
        out_shape=jax.ShapeDtypeStruct((n_batch, ltot // tm, d, tm), bf16),
        compiler_params=_cparams(("parallel", "parallel", "arbitrary")),
        name="diff_v_proj",
    )(h, w_qkv)


def _attn_kernel(lam_ref, sg_ref, q_ref, k_ref, vt_ref, o_ref, sa_s, sb_s, *, lambda_init, ctx_len):
    tq = o_ref.shape[0]
    n_k, _, tk = vt_ref.shape
    hd = DIFF_HEAD
    q_row0 = pl.multiple_of(ctx_len + pl.program_id(2) * tq, 256)
    qt = q_ref[pl.ds(q_row0, tq), :].astype(f32).T
    top = lax.broadcasted_iota(jnp.int32, qt.shape, 0) < hd
    w = jnp.concatenate([jnp.where(top, qt, 0.0), jnp.where(top, 0.0, qt)], axis=1).astype(bf16)
    groups = [(j0, min(ATTN_GROUP, n_k - j0)) for j0 in range(0, n_k, ATTN_GROUP)]
    bufs = (sa_s, sb_s)

    def scores(g):
        j0, nb = groups[g]
        bufs[g % 2][0:nb * tk, :] = jnp.dot(k_ref[j0 * tk:(j0 + nb) * tk, :], w, preferred_element_type=f32)

    def absorb(g, m_old, l_old, acc):
        j0, nb = groups[g]
        s = bufs[g % 2][0:nb * tk, :]
        m_new = jnp.maximum(m_old, jnp.max(s, axis=0, keepdims=True))
        alpha = jnp.exp2(m_old - m_new)
        p = jnp.exp2(s - m_new)
        l_new = alpha * l_old + jnp.sum(p, axis=0, keepdims=True)
        vt = jnp.concatenate([vt_ref[j0 + i] for i in range(nb)], axis=1) if nb > 1 else vt_ref[j0]
        return m_new, l_new, alpha * acc + jnp.dot(vt, p.astype(bf16), preferred_element_type=f32)

    m = jnp.full((1, 2 * tq), -jnp.inf, f32)
    l = jnp.zeros((1, 2 * tq), f32)
    acc = jnp.zeros((2 * hd, 2 * tq), f32)
    scores(0)
    for g in range(len(groups)):
        if g + 1 < len(groups):
            scores(g + 1)
        m, l, acc = absorb(g, m, l, acc)
    lv = lam_ref[...]
    lam = (jnp.exp(jnp.sum(lv[0:1] * lv[1:2], axis=1, keepdims=True))
           - jnp.exp(jnp.sum(lv[2:3] * lv[3:4], axis=1, keepdims=True)) + lambda_init)
    o = acc[:, 0:tq] / l[:, 0:tq] - lam * (acc[:, tq:] / l[:, tq:])
    ms = jnp.mean(o * o, axis=0, keepdims=True)
    o = o * lax.rsqrt(ms + EPS) * sg_ref[...]
    o_ref[...] = o.T.astype(bf16)


def _attn_call(qk, vt, lam_vec, subln_g, ctx_len, lambda_init):
    n_batch, ltot, d2 = qk.shape
    d = d2 // 2
    _, n_kb, _, tk = vt.shape
    hw = 2 * DIFF_HEAD
    n_heads = d // hw
    seq_len = ltot - ctx_len
    tq = 512
    assert ctx_len % 256 == 0 and seq_len % tq == 0
    gain = jnp.broadcast_to((subln_g * (1.0 - lambda_init)).reshape(hw, 1), (hw, tq))
    kern = functools.partial(_attn_kernel, lambda_init=lambda_init, ctx_len=ctx_len)
    return pl.pallas_call(
        kern,
        grid=(n_batch, n_heads, seq_len // tq),
        in_specs=[
            pl.BlockSpec((SUBLANES, DIFF_HEAD), lambda b, h, t: (0, 0)),
            pl.BlockSpec((hw, tq), lambda b, h, t: (0, 0)),
            pl.BlockSpec((None, ltot, hw), lambda b, h, t: (b, 0, h)),
            pl.BlockSpec((None, ltot, hw), lambda b, h, t: (b, 0, n_heads + h)),
            pl.BlockSpec((None, n_kb, hw, tk), lambda b, h, t: (b, 0, h, 0)),
        ],
        out_specs=pl.BlockSpec((None, tq, hw), lambda b, h, t: (b, t, h)),
        out_shape=jax.ShapeDtypeStruct((n_batch, seq_len, d), bf16),
        scratch_shapes=[pltpu.VMEM((min(ATTN_GROUP, n_kb) * tk, 2 * tq), f32),
                        pltpu.VMEM((min(ATTN_GROUP, n_kb) * tk, 2 * tq), f32)],
        compiler_params=_cparams(("parallel", "parallel", "parallel")),
        name="diff_attention",
    )(jnp.pad(lam_vec, ((0, SUBLANES - lam_vec.shape[0]), (0, 0))), gain, qk, qk, vt)


def _diff_layer(z, mods, norm_g, ctx_len, w_qkv, w_o, qk_g, lam_vec, subln_g, lambda_init):
    n_batch, ltot, d = z.shape
    seq_len = ltot - ctx_len
    tm = 768 if ltot % 768 == 0 else 512
    cos, sin = _rope_tables(ctx_len, seq_len)
    w_bf = w_qkv.astype(bf16)
    qk, h = _qk_call(z, mods, norm_g, w_bf, qk_g, cos, sin, ctx_len, tm)
    vt = _vt_call(h, w_bf, tm)
    o = _attn_call(qk, vt, lam_vec, subln_g, ctx_len, lambda_init)
    return _mm_resid_call(o, w_o.astype(bf16), z, mods, 0, 256, chunk=2, row_off=ctx_len)


def kernel(x, c, ctx, c_ctx, ada_w, ada_b, norm_g, ffn_w_in, ffn_w_out, pool_w, pool_scale, rwkv_mu, rwkv_w_rkv, rwkv_w_o, rwkv_dir_vec, rwkv_w_lora_a, rwkv_w_lora_b, rwkv_a_lora_a, rwkv_a_lora_b, rwkv_g_lora_a, rwkv_g_lora_b, rwkv_r_k, rwkv_ln_x, diff_w_qkv, diff_w_o, diff_qk_g, diff_lambda, diff_subln_g):
    n_batch, seq_len, d = x.shape
    depth = ada_w.shape[0]
    ctx_len = ctx.shape[1]
    assert n_batch + 1 <= SUBLANES
    cs = jnp.concatenate([c, c_ctx[None, :], jnp.zeros((SUBLANES - n_batch - 1, d), f32)], axis=0)
    mods_all = _mods_call(cs, ada_w, ada_b)

    last_reader = max((i for i in range(depth) if i % N_MIXERS != 0), default=-1)
    cur_ctx = ctx_len if last_reader >= 0 else 0
    join_in_pool = cur_ctx == 256 and depth > 0
    z = x if (cur_ctx == 0 or join_in_pool) else jnp.concatenate([ctx, x], axis=1)
    w_in_all = ffn_w_in.astype(bf16)
    w_out_all = ffn_w_out.astype(bf16)
    for i in range(depth):
        kind, j = i % N_MIXERS, i // N_MIXERS
        mods = mods_all[i]
        if kind == 0:
            z = _pool_call(z, mods, norm_g[i, 0], pool_w[j], pool_scale[j], cur_ctx,
                           ctx=ctx if (i == 0 and join_in_pool) else None)
        elif kind == 1:
            z = _rwkv_layer(z, mods, norm_g[i, 0], cur_ctx, rwkv_mu[j], rwkv_w_rkv[j], rwkv_w_o[j], rwkv_dir_vec[j],
                            rwkv_w_lora_a[j], rwkv_w_lora_b[j], rwkv_a_lora_a[j], rwkv_a_lora_b[j],
                            rwkv_g_lora_a[j], rwkv_g_lora_b[j], rwkv_r_k[j], rwkv_ln_x[j])
        else:
            lambda_init = 0.8 - 0.6 * math.exp(-0.3 * i)
            z = _diff_layer(z, mods, norm_g[i, 0], cur_ctx, diff_w_qkv[j], diff_w_o[j], diff_qk_g[j],
                            diff_lambda[j], diff_subln_g[j], lambda_init)
            cur_ctx = 0
        if cur_ctx and i >= last_reader:
            z = z[:, cur_ctx:]
            cur_ctx = 0
        ltot = z.shape[1]
        tm = 768 if ltot % 768 == 0 else 512
        z = _ffn_call(z, mods, norm_g[i, 1], w_in_all, w_out_all, i, cur_ctx, tm)
    return z[:, cur_ctx:] if cur_ctx else z
```

```python
import functools
import math

import jax
import jax.numpy as jnp
import numpy as np
from jax import lax
from jax.experimental import pallas as pl
from jax.experimental.pallas import tpu as pltpu

f32 = jnp.float32
bf16 = jnp.bfloat16

N_MIXERS = 3
EPS = 1e-6
POOL_WINDOWS = (2, 4, 8, 16)
POOL_HALO = 8
RWKV_HEAD = 64
LN_X_EPS = 64e-5
DIFF_HEAD = 128
ROPE_BASE = 10000.0
LOG2E = 1.4426950408889634
GRID_W = 64
LANES = 128
SUBLANES = 8
WKV_CHUNK = 64
ROW_CHUNK = 256
NORM_ROWS = 16
ATTN_GROUP = 2
VMEM_LIMIT = 60 * 1024 * 1024

NT_DIMS = (((1,), (1,)), ((), ()))
TN_DIMS = (((0,), (0,)), ((), ()))


def _cparams(sem):
    return pltpu.CompilerParams(dimension_semantics=sem, vmem_limit_bytes=VMEM_LIMIT)


def _dot(a, b, dims=None):
    a = a.astype(bf16)
    b = b.astype(bf16)
    if dims is None:
        return jnp.dot(a, b, preferred_element_type=f32)
    return lax.dot_general(a, b, dims, preferred_element_type=f32)


def _sigmoid(x):
    return 1.0 / (1.0 + jnp.exp(-x))


def _row_ids(shape, row0):
    return lax.broadcasted_iota(jnp.int32, shape, 0) + row0


def _mod_rows(mods_ref, chunk, b, n_batch, is_ctx, d):
    lat = mods_ref[pl.ds(b, 1), chunk * d:(chunk + 1) * d]
    if is_ctx is None:
        return lat
    ctx = mods_ref[n_batch:n_batch + 1, chunk * d:(chunk + 1) * d]
    return jnp.where(is_ctx, ctx, lat)


def _norm_mod(x, g, shift, scale):
    ms = jnp.mean(x * x, axis=-1, keepdims=True)
    h = x * lax.rsqrt(ms + EPS) * g
    return h * (1.0 + scale) + shift


def _is_ctx(shape, row0, ctx_len):
    if ctx_len == 0:
        return None
    return _row_ids(shape, row0) < ctx_len


def _mods_kernel(s_ref, w_ref, b_ref, o_ref):
    s = s_ref[...]
    s = s * _sigmoid(s)
    o_ref[...] = _dot(s, w_ref[...]) + b_ref[...]


def _mods_call(cs, ada_w, ada_b):
    depth, d, n6 = ada_w.shape
    tn = 1024
    return pl.pallas_call(
        _mods_kernel,
        grid=(depth, n6 // tn),
        in_specs=[
            pl.BlockSpec((SUBLANES, d), lambda l, n: (0, 0)),
            pl.BlockSpec((None, d, tn), lambda l, n: (l, 0, n)),
            pl.BlockSpec((None, 1, tn), lambda l, n: (l, 0, n)),
        ],
        out_specs=pl.BlockSpec((None, SUBLANES, tn), lambda l, n: (l, 0, n)),
        out_shape=jax.ShapeDtypeStruct((depth, SUBLANES, n6), f32),
        compiler_params=_cparams(("parallel", "parallel")),
        name="adaln_mods",
    )(cs, ada_w, ada_b.reshape(depth, 1, n6))


def _halo_specs(tq, d, n_tiles):
    per = tq // POOL_HALO
    last = n_tiles * per - 1
    cur = pl.BlockSpec((None, tq, d), lambda b, t: (b, t, 0))
    prev = pl.BlockSpec((None, POOL_HALO, d), lambda b, t: (b, jnp.maximum(t * per - 1, 0), 0))
    nxt = pl.BlockSpec((None, POOL_HALO, d), lambda b, t: (b, jnp.minimum((t + 1) * per, last), 0))
    return cur, prev, nxt


def _stream_edges(t, tq, ctx_len, n_tiles):
    ctx_tiles = ctx_len // tq
    first = t == 0
    last = t == n_tiles - 1
    if ctx_tiles:
        first = first | (t == ctx_tiles)
        last = last | (t == ctx_tiles - 1)
    return jnp.logical_not(first), jnp.logical_not(last)


def _pool_kernel(*refs, tq, d, ctx_len, n_tiles, n_batch, seq_len, split):
    if split:
        mods_ref, g_ref, w_ref, ls_ref, zc_ref, zp_ref, zn_ref, ctx_ref, o_ref, ext_ref = refs
    else:
        mods_ref, g_ref, w_ref, ls_ref, zc_ref, zp_ref, zn_ref, o_ref, ext_ref = refs
        ctx_ref = None
    b = pl.program_id(0)
    t = pl.program_id(1)
    row0 = t * tq
    g = g_ref[...]

    def hmod(x, r0):
        ic = _is_ctx((x.shape[0], 1), r0, ctx_len)
        sh = _mod_rows(mods_ref, 0, b, n_batch, ic, d)
        sc = _mod_rows(mods_ref, 1, b, n_batch, ic, d)
        return _norm_mod(x, g, sh, sc)

    has_prev, has_next = _stream_edges(t, tq, ctx_len, n_tiles)
    zc = zc_ref[...] if ctx_ref is None else jnp.where(t == 0, ctx_ref[...], zc_ref[...])
    hc = hmod(zc, row0)
    hp = jnp.where(has_prev, hmod(zp_ref[...], row0 - POOL_HALO), 0.0)
    hn = jnp.where(has_next, hmod(zn_ref[...], row0 + tq), 0.0)
    ext_ref[0:POOL_HALO, :] = hp
    ext_ref[POOL_HALO:POOL_HALO + tq, :] = hc
    ext_ref[POOL_HALO + tq:, :] = hn

    rows = _row_ids((tq, 1), row0)
    if ctx_len:
        in_ctx = rows < ctx_len
        pos = jnp.where(in_ctx, rows, rows - ctx_len)
        slen = jnp.where(in_ctx, ctx_len, seq_len)
    else:
        pos, slen = rows, seq_len

    cg = d // len(POOL_WINDOWS)
    ys = []
    for gi, win in enumerate(POOL_WINDOWS):
        lo_off, hi_off = win // 2, win - win // 2
        c0 = gi * cg
        acc = None
        for off in range(-lo_off, hi_off):
            piece = ext_ref[POOL_HALO + off:POOL_HALO + off + tq, c0:c0 + cg]
            acc = piece if acc is None else acc + piece
        cnt = jnp.minimum(pos + hi_off, slen) - jnp.maximum(pos - lo_off, 0)
        p = acc / cnt.astype(f32) - hc[:, c0:c0 + cg]
        ys.append(_dot(p, w_ref[gi]))
    y = jnp.concatenate(ys, axis=1) * ls_ref[...]
    ic = _is_ctx((tq, 1), row0, ctx_len)
    gate = _mod_rows(mods_ref, 2, b, n_batch, ic, d)
    o_ref[...] = zc + gate * y


def _pool_call(z, mods, norm_g, pool_w, pool_scale, ctx_len, ctx=None):
    n_batch, rows_z, d = z.shape
    tq = 256
    split = ctx is not None
    ltot = rows_z + (ctx_len if split else 0)
    n_tiles = ltot // tq
    ng, cg, _ = pool_w.shape
    if split:
        assert ctx_len == tq and ctx.shape[1] == tq
        per = tq // POOL_HALO
        last = (n_tiles - 1) * per - 1
        cur = pl.BlockSpec((None, tq, d), lambda b, t: (b, jnp.maximum(t - 1, 0), 0))
        prev = pl.BlockSpec((None, POOL_HALO, d), lambda b, t: (b, jnp.maximum((t - 1) * per - 1, 0), 0))
        nxt = pl.BlockSpec((None, POOL_HALO, d), lambda b, t: (b, jnp.minimum(t * per, last), 0))
        extra_specs = [pl.BlockSpec((None, tq, d), lambda b, t: (b, 0, 0))]
        extra_args = [ctx]
    else:
        cur, prev, nxt = _halo_specs(tq, d, n_tiles)
        extra_specs, extra_args = [], []
    kern = functools.partial(_pool_kernel, tq=tq, d=d, ctx_len=ctx_len, n_tiles=n_tiles,
                             n_batch=n_batch, seq_len=ltot - ctx_len, split=split)
    return pl.pallas_call(
        kern,
        grid=(n_batch, n_tiles),
        in_specs=[
            pl.BlockSpec(mods.shape, lambda b, t: (0, 0)),
            pl.BlockSpec((1, d), lambda b, t: (0, 0)),
            pl.BlockSpec((ng, cg, cg), lambda b, t: (0, 0, 0)),
            pl.BlockSpec((1, d), lambda b, t: (0, 0)),
            cur, prev, nxt,
        ] + extra_specs,
        out_specs=pl.BlockSpec((None, tq, d), lambda b, t: (b, t, 0)),
        out_shape=jax.ShapeDtypeStruct((n_batch, ltot, d), f32),
        scratch_shapes=[pltpu.VMEM((tq + 2 * POOL_HALO, d), f32)],
        compiler_params=_cparams(("parallel", "parallel")),
        name="pool_mix",
    )(mods, norm_g.reshape(1, d), pool_w.astype(bf16), pool_scale.reshape(1, d), z, z, z, *extra_args)


def _for_rows(n_rows, body):
    def step(i, carry):
        body(pl.multiple_of(i * ROW_CHUNK, ROW_CHUNK))
        return carry
    lax.fori_loop(0, n_rows // ROW_CHUNK, step, 0)


def _norm_mod_tile(z_ref, h_ref, mods_ref, g_ref, chunk, b, row0, tm, d, ctx_len, n_batch, out_off=0):
    assert ctx_len % NORM_ROWS == 0 and tm % (4 * NORM_ROWS) == 0

    def step(i, carry):
        r0 = pl.multiple_of(i * NORM_ROWS, NORM_ROWS)
        ic = None if ctx_len == 0 else (row0 + r0) < ctx_len
        sh = _mod_rows(mods_ref, chunk, b, n_batch, ic, d)
        sc = _mod_rows(mods_ref, chunk + 1, b, n_batch, ic, d)
        h = _norm_mod(z_ref[pl.ds(r0, NORM_ROWS), :], g_ref[...], sh, sc)
        h_ref[pl.ds(out_off + r0, NORM_ROWS), :] = h.astype(h_ref.dtype)
        return carry
    lax.fori_loop(0, tm // NORM_ROWS, step, 0, unroll=4)


def _ffn_kernel(mods_ref, g_ref, z_ref, wg_ref, wu_ref, wo_ref, o_ref, h_ref,
                *, tm, d, ctx_len, n_batch, n_f):
    b = pl.program_id(0)
    m = pl.program_id(1)
    f = pl.program_id(2)

    @pl.when(f == 0)
    def _():
        _norm_mod_tile(z_ref, h_ref, mods_ref, g_ref, 3, b, m * tm, tm, d, ctx_len, n_batch)
        o_ref[...] = jnp.zeros((tm, d), f32)

    def gate_up(c):
        h = h_ref[c * ROW_CHUNK:(c + 1) * ROW_CHUNK, :]
        return (jnp.dot(h, wg_ref[...], preferred_element_type=f32),
                jnp.dot(h, wu_ref[...], preferred_element_type=f32))

    n_slabs = tm // ROW_CHUNK
    nxt = gate_up(0)
    for c in range(n_slabs):
        gate, up = nxt
        if c + 1 < n_slabs:
            nxt = gate_up(c + 1)
        act = (gate * _sigmoid(gate) * up).astype(bf16)
        o_ref[c * ROW_CHUNK:(c + 1) * ROW_CHUNK, :] += jnp.dot(act, wo_ref[...], preferred_element_type=f32)

    @pl.when(f == n_f - 1)
    def _():
        def epi(r0):
            rows = pl.ds(r0, ROW_CHUNK)
            ic = _is_ctx((ROW_CHUNK, 1), m * tm + r0, ctx_len)
            g2 = _mod_rows(mods_ref, 5, b, n_batch, ic, d)
            o_ref[rows, :] = z_ref[rows, :] + g2 * o_ref[rows, :]
        _for_rows(tm, epi)


def _ffn_call(z, mods, norm_g, w_in, w_out, layer, ctx_len, tm):
    n_batch, ltot, d = z.shape
    fh = w_out.shape[1]
    tf = 512
    n_f = fh // tf
    kern = functools.partial(_ffn_kernel, tm=tm, d=d, ctx_len=ctx_len, n_batch=n_batch, n_f=n_f)
    return pl.pallas_call(
        kern,
        grid=(n_batch, ltot // tm, n_f),
        in_specs=[
            pl.BlockSpec(mods.shape, lambda b, m, f: (0, 0)),
            pl.BlockSpec((1, d), lambda b, m, f: (0, 0)),
            pl.BlockSpec((None, tm, d), lambda b, m, f: (b, m, 0)),
            pl.BlockSpec((None, d, tf), lambda b, m, f: (layer, 0, f)),
            pl.BlockSpec((None, d, tf), lambda b, m, f: (layer, 0, f + n_f)),
            pl.BlockSpec((None, tf, d), lambda b, m, f: (layer, f, 0)),
        ],
        out_specs=pl.BlockSpec((None, tm, d), lambda b, m, f: (b, m, 0)),
        out_shape=jax.ShapeDtypeStruct((n_batch, ltot, d), f32),
        scratch_shapes=[pltpu.VMEM((tm, d), bf16)],
        compiler_params=_cparams(("parallel", "parallel", "arbitrary")),
        name="swiglu_ffn",
    )(mods, norm_g.reshape(1, d), z, w_in, w_in, w_out)


def _mm_resid_kernel(mods_ref, x_ref, w_ref, z_ref, o_ref, *, tm, ctx_len, n_batch, row_off):
    b = pl.program_id(0)
    m = pl.program_id(1)
    lat = mods_ref[pl.ds(b, 1), :]

    def residual(c, y):
        rows = slice(c * ROW_CHUNK, (c + 1) * ROW_CHUNK)
        ic = _is_ctx((ROW_CHUNK, 1), m * tm + row_off + c * ROW_CHUNK, ctx_len)
        gate = lat if ic is None else jnp.where(ic, mods_ref[n_batch:n_batch + 1, :], lat)
        o_ref[rows, :] = z_ref[rows, :] + gate * y

    _slab_products(x_ref, w_ref, tm, residual)


def _mm_resid_call(x, w, z, mods, ctx_len, tm, chunk, row_off=0):
    n_batch, rows, k = x.shape
    d = w.shape[1]
    tn = 1024
    assert row_off % tm == 0 and rows % tm == 0
    m_off = row_off // tm
    g_off = chunk * d // tn
    kern = functools.partial(_mm_resid_kernel, tm=tm, ctx_len=ctx_len, n_batch=n_batch, row_off=row_off)
    return pl.pallas_call(
        kern,
        grid=(n_batch, rows // tm, d // tn),
        in_specs=[
            pl.BlockSpec((SUBLANES, tn), lambda b, m, n: (0, g_off + n)),
            pl.BlockSpec((None, tm, k), lambda b, m, n: (b, m, 0)),
            pl.BlockSpec((k, tn), lambda b, m, n: (0, n)),
            pl.BlockSpec((None, tm, tn), lambda b, m, n: (b, m + m_off, n)),
        ],
        out_specs=pl.BlockSpec((None, tm, tn), lambda b, m, n: (b, m, n)),
        out_shape=jax.ShapeDtypeStruct((n_batch, rows, d), f32),
        compiler_params=_cparams(("parallel", "parallel", "arbitrary")),
        name="proj_residual",
    )(mods, x, w, z)


def _rwkv_mix_kernel(mods_ref, g_ref, mu_ref, zc_ref, zp_ref, zn_ref, o_ref, ext_ref,
                     *, tq, d, ctx_len, n_tiles, n_batch):
    b = pl.program_id(0)
    t = pl.program_id(1)
    row0 = t * tq
    g = g_ref[...]

    def hmod(x, r0):
        ic = _is_ctx((x.shape[0], 1), r0, ctx_len)
        sh = _mod_rows(mods_ref, 0, b, n_batch, ic, d)
        sc = _mod_rows(mods_ref, 1, b, n_batch, ic, d)
        return _norm_mod(x, g, sh, sc)

    has_prev, has_next = _stream_edges(t, tq, ctx_len, n_tiles)
    ext_ref[0:POOL_HALO, :] = jnp.where(has_prev, hmod(zp_ref[...], row0 - POOL_HALO), 0.0)
    _norm_mod_tile(zc_ref, ext_ref, mods_ref, g_ref, 0, b, row0, tq, d, ctx_len, n_batch, out_off=POOL_HALO)
    ext_ref[POOL_HALO + tq:, :] = jnp.where(has_next, hmod(zn_ref[...], row0 + tq), 0.0)

    for r0 in range(0, tq, NORM_ROWS):
        hc = ext_ref[POOL_HALO + r0:POOL_HALO + r0 + NORM_ROWS, :]
        up = ext_ref[POOL_HALO - 1 + r0:POOL_HALO - 1 + r0 + NORM_ROWS, :]
        dn = ext_ref[POOL_HALO + 1 + r0:POOL_HALO + 1 + r0 + NORM_ROWS, :]
        xx = 0.5 * (up + dn) - hc
        for mi in range(6):
            o_ref[mi, r0:r0 + NORM_ROWS, :] = (hc + xx * mu_ref[mi:mi + 1, :]).astype(bf16)


def _rwkv_mix_call(z, mods, norm_g, mu, ctx_len):
    n_batch, ltot, d = z.shape
    tq = 256
    n_tiles = ltot // tq
    cur, prev, nxt = _halo_specs(tq, d, n_tiles)
    kern = functools.partial(_rwkv_mix_kernel, tq=tq, d=d, ctx_len=ctx_len, n_tiles=n_tiles, n_batch=n_batch)
    return pl.pallas_call(
        kern,
        grid=(n_batch, n_tiles),
        in_specs=[
            pl.BlockSpec(mods.shape, lambda b, t: (0, 0)),
            pl.BlockSpec((1, d), lambda b, t: (0, 0)),
            pl.BlockSpec((SUBLANES, d), lambda b, t: (0, 0)),
            cur, prev, nxt,
        ],
        out_specs=pl.BlockSpec((6, None, tq, d), lambda b, t: (0, b, t, 0)),
        out_shape=jax.ShapeDtypeStruct((6, n_batch, ltot, d), bf16),
        scratch_shapes=[pltpu.VMEM((tq + 2 * POOL_HALO, d), f32)],
        compiler_params=_cparams(("parallel", "parallel")),
        name="rwkv_shift_mix",
    )(mods, norm_g.reshape(1, d), jnp.pad(mu, ((0, SUBLANES - mu.shape[0]), (0, 0))), z, z, z)


def _rkv_kernel(x_ref, w_ref, o_ref, *, tm, tn):
    def store(c, y):
        for gi in range(tn // LANES):
            o_ref[gi, c * ROW_CHUNK:(c + 1) * ROW_CHUNK, :] = y[:, gi * LANES:(gi + 1) * LANES]

    _slab_products(x_ref, w_ref, tm, store)


def _rkv_call(mixes, w_rkv, tm):
    _, n_batch, ltot, d = mixes.shape
    tn = 1024
    src = (0, 2, 3)

    def x_map(b, m, p, n):
        return (jnp.where(p == 0, src[0], jnp.where(p == 1, src[1], src[2])), b, m, 0)

    return pl.pallas_call(
        functools.partial(_rkv_kernel, tm=tm, tn=tn),
        grid=(n_batch, ltot // tm, 3, d // tn),
        in_specs=[
            pl.BlockSpec((None, None, tm, d), x_map),
            pl.BlockSpec((None, d, tn), lambda b, m, p, n: (p, 0, n)),
        ],
        out_specs=pl.BlockSpec((None, None, tn // LANES, tm, LANES), lambda b, m, p, n: (p, b, n, m, 0)),
        out_shape=jax.ShapeDtypeStruct((3, n_batch, d // LANES, ltot, LANES), f32),
        compiler_params=_cparams(("parallel", "parallel", "arbitrary", "arbitrary")),
        name="rwkv_rkv_proj",
    )(mixes, w_rkv)


def _lora_a_kernel(x_ref, w_ref, o_ref):
    o_ref[...] = jnp.dot(x_ref[...], w_ref[...], preferred_element_type=f32)


def _lora_a_call(mixes, w_a, tm):
    _, n_batch, ltot, d = mixes.shape
    nh = w_a.shape[2]
    src = (1, 4, 5)

    def x_map(b, m, p):
        return (jnp.where(p == 0, src[0], jnp.where(p == 1, src[1], src[2])), b, m, 0)

    return pl.pallas_call(
        _lora_a_kernel,
        grid=(n_batch, ltot // tm, 3),
        in_specs=[
            pl.BlockSpec((None, None, tm, d), x_map),
            pl.BlockSpec((None, d, nh), lambda b, m, p: (p, 0, 0)),
        ],
        out_specs=pl.BlockSpec((None, None, tm, nh), lambda b, m, p: (p, b, m, 0)),
        out_shape=jax.ShapeDtypeStruct((3, n_batch, ltot, nh), f32),
        compiler_params=_cparams(("parallel", "parallel", "arbitrary")),
        name="rwkv_lora_a",
    )(mixes, w_a)


def _lora_b_kernel(h_ref, wb_ref, ab_ref, gb_ref, bias_ref, o_ref, *, d):
    n_pairs = d // LANES
    hw = jnp.tanh(h_ref[0])
    ha = h_ref[1]
    hg = _sigmoid(h_ref[2])
    outs = []
    for di in range(2):
        outs.append(_dot(hw[:, di * LANES:(di + 1) * LANES], wb_ref[di]) + bias_ref[di:di + 1, :])
    for di in range(2):
        outs.append(_dot(ha[:, di * LANES:(di + 1) * LANES], ab_ref[di]) + bias_ref[2 + di:3 + di, :])
    outs.append(_dot(hg, gb_ref[...]))
    for oi, y in enumerate(outs):
        for p in range(n_pairs):
            o_ref[oi, p] = y[:, p * LANES:(p + 1) * LANES]


def _lora_b_call(h, w_b, a_b, g_b, bias, tm):
    _, n_batch, ltot, nh = h.shape
    d = g_b.shape[1]
    n_pairs = d // LANES
    return pl.pallas_call(
        functools.partial(_lora_b_kernel, d=d),
        grid=(n_batch, ltot // tm),
        in_specs=[
            pl.BlockSpec((3, None, tm, nh), lambda b, m: (0, b, m, 0)),
            pl.BlockSpec(w_b.shape, lambda b, m: (0, 0, 0)),
            pl.BlockSpec(a_b.shape, lambda b, m: (0, 0, 0)),
            pl.BlockSpec(g_b.shape, lambda b, m: (0, 0)),
            pl.BlockSpec(bias.shape, lambda b, m: (0, 0)),
        ],
        out_specs=pl.BlockSpec((5, None, n_pairs, tm, LANES), lambda b, m: (0, b, 0, m, 0)),
        out_shape=jax.ShapeDtypeStruct((5, n_batch, n_pairs, ltot, LANES), f32),
        compiler_params=_cparams(("parallel", "parallel")),
        name="rwkv_lora_b",
    )(h, w_b, a_b, g_b, bias)


def _seg_sum(x, lo):
    s_lo = jnp.sum(jnp.where(lo, x, 0.0), axis=1, keepdims=True)
    s_all = jnp.sum(x, axis=1, keepdims=True)
    return jnp.where(lo, s_lo, s_all - s_lo)


def _wkv_kernel(*refs, n_pairs, reverse, fused):
    if fused:
        (r_ref, k_ref, v_ref, w_ref, a_ref, par_ref, y0_ref, bon0_ref, gate_ref, ln_ref, og_ref, s_ref,
         lhs_s, rhs1_s, rhs2_s, vbd_s, dec_s, mab_s, mak_s, arbk_s, tinv_s, tmp_s, w_s, g1r_s, uv_s, bon_ref) = refs
    else:
        (r_ref, k_ref, v_ref, w_ref, a_ref, par_ref, y_ref, bon_ref, s_ref,
         lhs_s, rhs1_s, rhs2_s, vbd_s, dec_s, mab_s, mak_s, arbk_s, tinv_s, tmp_s, w_s, g1r_s, uv_s) = refs
    _wkv_body(r_ref, k_ref, v_ref, w_ref, a_ref, par_ref, bon_ref, s_ref,
              lhs_s, rhs1_s, rhs2_s, vbd_s, dec_s, mab_s, mak_s, arbk_s, tinv_s, tmp_s, w_s, g1r_s, uv_s,
              (y0_ref, bon0_ref, gate_ref, ln_ref, og_ref) if fused else (y_ref,),
              n_pairs=n_pairs, reverse=reverse)


def _wkv_body(r_ref, k_ref, v_ref, w_ref, a_ref, par_ref, bon_ref, s_ref,
              lhs_s, rhs1_s, rhs2_s, vbd_s, dec_s, mab_s, mak_s, arbk_s, tinv_s, tmp_s, w_s, g1r_s, uv_s,
              out_refs, *, n_pairs, reverse):
    c = pl.program_id(1)
    L = WKV_CHUNK
    P = 2 * L

    @pl.when(c == 0)
    def _():
        s_ref[...] = jnp.zeros_like(s_ref)

    lane = lax.broadcasted_iota(jnp.int32, (L, LANES), 1)
    lo = lane < RWKV_HEAD
    ri = lax.broadcasted_iota(jnp.int32, (P, P), 0)
    ci = lax.broadcasted_iota(jnp.int32, (P, P), 1)
    same = (ri // L) == (ci // L)
    ii, jj = ri % L, ci % L
    if reverse:
        strict, incl = same & (jj > ii), same & (jj >= ii)
    else:
        strict, incl = same & (jj < ii), same & (jj <= ii)
    eye = (ri == ci).astype(f32)
    ti = lax.broadcasted_iota(jnp.int32, (L, L), 0)
    tj = lax.broadcasted_iota(jnp.int32, (L, L), 1)
    tri = ((tj >= ti) if reverse else (tj <= ti)).astype(bf16)
    diag_p = ((lax.broadcasted_iota(jnp.int32, (P, LANES), 0) // L)
              == (lax.broadcasted_iota(jnp.int32, (P, LANES), 1) // RWKV_HEAD))

    def blockdiag(x):
        return jnp.where(diag_p, jnp.concatenate([x, x], axis=0), 0.0)

    def stack(x):
        return jnp.concatenate([x, x], axis=0)

    def prepare(p):
        r, k, v = r_ref[p], k_ref[p], v_ref[p]
        k_k = par_ref[p, 0:1, :]
        k_a = par_ref[p, 1:2, :]
        r_k = par_ref[p, 2:3, :]
        lw = -math.exp(-0.5) * _sigmoid(w_ref[p])
        a = _sigmoid(a_ref[p])
        kn = k * k_k
        kk = kn * lax.rsqrt(jnp.maximum(_seg_sum(kn * kn, lo), 1e-24))
        kd = k * (1.0 + (a - 1.0) * k_a)
        bon_ref[p] = _seg_sum(r * kd * r_k, lo) * v

        lw_hi = lw.astype(bf16)
        lw_lo = (lw - lw_hi.astype(f32)).astype(bf16)
        cum = (jnp.dot(tri, lw_hi, preferred_element_type=f32)
               + jnp.dot(tri, lw_lo, preferred_element_type=f32))
        tot = cum[0:1, :] if reverse else cum[L - 1:L, :]
        e_neg = jnp.exp(-cum)
        e_end = jnp.exp(tot - cum)
        b_in = kk * a
        at = -kk * jnp.exp(cum - lw)
        rt = r * jnp.exp(cum)
        lhs_s[p] = jnp.concatenate([blockdiag(at), blockdiag(rt)], axis=0).astype(bf16)
        rhs1_s[p] = jnp.concatenate([stack(b_in * e_neg), stack(kd * e_neg)], axis=0).astype(bf16)
        rhs2_s[p] = jnp.concatenate([stack(b_in * e_end), stack(kd * e_end)], axis=0).astype(bf16)
        vbd_s[p] = blockdiag(v).astype(bf16)
        dec_s[p] = jnp.broadcast_to(jnp.exp(tot), (SUBLANES, LANES))

    def scores(p):
        sc = lax.dot_general(lhs_s[p], rhs1_s[p], NT_DIMS, preferred_element_type=f32)
        m_ab = jnp.where(strict, sc[0:P, 0:P], 0.0)
        mab_s[p] = m_ab.astype(bf16)
        mak_s[p] = jnp.where(strict, sc[0:P, P:2 * P], 0.0).astype(bf16)
        arbk_s[p, :, 0:P] = jnp.where(incl, sc[P:2 * P, 0:P], 0.0).astype(bf16)
        arbk_s[p, :, P:2 * P] = jnp.where(incl, sc[P:2 * P, P:2 * P], 0.0).astype(bf16)
        tinv_s[p] = eye + jnp.where((ri // 2) == (ci // 2), m_ab, 0.0)

    def invert(pairs):
        s = 4
        while s <= L:
            level = ((ri // s) == (ci // s)) & ((ri // (s // 2)) != (ci // (s // 2)))
            for p in pairs:
                e = jnp.where(level, mab_s[p], jnp.zeros((P, P), bf16))
                tmp_s[p] = _dot(tinv_s[p], e).astype(bf16)
            for p in pairs:
                t_inv = tinv_s[p]
                tinv_s[p] = t_inv + _dot(tmp_s[p], t_inv)
            s *= 2

    def read_state(p):
        g1 = _dot(lhs_s[p], s_ref[p], NT_DIMS)
        z = jnp.dot(mak_s[p], vbd_s[p], preferred_element_type=f32)
        w_s[p] = (g1[0:P] + z).astype(bf16)
        g1r_s[p] = g1[P:2 * P]

    def solve(p):
        uv_s[p, 0:P, :] = _dot(tinv_s[p], w_s[p]).astype(bf16)
        uv_s[p, P:2 * P, :] = vbd_s[p]

    def emit(p):
        y = g1r_s[p] + jnp.dot(arbk_s[p], uv_s[p], preferred_element_type=f32)
        y = y[0:L] + y[L:P]
        if len(out_refs) == 1:
            out_refs[0][p] = y
        else:
            y0_ref, bon0_ref, gate_ref, ln_ref, og_ref = out_refs
            y = y + y0_ref[p]
            mean = _seg_sum(y, lo) * (1.0 / RWKV_HEAD)
            yc = y - mean
            var = _seg_sum(yc * yc, lo) * (1.0 / RWKV_HEAD)
            sl = slice(p * LANES, (p + 1) * LANES)
            o = yc * lax.rsqrt(var + LN_X_EPS) * ln_ref[0:1, sl] + ln_ref[1:2, sl] + bon0_ref[p] + bon_ref[p]
            og_ref[:, sl] = (o * gate_ref[p]).astype(bf16)
        upd = lax.dot_general(uv_s[p], rhs2_s[p], TN_DIMS, preferred_element_type=f32)
        s_ref[p] = s_ref[p] * dec_s[p, 0:1, :] + jnp.where(diag_p, upd, 0.0)

    def finish(pairs):
        for stage in (read_state, solve, emit):
            for p in pairs:
                stage(p)

    pairs = range(n_pairs)
    for p in pairs:
        prepare(p)
    for p in pairs:
        scores(p)
    invert(pairs)
    finish(pairs)


def _wkv_call(rkv, pre, params, ctx_len, direction, first=None, ln_x=None):
    _, n_batch, n_pairs, ltot, _ = rkv.shape
    fused = first is not None
    L = WKV_CHUNK
    P = 2 * L
    n_chunks = ltot // L
    ctx_chunks = ctx_len // L
    reverse = direction == 1

    def chunk_of(c):
        if not reverse:
            return c
        return jnp.where(c < ctx_chunks, ctx_chunks - 1 - c, n_chunks - 1 - (c - ctx_chunks))

    def spec(lead):
        return pl.BlockSpec((None, None, n_pairs, L, LANES), lambda b, c: (lead, b, 0, chunk_of(c), 0))

    pair_spec = pl.BlockSpec((None, n_pairs, L, LANES), lambda b, c: (b, 0, chunk_of(c), 0))
    pair_sds = jax.ShapeDtypeStruct((n_batch, n_pairs, ltot, LANES), f32)
    in_specs = [spec(0), spec(1), spec(2),
                pl.BlockSpec((None, None, n_pairs, L, LANES), lambda b, c: (direction, b, 0, chunk_of(c), 0)),
                pl.BlockSpec((None, None, n_pairs, L, LANES), lambda b, c: (2 + direction, b, 0, chunk_of(c), 0)),
                pl.BlockSpec((None, n_pairs, SUBLANES, LANES), lambda b, c: (direction, 0, 0, 0))]
    operands = [rkv, rkv, rkv, pre, pre, params]
    extra_scratch = []
    if fused:
        d = n_pairs * LANES
        in_specs += [pair_spec, pair_spec,
                     pl.BlockSpec((None, None, n_pairs, L, LANES), lambda b, c: (4, b, 0, chunk_of(c), 0)),
                     pl.BlockSpec((SUBLANES, d), lambda b, c: (0, 0))]
        operands += [first[0], first[1], pre, jnp.pad(ln_x, ((0, SUBLANES - ln_x.shape[0]), (0, 0)))]
        out_specs = pl.BlockSpec((None, L, d), lambda b, c: (b, chunk_of(c), 0))
        out_shape = jax.ShapeDtypeStruct((n_batch, ltot, d), bf16)
        extra_scratch = [pltpu.VMEM((n_pairs, L, LANES), f32)]
    else:
        out_specs = [pair_spec, pair_spec]
        out_shape = [pair_sds, pair_sds]
    return pl.pallas_call(
        functools.partial(_wkv_kernel, n_pairs=n_pairs, reverse=reverse, fused=fused),
        grid=(n_batch, n_chunks),
        in_specs=in_specs,
        out_specs=out_specs,
        out_shape=out_shape,
        scratch_shapes=[
            pltpu.VMEM((n_pairs, P, LANES), f32),
            pltpu.VMEM((n_pairs, 2 * P, LANES), bf16),
            pltpu.VMEM((n_pairs, 2 * P, LANES), bf16),
            pltpu.VMEM((n_pairs, 2 * P, LANES), bf16),
            pltpu.VMEM((n_pairs, P, LANES), bf16),
            pltpu.VMEM((n_pairs, SUBLANES, LANES), f32),
            pltpu.VMEM((n_pairs, P, P), bf16),
            pltpu.VMEM((n_pairs, P, P), bf16),
            pltpu.VMEM((n_pairs, P, 2 * P), bf16),
            pltpu.VMEM((n_pairs, P, P), f32),
            pltpu.VMEM((n_pairs, P, P), bf16),
            pltpu.VMEM((n_pairs, P, LANES), bf16),
            pltpu.VMEM((n_pairs, P, LANES), f32),
            pltpu.VMEM((n_pairs, 2 * P, LANES), bf16),
        ] + extra_scratch,
        compiler_params=_cparams(("parallel", "arbitrary")),
        name="rwkv_wkv_fwd" if not reverse else "rwkv_wkv_bwd",
    )(*operands)


def _rwkv_layer(z, mods, norm_g, ctx_len, mu, w_rkv, w_o, dir_vec, w_la, w_lb, a_la, a_lb, g_la, g_lb, r_k, ln_x):
    n_batch, ltot, d = z.shape
    n_pairs = d // LANES
    tm = 768 if ltot % 768 == 0 else 512
    mixes = _rwkv_mix_call(z, mods, norm_g, mu, ctx_len)
    rkv = _rkv_call(mixes, w_rkv.astype(bf16), tm)

    def pad_cols(w):
        return jnp.concatenate([jnp.pad(w[i], ((0, 0), (0, LANES - w.shape[2]))) for i in range(2)], axis=1)

    def pad_rows(w):
        return jnp.pad(w, ((0, 0), (0, LANES - w.shape[1]), (0, 0)))

    w_a = jnp.stack([pad_cols(w_la), pad_cols(a_la), g_la]).astype(bf16)
    h = _lora_a_call(mixes, w_a, tm)
    bias = jnp.pad(jnp.stack([dir_vec[0, 0], dir_vec[1, 0], dir_vec[0, 1], dir_vec[1, 1]]), ((0, 4), (0, 0)))
    pre = _lora_b_call(h, pad_rows(w_lb).astype(bf16), pad_rows(a_lb).astype(bf16), g_lb.astype(bf16), bias, 256)
    rk_row = r_k.reshape(d)
    params = jnp.stack([jnp.stack([dir_vec[di, 2], dir_vec[di, 3], rk_row]) for di in range(2)])
    params = jnp.pad(params, ((0, 0), (0, SUBLANES - 3), (0, 0)))
    params = params.reshape(2, SUBLANES, n_pairs, LANES).transpose(0, 2, 1, 3)
    first = _wkv_call(rkv, pre, params, ctx_len, 0)
    og = _wkv_call(rkv, pre, params, ctx_len, 1, first=first, ln_x=ln_x)
    return _mm_resid_call(og, w_o.astype(bf16), z, mods, ctx_len, tm, chunk=2)


def _rope_tables(ctx_len, seq_len):
    rows = seq_len // GRID_W
    row = np.repeat(np.arange(rows, dtype=np.float32), GRID_W)
    col = np.tile(np.arange(GRID_W, dtype=np.float32), rows)
    n_freq = DIFF_HEAD // 4
    inv = (np.float32(ROPE_BASE) ** (-np.arange(n_freq, dtype=np.float32) / np.float32(n_freq))).astype(np.float32)
    ang = np.concatenate([row[:, None] * inv, col[:, None] * inv], axis=-1).astype(np.float32)
    cos = np.repeat(np.cos(ang), 2, axis=1)
    sin = np.repeat(np.sin(ang), 2, axis=1) * np.tile(np.array([-1.0, 1.0], np.float32), DIFF_HEAD // 2)
    cos = np.concatenate([np.ones((ctx_len, DIFF_HEAD), np.float32), cos], axis=0)
    sin = np.concatenate([np.zeros((ctx_len, DIFF_HEAD), np.float32), sin], axis=0)
    return jnp.asarray(cos, f32), jnp.asarray(sin, f32)


def _slab_products(x_ref, w_ref, tm, consume):
    def product(c):
        return jnp.dot(x_ref[c * ROW_CHUNK:(c + 1) * ROW_CHUNK, :], w_ref[...], preferred_element_type=f32)

    n_slabs = tm // ROW_CHUNK
    nxt = product(0)
    for c in range(n_slabs):
        y = nxt
        if c + 1 < n_slabs:
            nxt = product(c + 1)
        consume(c, y)


def _qk_kernel(mods_ref, g_ref, qkg_ref, cos_ref, sin_ref, z_ref, w_ref, o_ref, h_ref,
               *, tm, tn, d, ctx_len, n_batch):
    b = pl.program_id(0)
    m = pl.program_id(1)
    n = pl.program_id(2)

    @pl.when(n == 0)
    def _():
        _norm_mod_tile(z_ref, h_ref, mods_ref, g_ref, 0, b, m * tm, tm, d, ctx_len, n_batch)

    is_q = n < d // tn
    gain = qkg_ref[pl.ds(jnp.where(is_q, 0, 1), 1), :]
    scale = jnp.where(is_q, DIFF_HEAD ** -0.5 * LOG2E, 1.0)
    even = (lax.broadcasted_iota(jnp.int32, (ROW_CHUNK, LANES), 1) % 2) == 0

    def epilogue(c, y):
        rows = slice(c * ROW_CHUNK, (c + 1) * ROW_CHUNK)
        cos = cos_ref[rows, :]
        sin = sin_ref[rows, :]
        for gi in range(tn // LANES):
            x = y[:, gi * LANES:(gi + 1) * LANES]
            ms = jnp.mean(x * x, axis=-1, keepdims=True)
            x = x * (lax.rsqrt(ms + EPS) * scale) * gain
            partner = jnp.where(even, pltpu.roll(x, LANES - 1, 1), pltpu.roll(x, 1, 1))
            o_ref[rows, gi * LANES:(gi + 1) * LANES] = (x * cos + partner * sin).astype(bf16)

    _slab_products(h_ref, w_ref, tm, epilogue)


def _qk_call(z, mods, norm_g, w_qk, qk_g, cos, sin, ctx_len, tm):
    n_batch, ltot, d = z.shape
    tn = 512
    kern = functools.partial(_qk_kernel, tm=tm, tn=tn, d=d, ctx_len=ctx_len, n_batch=n_batch)
    return pl.pallas_call(
        kern,
        grid=(n_batch, ltot // tm, 2 * d // tn),
        in_specs=[
            pl.BlockSpec(mods.shape, lambda b, m, n: (0, 0)),
            pl.BlockSpec((1, d), lambda b, m, n: (0, 0)),
            pl.BlockSpec((SUBLANES, DIFF_HEAD), lambda b, m, n: (0, 0)),
            pl.BlockSpec((tm, DIFF_HEAD), lambda b, m, n: (m, 0)),
            pl.BlockSpec((tm, DIFF_HEAD), lambda b, m, n: (m, 0)),
            pl.BlockSpec((None, tm, d), lambda b, m, n: (b, m, 0)),
            pl.BlockSpec((d, tn), lambda b, m, n: (0, n)),
        ],
        out_specs=[pl.BlockSpec((None, tm, tn), lambda b, m, n: (b, m, n)),
                   pl.BlockSpec((None, tm, d), lambda b, m, n: (b, m, 0))],
        out_shape=[jax.ShapeDtypeStruct((n_batch, ltot, 2 * d), bf16),
                   jax.ShapeDtypeStruct((n_batch, ltot, d), bf16)],
        compiler_params=_cparams(("parallel", "parallel", "arbitrary")),
        name="diff_qk_proj",
    )(mods, norm_g.reshape(1, d), jnp.pad(qk_g, ((0, SUBLANES - qk_g.shape[0]), (0, 0))), cos, sin, z, w_qk)


def _vt_kernel(h_ref, w_ref, o_ref, *, tm):
    def store(c, y):
        o_ref[:, c * ROW_CHUNK:(c + 1) * ROW_CHUNK] = y.astype(bf16).T

    _slab_products(h_ref, w_ref, tm, store)


def _vt_call(h, w_qkv, tm):
    n_batch, ltot, d = h.shape
    tn = 512
    v_off = 2 * d // tn
    return pl.pallas_call(
        functools.partial(_vt_kernel, tm=tm),
        grid=(n_batch, ltot // tm, d // tn),
        in_specs=[pl.BlockSpec((None, tm, d), lambda b, m, n: (b, m, 0)),
                  pl.BlockSpec((d, tn), lambda b, m, n: (0, v_off + n))],
        out_specs=pl.BlockSpec((None, None, tn, tm), lambda b, m, n: (b, m, n, 0)),
        out_shape=jax.ShapeDtypeStruct((n_batch, ltot // tm, d, tm), bf16),
        compiler_params=_cparams(("parallel", "parallel", "arbitrary")),
        name="diff_v_proj",
    )(h, w_qkv)


def _attn_kernel(lam_ref, sg_ref, q_ref, k_ref, vt_ref, o_ref, sa_s, sb_s, *, lambda_init, ctx_len):
    tq = o_ref.shape[0]
    n_k, _, tk = vt_ref.shape
    hd = DIFF_HEAD
    q_row0 = pl.multiple_of(ctx_len + pl.program_id(2) * tq, 256)
    qt = q_ref[pl.ds(q_row0, tq), :].astype(f32).T
    top = lax.broadcasted_iota(jnp.int32, qt.shape, 0) < hd
    w = jnp.concatenate([jnp.where(top, qt, 0.0), jnp.where(top, 0.0, qt)], axis=1).astype(bf16)
    groups = [(j0, min(ATTN_GROUP, n_k - j0)) for j0 in range(0, n_k, ATTN_GROUP)]
    bufs = (sa_s, sb_s)

    def scores(g):
        j0, nb = groups[g]
        bufs[g % 2][0:nb * tk, :] = jnp.dot(k_ref[j0 * tk:(j0 + nb) * tk, :], w, preferred_element_type=f32)

    def absorb(g, m_old, l_old, acc):
        j0, nb = groups[g]
        s = bufs[g % 2][0:nb * tk, :]
        m_new = jnp.maximum(m_old, jnp.max(s, axis=0, keepdims=True))
        alpha = jnp.exp2(m_old - m_new)
        p = jnp.exp2(s - m_new)
        l_new = alpha * l_old + jnp.sum(p, axis=0, keepdims=True)
        vt = jnp.concatenate([vt_ref[j0 + i] for i in range(nb)], axis=1) if nb > 1 else vt_ref[j0]
        return m_new, l_new, alpha * acc + jnp.dot(vt, p.astype(bf16), preferred_element_type=f32)

    m = jnp.full((1, 2 * tq), -jnp.inf, f32)
    l = jnp.zeros((1, 2 * tq), f32)
    acc = jnp.zeros((2 * hd, 2 * tq), f32)
    scores(0)
    for g in range(len(groups)):
        if g + 1 < len(groups):
            scores(g + 1)
        m, l, acc = absorb(g, m, l, acc)
    lv = lam_ref[...]
    lam = (jnp.exp(jnp.sum(lv[0:1] * lv[1:2], axis=1, keepdims=True))
           - jnp.exp(jnp.sum(lv[2:3] * lv[3:4], axis=1, keepdims=True)) + lambda_init)
    o = acc[:, 0:tq] / l[:, 0:tq] - lam * (acc[:, tq:] / l[:, tq:])
    ms = jnp.mean(o * o, axis=0, keepdims=True)
    o = o * lax.rsqrt(ms + EPS) * sg_ref[...]
    o_ref[...] = o.T.astype(bf16)


def _attn_call(qk, vt, lam_vec, subln_g, ctx_len, lambda_init):
    n_batch, ltot, d2 = qk.shape
    d = d2 // 2
    _, n_kb, _, tk = vt.shape
    hw = 2 * DIFF_HEAD
    n_heads = d // hw
    seq_len = ltot - ctx_len
    tq = 512
    assert ctx_len % 256 == 0 and seq_len % tq == 0
    gain = jnp.broadcast_to((subln_g * (1.0 - lambda_init)).reshape(hw, 1), (hw, tq))
    kern = functools.partial(_attn_kernel, lambda_init=lambda_init, ctx_len=ctx_len)
    return pl.pallas_call(
        kern,
        grid=(n_batch, n_heads, seq_len // tq),
        in_specs=[
            pl.BlockSpec((SUBLANES, DIFF_HEAD), lambda b, h, t: (0, 0)),
            pl.BlockSpec((hw, tq), lambda b, h, t: (0, 0)),
            pl.BlockSpec((None, ltot, hw), lambda b, h, t: (b, 0, h)),
            pl.BlockSpec((None, ltot, hw), lambda b, h, t: (b, 0, n_heads + h)),
            pl.BlockSpec((None, n_kb, hw, tk), lambda b, h, t: (b, 0, h, 0)),
        ],
        out_specs=pl.BlockSpec((None, tq, hw), lambda b, h, t: (b, t, h)),
        out_shape=jax.ShapeDtypeStruct((n_batch, seq_len, d), bf16),
        scratch_shapes=[pltpu.VMEM((min(ATTN_GROUP, n_kb) * tk, 2 * tq), f32),
                        pltpu.VMEM((min(ATTN_GROUP, n_kb) * tk, 2 * tq), f32)],
        compiler_params=_cparams(("parallel", "parallel", "parallel")),
        name="diff_attention",
    )(jnp.pad(lam_vec, ((0, SUBLANES - lam_vec.shape[0]), (0, 0))), gain, qk, qk, vt)


def _diff_layer(z, mods, norm_g, ctx_len, w_qkv, w_o, qk_g, lam_vec, subln_g, lambda_init):
    n_batch, ltot, d = z.shape
    seq_len = ltot - ctx_len
    tm = 768 if ltot % 768 == 0 else 512
    cos, sin = _rope_tables(ctx_len, seq_len)
    w_bf = w_qkv.astype(bf16)
    qk, h = _qk_call(z, mods, norm_g, w_bf, qk_g, cos, sin, ctx_len, tm)
    vt = _vt_call(h, w_bf, tm)
    o = _attn_call(qk, vt, lam_vec, subln_g, ctx_len, lambda_init)
    return _mm_resid_call(o, w_o.astype(bf16), z, mods, 0, 256, chunk=2, row_off=ctx_len)


def kernel(x, c, ctx, c_ctx, ada_w, ada_b, norm_g, ffn_w_in, ffn_w_out, pool_w, pool_scale, rwkv_mu, rwkv_w_rkv, rwkv_w_o, rwkv_dir_vec, rwkv_w_lora_a, rwkv_w_lora_b, rwkv_a_lora_a, rwkv_a_lora_b, rwkv_g_lora_a, rwkv_g_lora_b, rwkv_r_k, rwkv_ln_x, diff_w_qkv, diff_w_o, diff_qk_g, diff_lambda, diff_subln_g):
    n_batch, seq_len, d = x.shape
    depth = ada_w.shape[0]
    ctx_len = ctx.shape[1]
    assert n_batch + 1 <= SUBLANES
    cs = jnp.concatenate([c, c_ctx[None, :], jnp.zeros((SUBLANES - n_batch - 1, d), f32)], axis=0)
    mods_all = _mods_call(cs, ada_w, ada_b)

    last_reader = max((i for i in range(depth) if i % N_MIXERS != 0), default=-1)
    cur_ctx = ctx_len if last_reader >= 0 else 0
    join_in_pool = cur_ctx == 256 and depth > 0
    z = x if (cur_ctx == 0 or join_in_pool) else jnp.concatenate([ctx, x], axis=1)
    w_in_all = ffn_w_in.astype(bf16)
    w_out_all = ffn_w_out.astype(bf16)
    for i in range(depth):
        kind, j = i % N_MIXERS, i // N_MIXERS
        mods = mods_all[i]
        if kind == 0:
            z = _pool_call(z, mods, norm_g[i, 0], pool_w[j], pool_scale[j], cur_ctx,
                           ctx=ctx if (i == 0 and join_in_pool) else None)
        elif kind == 1:
            z = _rwkv_layer(z, mods, norm_g[i, 0], cur_ctx, rwkv_mu[j], rwkv_w_rkv[j], rwkv_w_o[j], rwkv_dir_vec[j],
                            rwkv_w_lora_a[j], rwkv_w_lora_b[j], rwkv_a_lora_a[j], rwkv_a_lora_b[j],
                            rwkv_g_lora_a[j], rwkv_g_lora_b[j], rwkv_r_k[j], rwkv_ln_x[j])
        else:
            lambda_init = 0.8 - 0.6 * math.exp(-0.3 * i)
            z = _diff_layer(z, mods, norm_g[i, 0], cur_ctx, diff_w_qkv[j], diff_w_o[j], diff_qk_g[j],
                            diff_lambda[j], diff_subln_g[j], lambda_init)
            cur_ctx = 0
        if cur_ctx and i >= last_reader:
            z = z[:, cur_ctx:]
            cur_ctx = 0
        ltot = z.shape[1]
        tm = 768 if ltot % 768 == 0 else 512
        z = _ffn_call(z, mods, norm_g[i, 1], w_in_all, w_out_all, i, cur_ctx, tm)
    return z[:, cur_ctx:] if cur_ctx else z
```

```python
import functools
import math

import jax
import jax.numpy as jnp
import numpy as np
from jax import lax
from jax.experimental import pallas as pl
from jax.experimental.pallas import tpu as pltpu

f32 = jnp.float32
bf16 = jnp.bfloat16

N_MIXERS = 3
EPS = 1e-6
POOL_WINDOWS = (2, 4, 8, 16)
POOL_HALO = 8
RWKV_HEAD = 64
LN_X_EPS = 64e-5
DIFF_HEAD = 128
ROPE_BASE = 10000.0
LOG2E = 1.4426950408889634
GRID_W = 64
LANES = 128
SUBLANES = 8
WKV_CHUNK = 64
ROW_CHUNK = 256
NORM_ROWS = 16
ATTN_GROUP = 2
VMEM_LIMIT = 60 * 1024 * 1024

NT_DIMS = (((1,), (1,)), ((), ()))
TN_DIMS = (((0,), (0,)), ((), ()))


def _cparams(sem):
    return pltpu.CompilerParams(dimension_semantics=sem, vmem_limit_bytes=VMEM_LIMIT)


def _dot(a, b, dims=None):
    a = a.astype(bf16)
    b = b.astype(bf16)
    if dims is None:
        return jnp.dot(a, b, preferred_element_type=f32)
    return lax.dot_general(a, b, dims, preferred_element_type=f32)


def _sigmoid(x):
    return 1.0 / (1.0 + jnp.exp(-x))


def _row_ids(shape, row0):
    return lax.broadcasted_iota(jnp.int32, shape, 0) + row0


def _mod_rows(mods_ref, chunk, b, n_batch, is_ctx, d):
    lat = mods_ref[pl.ds(b, 1), chunk * d:(chunk + 1) * d]
    if is_ctx is None:
        return lat
    ctx = mods_ref[n_batch:n_batch + 1, chunk * d:(chunk + 1) * d]
    return jnp.where(is_ctx, ctx, lat)


def _norm_mod(x, g, shift, scale):
    ms = jnp.mean(x * x, axis=-1, keepdims=True)
    h = x * lax.rsqrt(ms + EPS) * g
    return h * (1.0 + scale) + shift


def _is_ctx(shape, row0, ctx_len):
    if ctx_len == 0:
        return None
    return _row_ids(shape, row0) < ctx_len


def _mods_kernel(s_ref, w_ref, b_ref, o_ref):
    s = s_ref[...]
    s = s * _sigmoid(s)
    o_ref[...] = _dot(s, w_ref[...]) + b_ref[...]


def _mods_call(cs, ada_w, ada_b):
    depth, d, n6 = ada_w.shape
    tn = 1024
    return pl.pallas_call(
        _mods_kernel,
        grid=(depth, n6 // tn),
        in_specs=[
            pl.BlockSpec((SUBLANES, d), lambda l, n: (0, 0)),
            pl.BlockSpec((None, d, tn), lambda l, n: (l, 0, n)),
            pl.BlockSpec((None, 1, tn), lambda l, n: (l, 0, n)),
        ],
        out_specs=pl.BlockSpec((None, SUBLANES, tn), lambda l, n: (l, 0, n)),
        out_shape=jax.ShapeDtypeStruct((depth, SUBLANES, n6), f32),
        compiler_params=_cparams(("parallel", "parallel")),
        name="adaln_mods",
    )(cs, ada_w, ada_b.reshape(depth, 1, n6))


def _halo_specs(tq, d, n_tiles):
    per = tq // POOL_HALO
    last = n_tiles * per - 1
    cur = pl.BlockSpec((None, tq, d), lambda b, t: (b, t, 0))
    prev = pl.BlockSpec((None, POOL_HALO, d), lambda b, t: (b, jnp.maximum(t * per - 1, 0), 0))
    nxt = pl.BlockSpec((None, POOL_HALO, d), lambda b, t: (b, jnp.minimum((t + 1) * per, last), 0))
    return cur, prev, nxt


def _stream_edges(t, tq, ctx_len, n_tiles):
    ctx_tiles = ctx_len // tq
    first = t == 0
    last = t == n_tiles - 1
    if ctx_tiles:
        first = first | (t == ctx_tiles)
        last = last | (t == ctx_tiles - 1)
    return jnp.logical_not(first), jnp.logical_not(last)


def _pool_kernel(*refs, tq, d, ctx_len, n_tiles, n_batch, seq_len, split):
    if split:
        mods_ref, g_ref, w_ref, ls_ref, zc_ref, zp_ref, zn_ref, ctx_ref, o_ref, ext_ref, wa_ref, wb_ref = refs
    else:
        mods_ref, g_ref, w_ref, ls_ref, zc_ref, zp_ref, zn_ref, o_ref, ext_ref, wa_ref, wb_ref = refs
        ctx_ref = None
    b = pl.program_id(0)
    t = pl.program_id(1)
    row0 = t * tq
    g = g_ref[...]

    def hmod(x, r0):
        ic = _is_ctx((x.shape[0], 1), r0, ctx_len)
        sh = _mod_rows(mods_ref, 0, b, n_batch, ic, d)
        sc = _mod_rows(mods_ref, 1, b, n_batch, ic, d)
        return _norm_mod(x, g, sh, sc)

    has_prev, has_next = _stream_edges(t, tq, ctx_len, n_tiles)
    if ctx_ref is None:
        src_ref = zc_ref
    else:
        o_ref[...] = jnp.where(t == 0, ctx_ref[...], zc_ref[...])
        src_ref = o_ref
    ext_ref[0:POOL_HALO, :] = jnp.where(has_prev, hmod(zp_ref[...], row0 - POOL_HALO), 0.0)
    _norm_mod_tile(src_ref, ext_ref, mods_ref, g_ref, 0, b, row0, tq, d, ctx_len, n_batch, out_off=POOL_HALO)
    ext_ref[POOL_HALO + tq:, :] = jnp.where(has_next, hmod(zn_ref[...], row0 + tq), 0.0)

    rows = _row_ids((tq, 1), row0)
    if ctx_len:
        in_ctx = rows < ctx_len
        pos = jnp.where(in_ctx, rows, rows - ctx_len)
        slen = jnp.where(in_ctx, ctx_len, seq_len)
    else:
        pos, slen = rows, seq_len

    cg = d // len(POOL_WINDOWS)
    ys = []
    for gi, win in enumerate(POOL_WINDOWS):
        lo_off, hi_off = win // 2, win - win // 2
        c0 = gi * cg
        cols = slice(c0, c0 + cg)
        levels = win.bit_length() - 1
        assert win == 1 << levels and win // 2 <= POOL_HALO
        src, span, reach = ext_ref, 1, tq + 2 * POOL_HALO
        for k in range(levels - 1):
            dst = wa_ref if k % 2 == 0 else wb_ref
            reach -= span
            if k == 0:
                dst[0:reach, :] = src[0:reach, cols] + src[span:span + reach, cols]
            else:
                dst[0:reach, :] = src[0:reach, :] + src[span:span + reach, :]
            src, span = dst, 2 * span
        start = POOL_HALO - lo_off
        if levels == 1:
            acc = ext_ref[start:start + tq, cols] + ext_ref[start + 1:start + 1 + tq, cols]
        else:
            acc = src[start:start + tq, :] + src[start + span:start + span + tq, :]
        cnt = jnp.minimum(pos + hi_off, slen) - jnp.maximum(pos - lo_off, 0)
        p = acc * (1.0 / cnt.astype(f32)) - ext_ref[POOL_HALO:POOL_HALO + tq, c0:c0 + cg]
        ys.append(_dot(p, w_ref[gi]))
    y = jnp.concatenate(ys, axis=1) * ls_ref[...]
    ic = _is_ctx((tq, 1), row0, ctx_len)
    gate = _mod_rows(mods_ref, 2, b, n_batch, ic, d)
    o_ref[...] = src_ref[...] + gate * y


def _pool_call(z, mods, norm_g, pool_w, pool_scale, ctx_len, ctx=None):
    n_batch, rows_z, d = z.shape
    tq = 256
    split = ctx is not None
    ltot = rows_z + (ctx_len if split else 0)
    n_tiles = ltot // tq
    ng, cg, _ = pool_w.shape
    if split:
        assert ctx_len == tq and ctx.shape[1] == tq
        per = tq // POOL_HALO
        last = (n_tiles - 1) * per - 1
        cur = pl.BlockSpec((None, tq, d), lambda b, t: (b, jnp.maximum(t - 1, 0), 0))
        prev = pl.BlockSpec((None, POOL_HALO, d), lambda b, t: (b, jnp.maximum((t - 1) * per - 1, 0), 0))
        nxt = pl.BlockSpec((None, POOL_HALO, d), lambda b, t: (b, jnp.minimum(t * per, last), 0))
        extra_specs = [pl.BlockSpec((None, tq, d), lambda b, t: (b, 0, 0))]
        extra_args = [ctx]
    else:
        cur, prev, nxt = _halo_specs(tq, d, n_tiles)
        extra_specs, extra_args = [], []
    kern = functools.partial(_pool_kernel, tq=tq, d=d, ctx_len=ctx_len, n_tiles=n_tiles,
                             n_batch=n_batch, seq_len=ltot - ctx_len, split=split)
    return pl.pallas_call(
        kern,
        grid=(n_batch, n_tiles),
        in_specs=[
            pl.BlockSpec(mods.shape, lambda b, t: (0, 0)),
            pl.BlockSpec((1, d), lambda b, t: (0, 0)),
            pl.BlockSpec((ng, cg, cg), lambda b, t: (0, 0, 0)),
            pl.BlockSpec((1, d), lambda b, t: (0, 0)),
            cur, prev, nxt,
        ] + extra_specs,
        out_specs=pl.BlockSpec((None, tq, d), lambda b, t: (b, t, 0)),
        out_shape=jax.ShapeDtypeStruct((n_batch, ltot, d), f32),
        scratch_shapes=[pltpu.VMEM((tq + 2 * POOL_HALO, d), f32),
                        pltpu.VMEM((tq + 2 * POOL_HALO, cg), f32),
                        pltpu.VMEM((tq + 2 * POOL_HALO, cg), f32)],
        compiler_params=_cparams(("parallel", "parallel")),
        name="pool_mix",
    )(mods, norm_g.reshape(1, d), pool_w.astype(bf16), pool_scale.reshape(1, d), z, z, z, *extra_args)


def _for_rows(n_rows, body):
    def step(i, carry):
        body(pl.multiple_of(i * ROW_CHUNK, ROW_CHUNK))
        return carry
    lax.fori_loop(0, n_rows // ROW_CHUNK, step, 0)


def _norm_mod_tile(z_ref, h_ref, mods_ref, g_ref, chunk, b, row0, tm, d, ctx_len, n_batch, out_off=0):
    assert ctx_len % NORM_ROWS == 0 and tm % (4 * NORM_ROWS) == 0

    def step(i, carry):
        r0 = pl.multiple_of(i * NORM_ROWS, NORM_ROWS)
        ic = None if ctx_len == 0 else (row0 + r0) < ctx_len
        sh = _mod_rows(mods_ref, chunk, b, n_batch, ic, d)
        sc = _mod_rows(mods_ref, chunk + 1, b, n_batch, ic, d)
        h = _norm_mod(z_ref[pl.ds(r0, NORM_ROWS), :], g_ref[...], sh, sc)
        h_ref[pl.ds(out_off + r0, NORM_ROWS), :] = h.astype(h_ref.dtype)
        return carry
    lax.fori_loop(0, tm // NORM_ROWS, step, 0, unroll=4)


def _ffn_kernel(mods_ref, g_ref, z_ref, wg_ref, wu_ref, wo_ref, o_ref, h_ref,
                *, tm, d, ctx_len, n_batch, n_f):
    b = pl.program_id(0)
    m = pl.program_id(1)
    f = pl.program_id(2)

    @pl.when(f == 0)
    def _():
        _norm_mod_tile(z_ref, h_ref, mods_ref, g_ref, 3, b, m * tm, tm, d, ctx_len, n_batch)
        o_ref[...] = jnp.zeros((tm, d), f32)

    def gate_up(c):
        h = h_ref[c * ROW_CHUNK:(c + 1) * ROW_CHUNK, :]
        return (jnp.dot(h, wg_ref[...], preferred_element_type=f32),
                jnp.dot(h, wu_ref[...], preferred_element_type=f32))

    n_slabs = tm // ROW_CHUNK
    nxt = gate_up(0)
    for c in range(n_slabs):
        gate, up = nxt
        if c + 1 < n_slabs:
            nxt = gate_up(c + 1)
        act = (gate * _sigmoid(gate) * up).astype(bf16)
        o_ref[c * ROW_CHUNK:(c + 1) * ROW_CHUNK, :] += jnp.dot(act, wo_ref[...], preferred_element_type=f32)

    @pl.when(f == n_f - 1)
    def _():
        def epi(r0):
            rows = pl.ds(r0, ROW_CHUNK)
            ic = _is_ctx((ROW_CHUNK, 1), m * tm + r0, ctx_len)
            g2 = _mod_rows(mods_ref, 5, b, n_batch, ic, d)
            o_ref[rows, :] = z_ref[rows, :] + g2 * o_ref[rows, :]
        _for_rows(tm, epi)


def _ffn_call(z, mods, norm_g, w_in, w_out, layer, ctx_len, tm):
    n_batch, ltot, d = z.shape
    fh = w_out.shape[1]
    tf = 512
    n_f = fh // tf
    kern = functools.partial(_ffn_kernel, tm=tm, d=d, ctx_len=ctx_len, n_batch=n_batch, n_f=n_f)
    return pl.pallas_call(
        kern,
        grid=(n_batch, ltot // tm, n_f),
        in_specs=[
            pl.BlockSpec(mods.shape, lambda b, m, f: (0, 0)),
            pl.BlockSpec((1, d), lambda b, m, f: (0, 0)),
            pl.BlockSpec((None, tm, d), lambda b, m, f: (b, m, 0)),
            pl.BlockSpec((None, d, tf), lambda b, m, f: (layer, 0, f)),
            pl.BlockSpec((None, d, tf), lambda b, m, f: (layer, 0, f + n_f)),
            pl.BlockSpec((None, tf, d), lambda b, m, f: (layer, f, 0)),
        ],
        out_specs=pl.BlockSpec((None, tm, d), lambda b, m, f: (b, m, 0)),
        out_shape=jax.ShapeDtypeStruct((n_batch, ltot, d), f32),
        scratch_shapes=[pltpu.VMEM((tm, d), bf16)],
        compiler_params=_cparams(("parallel", "parallel", "arbitrary")),
        name="swiglu_ffn",
    )(mods, norm_g.reshape(1, d), z, w_in, w_in, w_out)


def _mm_resid_kernel(mods_ref, x_ref, w_ref, z_ref, o_ref, *, tm, ctx_len, n_batch, row_off):
    b = pl.program_id(0)
    m = pl.program_id(1)
    lat = mods_ref[pl.ds(b, 1), :]

    def residual(c, y):
        rows = slice(c * ROW_CHUNK, (c + 1) * ROW_CHUNK)
        ic = _is_ctx((ROW_CHUNK, 1), m * tm + row_off + c * ROW_CHUNK, ctx_len)
        gate = lat if ic is None else jnp.where(ic, mods_ref[n_batch:n_batch + 1, :], lat)
        o_ref[rows, :] = z_ref[rows, :] + gate * y

    _slab_products(x_ref, w_ref, tm, residual)


def _mm_resid_call(x, w, z, mods, ctx_len, tm, chunk, row_off=0):
    n_batch, rows, k = x.shape
    d = w.shape[1]
    tn = 1024
    assert row_off % tm == 0 and rows % tm == 0
    m_off = row_off // tm
    g_off = chunk * d // tn
    kern = functools.partial(_mm_resid_kernel, tm=tm, ctx_len=ctx_len, n_batch=n_batch, row_off=row_off)
    return pl.pallas_call(
        kern,
        grid=(n_batch, rows // tm, d // tn),
        in_specs=[
            pl.BlockSpec((SUBLANES, tn), lambda b, m, n: (0, g_off + n)),
            pl.BlockSpec((None, tm, k), lambda b, m, n: (b, m, 0)),
            pl.BlockSpec((k, tn), lambda b, m, n: (0, n)),
            pl.BlockSpec((None, tm, tn), lambda b, m, n: (b, m + m_off, n)),
        ],
        out_specs=pl.BlockSpec((None, tm, tn), lambda b, m, n: (b, m, n)),
        out_shape=jax.ShapeDtypeStruct((n_batch, rows, d), f32),
        compiler_params=_cparams(("parallel", "parallel", "arbitrary")),
        name="proj_residual",
    )(mods, x, w, z)


def _rwkv_mix_kernel(mods_ref, g_ref, mu_ref, zc_ref, zp_ref, zn_ref, o_ref, ext_ref,
                     *, tq, d, ctx_len, n_tiles, n_batch):
    b = pl.program_id(0)
    t = pl.program_id(1)
    row0 = t * tq
    g = g_ref[...]

    def hmod(x, r0):
        ic = _is_ctx((x.shape[0], 1), r0, ctx_len)
        sh = _mod_rows(mods_ref, 0, b, n_batch, ic, d)
        sc = _mod_rows(mods_ref, 1, b, n_batch, ic, d)
        return _norm_mod(x, g, sh, sc)

    has_prev, has_next = _stream_edges(t, tq, ctx_len, n_tiles)
    ext_ref[0:POOL_HALO, :] = jnp.where(has_prev, hmod(zp_ref[...], row0 - POOL_HALO), 0.0)
    _norm_mod_tile(zc_ref, ext_ref, mods_ref, g_ref, 0, b, row0, tq, d, ctx_len, n_batch, out_off=POOL_HALO)
    ext_ref[POOL_HALO + tq:, :] = jnp.where(has_next, hmod(zn_ref[...], row0 + tq), 0.0)

    for r0 in range(0, tq, NORM_ROWS):
        hc = ext_ref[POOL_HALO + r0:POOL_HALO + r0 + NORM_ROWS, :]
        up = ext_ref[POOL_HALO - 1 + r0:POOL_HALO - 1 + r0 + NORM_ROWS, :]
        dn = ext_ref[POOL_HALO + 1 + r0:POOL_HALO + 1 + r0 + NORM_ROWS, :]
        xx = 0.5 * (up + dn) - hc
        for mi in range(6):
            o_ref[mi, r0:r0 + NORM_ROWS, :] = (hc + xx * mu_ref[mi:mi + 1, :]).astype(bf16)


def _rwkv_mix_call(z, mods, norm_g, mu, ctx_len):
    n_batch, ltot, d = z.shape
    tq = 256
    n_tiles = ltot // tq
    cur, prev, nxt = _halo_specs(tq, d, n_tiles)
    kern = functools.partial(_rwkv_mix_kernel, tq=tq, d=d, ctx_len=ctx_len, n_tiles=n_tiles, n_batch=n_batch)
    return pl.pallas_call(
        kern,
        grid=(n_batch, n_tiles),
        in_specs=[
            pl.BlockSpec(mods.shape, lambda b, t: (0, 0)),
            pl.BlockSpec((1, d), lambda b, t: (0, 0)),
            pl.BlockSpec((SUBLANES, d), lambda b, t: (0, 0)),
            cur, prev, nxt,
        ],
        out_specs=pl.BlockSpec((6, None, tq, d), lambda b, t: (0, b, t, 0)),
        out_shape=jax.ShapeDtypeStruct((6, n_batch, ltot, d), bf16),
        scratch_shapes=[pltpu.VMEM((tq + 2 * POOL_HALO, d), f32)],
        compiler_params=_cparams(("parallel", "parallel")),
        name="rwkv_shift_mix",
    )(mods, norm_g.reshape(1, d), jnp.pad(mu, ((0, SUBLANES - mu.shape[0]), (0, 0))), z, z, z)


def _rkv_kernel(x_ref, w_ref, o_ref, *, tm, tn):
    def store(c, y):
        for gi in range(tn // LANES):
            o_ref[gi, c * ROW_CHUNK:(c + 1) * ROW_CHUNK, :] = y[:, gi * LANES:(gi + 1) * LANES]

    _slab_products(x_ref, w_ref, tm, store)


def _rkv_call(mixes, w_rkv, tm):
    _, n_batch, ltot, d = mixes.shape
    tn = 1024
    src = (0, 2, 3)

    def x_map(b, m, p, n):
        return (jnp.where(p == 0, src[0], jnp.where(p == 1, src[1], src[2])), b, m, 0)

    return pl.pallas_call(
        functools.partial(_rkv_kernel, tm=tm, tn=tn),
        grid=(n_batch, ltot // tm, 3, d // tn),
        in_specs=[
            pl.BlockSpec((None, None, tm, d), x_map),
            pl.BlockSpec((None, d, tn), lambda b, m, p, n: (p, 0, n)),
        ],
        out_specs=pl.BlockSpec((None, None, tn // LANES, tm, LANES), lambda b, m, p, n: (p, b, n, m, 0)),
        out_shape=jax.ShapeDtypeStruct((3, n_batch, d // LANES, ltot, LANES), f32),
        compiler_params=_cparams(("parallel", "parallel", "arbitrary", "arbitrary")),
        name="rwkv_rkv_proj",
    )(mixes, w_rkv)


def _lora_a_kernel(x_ref, w_ref, o_ref):
    o_ref[...] = jnp.dot(x_ref[...], w_ref[...], preferred_element_type=f32)


def _lora_a_call(mixes, w_a, tm):
    _, n_batch, ltot, d = mixes.shape
    nh = w_a.shape[2]
    src = (1, 4, 5)

    def x_map(b, m, p):
        return (jnp.where(p == 0, src[0], jnp.where(p == 1, src[1], src[2])), b, m, 0)

    return pl.pallas_call(
        _lora_a_kernel,
        grid=(n_batch, ltot // tm, 3),
        in_specs=[
            pl.BlockSpec((None, None, tm, d), x_map),
            pl.BlockSpec((None, d, nh), lambda b, m, p: (p, 0, 0)),
        ],
        out_specs=pl.BlockSpec((None, None, tm, nh), lambda b, m, p: (p, b, m, 0)),
        out_shape=jax.ShapeDtypeStruct((3, n_batch, ltot, nh), f32),
        compiler_params=_cparams(("parallel", "parallel", "arbitrary")),
        name="rwkv_lora_a",
    )(mixes, w_a)


def _lora_b_kernel(h_ref, wb_ref, ab_ref, gb_ref, bias_ref, o_ref, *, d):
    n_pairs = d // LANES
    hw = jnp.tanh(h_ref[0])
    ha = h_ref[1]
    hg = _sigmoid(h_ref[2])
    outs = []
    for di in range(2):
        outs.append(_dot(hw[:, di * LANES:(di + 1) * LANES], wb_ref[di]) + bias_ref[di:di + 1, :])
    for di in range(2):
        outs.append(_dot(ha[:, di * LANES:(di + 1) * LANES], ab_ref[di]) + bias_ref[2 + di:3 + di, :])
    outs.append(_dot(hg, gb_ref[...]))
    for oi, y in enumerate(outs):
        for p in range(n_pairs):
            o_ref[oi, p] = y[:, p * LANES:(p + 1) * LANES]


def _lora_b_call(h, w_b, a_b, g_b, bias, tm):
    _, n_batch, ltot, nh = h.shape
    d = g_b.shape[1]
    n_pairs = d // LANES
    return pl.pallas_call(
        functools.partial(_lora_b_kernel, d=d),
        grid=(n_batch, ltot // tm),
        in_specs=[
            pl.BlockSpec((3, None, tm, nh), lambda b, m: (0, b, m, 0)),
            pl.BlockSpec(w_b.shape, lambda b, m: (0, 0, 0)),
            pl.BlockSpec(a_b.shape, lambda b, m: (0, 0, 0)),
            pl.BlockSpec(g_b.shape, lambda b, m: (0, 0)),
            pl.BlockSpec(bias.shape, lambda b, m: (0, 0)),
        ],
        out_specs=pl.BlockSpec((5, None, n_pairs, tm, LANES), lambda b, m: (0, b, 0, m, 0)),
        out_shape=jax.ShapeDtypeStruct((5, n_batch, n_pairs, ltot, LANES), f32),
        compiler_params=_cparams(("parallel", "parallel")),
        name="rwkv_lora_b",
    )(h, w_b, a_b, g_b, bias)


def _seg_sum(x, lo):
    s_lo = jnp.sum(jnp.where(lo, x, 0.0), axis=1, keepdims=True)
    s_all = jnp.sum(x, axis=1, keepdims=True)
    return jnp.where(lo, s_lo, s_all - s_lo)


def _wkv_kernel(*refs, n_pairs, reverse, fused):
    if fused:
        (r_ref, k_ref, v_ref, w_ref, a_ref, par_ref, y0_ref, bon0_ref, gate_ref, ln_ref, og_ref, s_ref,
         lhs_s, rhs1_s, rhs2_s, vbd_s, dec_s, mab_s, mak_s, arbk_s, tinv_s, tmp_s, w_s, g1r_s, uv_s, bon_ref) = refs
    else:
        (r_ref, k_ref, v_ref, w_ref, a_ref, par_ref, y_ref, bon_ref, s_ref,
         lhs_s, rhs1_s, rhs2_s, vbd_s, dec_s, mab_s, mak_s, arbk_s, tinv_s, tmp_s, w_s, g1r_s, uv_s) = refs
    _wkv_body(r_ref, k_ref, v_ref, w_ref, a_ref, par_ref, bon_ref, s_ref,
              lhs_s, rhs1_s, rhs2_s, vbd_s, dec_s, mab_s, mak_s, arbk_s, tinv_s, tmp_s, w_s, g1r_s, uv_s,
              (y0_ref, bon0_ref, gate_ref, ln_ref, og_ref) if fused else (y_ref,),
              n_pairs=n_pairs, reverse=reverse)


def _wkv_body(r_ref, k_ref, v_ref, w_ref, a_ref, par_ref, bon_ref, s_ref,
              lhs_s, rhs1_s, rhs2_s, vbd_s, dec_s, mab_s, mak_s, arbk_s, tinv_s, tmp_s, w_s, g1r_s, uv_s,
              out_refs, *, n_pairs, reverse):
    c = pl.program_id(1)
    L = WKV_CHUNK
    P = 2 * L

    @pl.when(c == 0)
    def _():
        s_ref[...] = jnp.zeros_like(s_ref)

    lane = lax.broadcasted_iota(jnp.int32, (L, LANES), 1)
    lo = lane < RWKV_HEAD
    ri = lax.broadcasted_iota(jnp.int32, (P, P), 0)
    ci = lax.broadcasted_iota(jnp.int32, (P, P), 1)
    same = (ri // L) == (ci // L)
    ii, jj = ri % L, ci % L
    if reverse:
        strict, incl = same & (jj > ii), same & (jj >= ii)
    else:
        strict, incl = same & (jj < ii), same & (jj <= ii)
    eye = (ri == ci).astype(f32)
    ti = lax.broadcasted_iota(jnp.int32, (L, L), 0)
    tj = lax.broadcasted_iota(jnp.int32, (L, L), 1)
    tri = ((tj >= ti) if reverse else (tj <= ti)).astype(bf16)
    diag_p = ((lax.broadcasted_iota(jnp.int32, (P, LANES), 0) // L)
              == (lax.broadcasted_iota(jnp.int32, (P, LANES), 1) // RWKV_HEAD))

    def blockdiag(x):
        return jnp.where(diag_p, jnp.concatenate([x, x], axis=0), 0.0)

    def stack(x):
        return jnp.concatenate([x, x], axis=0)

    def prepare(p):
        r, k, v = r_ref[p], k_ref[p], v_ref[p]
        k_k = par_ref[p, 0:1, :]
        k_a = par_ref[p, 1:2, :]
        r_k = par_ref[p, 2:3, :]
        lw = -math.exp(-0.5) * _sigmoid(w_ref[p])
        a = _sigmoid(a_ref[p])
        kn = k * k_k
        kk = kn * lax.rsqrt(jnp.maximum(_seg_sum(kn * kn, lo), 1e-24))
        kd = k * (1.0 + (a - 1.0) * k_a)
        bon_ref[p] = _seg_sum(r * kd * r_k, lo) * v

        lw_hi = lw.astype(bf16)
        lw_lo = (lw - lw_hi.astype(f32)).astype(bf16)
        cum = (jnp.dot(tri, lw_hi, preferred_element_type=f32)
               + jnp.dot(tri, lw_lo, preferred_element_type=f32))
        tot = cum[0:1, :] if reverse else cum[L - 1:L, :]
        e_neg = jnp.exp(-cum)
        e_end = jnp.exp(tot - cum)
        b_in = kk * a
        at = -kk * jnp.exp(cum - lw)
        rt = r * jnp.exp(cum)
        lhs_s[p] = jnp.concatenate([blockdiag(at), blockdiag(rt)], axis=0).astype(bf16)
        rhs1_s[p] = jnp.concatenate([stack(b_in * e_neg), stack(kd * e_neg)], axis=0).astype(bf16)
        rhs2_s[p] = jnp.concatenate([stack(b_in * e_end), stack(kd * e_end)], axis=0).astype(bf16)
        vbd_s[p] = blockdiag(v).astype(bf16)
        dec_s[p] = jnp.broadcast_to(jnp.exp(tot), (SUBLANES, LANES))

    def scores(p):
        sc = lax.dot_general(lhs_s[p], rhs1_s[p], NT_DIMS, preferred_element_type=f32)
        m_ab = jnp.where(strict, sc[0:P, 0:P], 0.0)
        mab_s[p] = m_ab.astype(bf16)
        mak_s[p] = jnp.where(strict, sc[0:P, P:2 * P], 0.0).astype(bf16)
        arbk_s[p, :, 0:P] = jnp.where(incl, sc[P:2 * P, 0:P], 0.0).astype(bf16)
        arbk_s[p, :, P:2 * P] = jnp.where(incl, sc[P:2 * P, P:2 * P], 0.0).astype(bf16)
        tinv_s[p] = eye + jnp.where((ri // 2) == (ci // 2), m_ab, 0.0)

    def invert(pairs):
        s = 4
        while s <= L:
            level = ((ri // s) == (ci // s)) & ((ri // (s // 2)) != (ci // (s // 2)))
            for p in pairs:
                e = jnp.where(level, mab_s[p], jnp.zeros((P, P), bf16))
                tmp_s[p] = _dot(tinv_s[p], e).astype(bf16)
            for p in pairs:
                t_inv = tinv_s[p]
                tinv_s[p] = t_inv + _dot(tmp_s[p], t_inv)
            s *= 2

    def read_state(p):
        g1 = _dot(lhs_s[p], s_ref[p], NT_DIMS)
        z = jnp.dot(mak_s[p], vbd_s[p], preferred_element_type=f32)
        w_s[p] = (g1[0:P] + z).astype(bf16)
        g1r_s[p] = g1[P:2 * P]

    def solve(p):
        uv_s[p, 0:P, :] = _dot(tinv_s[p], w_s[p]).astype(bf16)
        uv_s[p, P:2 * P, :] = vbd_s[p]

    def emit(p):
        y = g1r_s[p] + jnp.dot(arbk_s[p], uv_s[p], preferred_element_type=f32)
        y = y[0:L] + y[L:P]
        if len(out_refs) == 1:
            out_refs[0][p] = y
        else:
            y0_ref, bon0_ref, gate_ref, ln_ref, og_ref = out_refs
            y = y + y0_ref[p]
            mean = _seg_sum(y, lo) * (1.0 / RWKV_HEAD)
            yc = y - mean
            var = _seg_sum(yc * yc, lo) * (1.0 / RWKV_HEAD)
            sl = slice(p * LANES, (p + 1) * LANES)
            o = yc * lax.rsqrt(var + LN_X_EPS) * ln_ref[0:1, sl] + ln_ref[1:2, sl] + bon0_ref[p] + bon_ref[p]
            og_ref[:, sl] = (o * gate_ref[p]).astype(bf16)
        upd = lax.dot_general(uv_s[p], rhs2_s[p], TN_DIMS, preferred_element_type=f32)
        s_ref[p] = s_ref[p] * dec_s[p, 0:1, :] + jnp.where(diag_p, upd, 0.0)

    def finish(pairs):
        for stage in (read_state, solve, emit):
            for p in pairs:
                stage(p)

    pairs = range(n_pairs)
    for p in pairs:
        prepare(p)
    for p in pairs:
        scores(p)
    invert(pairs)
    finish(pairs)


def _wkv_call(rkv, pre, params, ctx_len, direction, first=None, ln_x=None):
    _, n_batch, n_pairs, ltot, _ = rkv.shape
    fused = first is not None
    L = WKV_CHUNK
    P = 2 * L
    n_chunks = ltot // L
    ctx_chunks = ctx_len // L
    reverse = direction == 1

    def chunk_of(c):
        if not reverse:
            return c
        return jnp.where(c < ctx_chunks, ctx_chunks - 1 - c, n_chunks - 1 - (c - ctx_chunks))

    def spec(lead):
        return pl.BlockSpec((None, None, n_pairs, L, LANES), lambda b, c: (lead, b, 0, chunk_of(c), 0))

    pair_spec = pl.BlockSpec((None, n_pairs, L, LANES), lambda b, c: (b, 0, chunk_of(c), 0))
    pair_sds = jax.ShapeDtypeStruct((n_batch, n_pairs, ltot, LANES), f32)
    in_specs = [spec(0), spec(1), spec(2),
                pl.BlockSpec((None, None, n_pairs, L, LANES), lambda b, c: (direction, b, 0, chunk_of(c), 0)),
                pl.BlockSpec((None, None, n_pairs, L, LANES), lambda b, c: (2 + direction, b, 0, chunk_of(c), 0)),
                pl.BlockSpec((None, n_pairs, SUBLANES, LANES), lambda b, c: (direction, 0, 0, 0))]
    operands = [rkv, rkv, rkv, pre, pre, params]
    extra_scratch = []
    if fused:
        d = n_pairs * LANES
        in_specs += [pair_spec, pair_spec,
                     pl.BlockSpec((None, None, n_pairs, L, LANES), lambda b, c: (4, b, 0, chunk_of(c), 0)),
                     pl.BlockSpec((SUBLANES, d), lambda b, c: (0, 0))]
        operands += [first[0], first[1], pre, jnp.pad(ln_x, ((0, SUBLANES - ln_x.shape[0]), (0, 0)))]
        out_specs = pl.BlockSpec((None, L, d), lambda b, c: (b, chunk_of(c), 0))
        out_shape = jax.ShapeDtypeStruct((n_batch, ltot, d), bf16)
        extra_scratch = [pltpu.VMEM((n_pairs, L, LANES), f32)]
    else:
        out_specs = [pair_spec, pair_spec]
        out_shape = [pair_sds, pair_sds]
    return pl.pallas_call(
        functools.partial(_wkv_kernel, n_pairs=n_pairs, reverse=reverse, fused=fused),
        grid=(n_batch, n_chunks),
        in_specs=in_specs,
        out_specs=out_specs,
        out_shape=out_shape,
        scratch_shapes=[
            pltpu.VMEM((n_pairs, P, LANES), f32),
            pltpu.VMEM((n_pairs, 2 * P, LANES), bf16),
            pltpu.VMEM((n_pairs, 2 * P, LANES), bf16),
            pltpu.VMEM((n_pairs, 2 * P, LANES), bf16),
            pltpu.VMEM((n_pairs, P, LANES), bf16),
            pltpu.VMEM((n_pairs, SUBLANES, LANES), f32),
            pltpu.VMEM((n_pairs, P, P), bf16),
            pltpu.VMEM((n_pairs, P, P), bf16),
            pltpu.VMEM((n_pairs, P, 2 * P), bf16),
            pltpu.VMEM((n_pairs, P, P), f32),
            pltpu.VMEM((n_pairs, P, P), bf16),
            pltpu.VMEM((n_pairs, P, LANES), bf16),
            pltpu.VMEM((n_pairs, P, LANES), f32),
            pltpu.VMEM((n_pairs, 2 * P, LANES), bf16),
        ] + extra_scratch,
        compiler_params=_cparams(("parallel", "arbitrary")),
        name="rwkv_wkv_fwd" if not reverse else "rwkv_wkv_bwd",
    )(*operands)


def _rwkv_layer(z, mods, norm_g, ctx_len, mu, w_rkv, w_o, dir_vec, w_la, w_lb, a_la, a_lb, g_la, g_lb, r_k, ln_x):
    n_batch, ltot, d = z.shape
    n_pairs = d // LANES
    tm = 768 if ltot % 768 == 0 else 512
    mixes = _rwkv_mix_call(z, mods, norm_g, mu, ctx_len)
    rkv = _rkv_call(mixes, w_rkv.astype(bf16), tm)

    def pad_cols(w):
        return jnp.concatenate([jnp.pad(w[i], ((0, 0), (0, LANES - w.shape[2]))) for i in range(2)], axis=1)

    def pad_rows(w):
        return jnp.pad(w, ((0, 0), (0, LANES - w.shape[1]), (0, 0)))

    w_a = jnp.stack([pad_cols(w_la), pad_cols(a_la), g_la]).astype(bf16)
    h = _lora_a_call(mixes, w_a, tm)
    bias = jnp.pad(jnp.stack([dir_vec[0, 0], dir_vec[1, 0], dir_vec[0, 1], dir_vec[1, 1]]), ((0, 4), (0, 0)))
    pre = _lora_b_call(h, pad_rows(w_lb).astype(bf16), pad_rows(a_lb).astype(bf16), g_lb.astype(bf16), bias, 256)
    rk_row = r_k.reshape(d)
    params = jnp.stack([jnp.stack([dir_vec[di, 2], dir_vec[di, 3], rk_row]) for di in range(2)])
    params = jnp.pad(params, ((0, 0), (0, SUBLANES - 3), (0, 0)))
    params = params.reshape(2, SUBLANES, n_pairs, LANES).transpose(0, 2, 1, 3)
    first = _wkv_call(rkv, pre, params, ctx_len, 0)
    og = _wkv_call(rkv, pre, params, ctx_len, 1, first=first, ln_x=ln_x)
    return _mm_resid_call(og, w_o.astype(bf16), z, mods, ctx_len, tm, chunk=2)


def _rope_tables(ctx_len, seq_len):
    rows = seq_len // GRID_W
    row = np.repeat(np.arange(rows, dtype=np.float32), GRID_W)
    col = np.tile(np.arange(GRID_W, dtype=np.float32), rows)
    n_freq = DIFF_HEAD // 4
    inv = (np.float32(ROPE_BASE) ** (-np.arange(n_freq, dtype=np.float32) / np.float32(n_freq))).astype(np.float32)
    ang = np.concatenate([row[:, None] * inv, col[:, None] * inv], axis=-1).astype(np.float32)
    cos = np.repeat(np.cos(ang), 2, axis=1)
    sin = np.repeat(np.sin(ang), 2, axis=1) * np.tile(np.array([-1.0, 1.0], np.float32), DIFF_HEAD // 2)
    cos = np.concatenate([np.ones((ctx_len, DIFF_HEAD), np.float32), cos], axis=0)
    sin = np.concatenate([np.zeros((ctx_len, DIFF_HEAD), np.float32), sin], axis=0)
    return jnp.asarray(cos, f32), jnp.asarray(sin, f32)


def _slab_products(x_ref, w_ref, tm, consume):
    def product(c):
        return jnp.dot(x_ref[c * ROW_CHUNK:(c + 1) * ROW_CHUNK, :], w_ref[...], preferred_element_type=f32)

    n_slabs = tm // ROW_CHUNK
    nxt = product(0)
    for c in range(n_slabs):
        y = nxt
        if c + 1 < n_slabs:
            nxt = product(c + 1)
        consume(c, y)


def _qk_kernel(mods_ref, g_ref, qkg_ref, cos_ref, sin_ref, z_ref, w_ref, o_ref, h_ref,
               *, tm, tn, d, ctx_len, n_batch):
    b = pl.program_id(0)
    m = pl.program_id(1)
    n = pl.program_id(2)

    @pl.when(n == 0)
    def _():
        _norm_mod_tile(z_ref, h_ref, mods_ref, g_ref, 0, b, m * tm, tm, d, ctx_len, n_batch)

    is_q = n < d // tn
    gain = qkg_ref[pl.ds(jnp.where(is_q, 0, 1), 1), :]
    scale = jnp.where(is_q, DIFF_HEAD ** -0.5 * LOG2E, 1.0)
    even = (lax.broadcasted_iota(jnp.int32, (ROW_CHUNK, LANES), 1) % 2) == 0

    def epilogue(c, y):
        rows = slice(c * ROW_CHUNK, (c + 1) * ROW_CHUNK)
        cos = cos_ref[rows, :]
        sin = sin_ref[rows, :]
        for gi in range(tn // LANES):
            x = y[:, gi * LANES:(gi + 1) * LANES]
            ms = jnp.mean(x * x, axis=-1, keepdims=True)
            x = x * (lax.rsqrt(ms + EPS) * scale) * gain
            partner = jnp.where(even, pltpu.roll(x, LANES - 1, 1), pltpu.roll(x, 1, 1))
            o_ref[rows, gi * LANES:(gi + 1) * LANES] = (x * cos + partner * sin).astype(bf16)

    _slab_products(h_ref, w_ref, tm, epilogue)


def _qk_call(z, mods, norm_g, w_qk, qk_g, cos, sin, ctx_len, tm):
    n_batch, ltot, d = z.shape
    tn = 512
    kern = functools.partial(_qk_kernel, tm=tm, tn=tn, d=d, ctx_len=ctx_len, n_batch=n_batch)
    return pl.pallas_call(
        kern,
        grid=(n_batch, ltot // tm, 2 * d // tn),
        in_specs=[
            pl.BlockSpec(mods.shape, lambda b, m, n: (0, 0)),
            pl.BlockSpec((1, d), lambda b, m, n: (0, 0)),
            pl.BlockSpec((SUBLANES, DIFF_HEAD), lambda b, m, n: (0, 0)),
            pl.BlockSpec((tm, DIFF_HEAD), lambda b, m, n: (m, 0)),
            pl.BlockSpec((tm, DIFF_HEAD), lambda b, m, n: (m, 0)),
            pl.BlockSpec((None, tm, d), lambda b, m, n: (b, m, 0)),
            pl.BlockSpec((d, tn), lambda b, m, n: (0, n)),
        ],
        out_specs=[pl.BlockSpec((None, tm, tn), lambda b, m, n: (b, m, n)),
                   pl.BlockSpec((None, tm, d), lambda b, m, n: (b, m, 0))],
        out_shape=[jax.ShapeDtypeStruct((n_batch, ltot, 2 * d), bf16),
                   jax.ShapeDtypeStruct((n_batch, ltot, d), bf16)],
        compiler_params=_cparams(("parallel", "parallel", "arbitrary")),
        name="diff_qk_proj",
    )(mods, norm_g.reshape(1, d), jnp.pad(qk_g, ((0, SUBLANES - qk_g.shape[0]), (0, 0))), cos, sin, z, w_qk)


def _vt_kernel(h_ref, w_ref, o_ref, *, tm):
    def store(c, y):
        o_ref[:, c * ROW_CHUNK:(c + 1) * ROW_CHUNK] = y.astype(bf16).T

    _slab_products(h_ref, w_ref, tm, store)


def _vt_call(h, w_qkv, tm):
    n_batch, ltot, d = h.shape
    tn = 512
    v_off = 2 * d // tn
    return pl.pallas_call(
        functools.partial(_vt_kernel, tm=tm),
        grid=(n_batch, ltot // tm, d // tn),
        in_specs=[pl.BlockSpec((None, tm, d), lambda b, m, n: (b, m, 0)),
                  pl.BlockSpec((d, tn), lambda b, m, n: (0, v_off + n))],
        out_specs=pl.BlockSpec((None, None, tn, tm), lambda b, m, n: (b, m, n, 0)),
        out_shape=jax.ShapeDtypeStruct((n_batch, ltot // tm, d, tm), bf16),
        compiler_params=_cparams(("parallel", "parallel", "arbitrary")),
        name="diff_v_proj",
    )(h, w_qkv)


def _attn_kernel(lam_ref, sg_ref, q_ref, k_ref, vt_ref, o_ref, sa_s, sb_s, *, lambda_init, ctx_len):
    tq = o_ref.shape[0]
    n_k, _, tk = vt_ref.shape
    hd = DIFF_HEAD
    q_row0 = pl.multiple_of(ctx_len + pl.program_id(2) * tq, 256)
    qt = q_ref[pl.ds(q_row0, tq), :].astype(f32).T
    top = lax.broadcasted_iota(jnp.int32, qt.shape, 0) < hd
    w = jnp.concatenate([jnp.where(top, qt, 0.0), jnp.where(top, 0.0, qt)], axis=1).astype(bf16)
    groups = [(j0, min(ATTN_GROUP, n_k - j0)) for j0 in range(0, n_k, ATTN_GROUP)]
    bufs = (sa_s, sb_s)

    def scores(g):
        j0, nb = groups[g]
        bufs[g % 2][0:nb * tk, :] = jnp.dot(k_ref[j0 * tk:(j0 + nb) * tk, :], w, preferred_element_type=f32)

    def absorb(g, m_old, l_old, acc):
        j0, nb = groups[g]
        s = bufs[g % 2][0:nb * tk, :]
        m_new = jnp.maximum(m_old, jnp.max(s, axis=0, keepdims=True))
        alpha = jnp.exp2(m_old - m_new)
        p = jnp.exp2(s - m_new)
        l_new = alpha * l_old + jnp.sum(p, axis=0, keepdims=True)
        vt = jnp.concatenate([vt_ref[j0 + i] for i in range(nb)], axis=1) if nb > 1 else vt_ref[j0]
        return m_new, l_new, alpha * acc + jnp.dot(vt, p.astype(bf16), preferred_element_type=f32)

    m = jnp.full((1, 2 * tq), -jnp.inf, f32)
    l = jnp.zeros((1, 2 * tq), f32)
    acc = jnp.zeros((2 * hd, 2 * tq), f32)
    scores(0)
    for g in range(len(groups)):
        if g + 1 < len(groups):
            scores(g + 1)
        m, l, acc = absorb(g, m, l, acc)
    lv = lam_ref[...]
    lam = (jnp.exp(jnp.sum(lv[0:1] * lv[1:2], axis=1, keepdims=True))
           - jnp.exp(jnp.sum(lv[2:3] * lv[3:4], axis=1, keepdims=True)) + lambda_init)
    o = acc[:, 0:tq] / l[:, 0:tq] - lam * (acc[:, tq:] / l[:, tq:])
    ms = jnp.mean(o * o, axis=0, keepdims=True)
    o = o * lax.rsqrt(ms + EPS) * sg_ref[...]
    o_ref[...] = o.T.astype(bf16)


def _attn_call(qk, vt, lam_vec, subln_g, ctx_len, lambda_init):
    n_batch, ltot, d2 = qk.shape
    d = d2 // 2
    _, n_kb, _, tk = vt.shape
    hw = 2 * DIFF_HEAD
    n_heads = d // hw
    seq_len = ltot - ctx_len
    tq = 512
    assert ctx_len % 256 == 0 and seq_len % tq == 0
    gain = jnp.broadcast_to((subln_g * (1.0 - lambda_init)).reshape(hw, 1), (hw, tq))
    kern = functools.partial(_attn_kernel, lambda_init=lambda_init, ctx_len=ctx_len)
    return pl.pallas_call(
        kern,
        grid=(n_batch, n_heads, seq_len // tq),
        in_specs=[
            pl.BlockSpec((SUBLANES, DIFF_HEAD), lambda b, h, t: (0, 0)),
            pl.BlockSpec((hw, tq), lambda b, h, t: (0, 0)),
            pl.BlockSpec((None, ltot, hw), lambda b, h, t: (b, 0, h)),
            pl.BlockSpec((None, ltot, hw), lambda b, h, t: (b, 0, n_heads + h)),
            pl.BlockSpec((None, n_kb, hw, tk), lambda b, h, t: (b, 0, h, 0)),
        ],
        out_specs=pl.BlockSpec((None, tq, hw), lambda b, h, t: (b, t, h)),
        out_shape=jax.ShapeDtypeStruct((n_batch, seq_len, d), bf16),
        scratch_shapes=[pltpu.VMEM((min(ATTN_GROUP, n_kb) * tk, 2 * tq), f32),
                        pltpu.VMEM((min(ATTN_GROUP, n_kb) * tk, 2 * tq), f32)],
        compiler_params=_cparams(("parallel", "parallel", "parallel")),
        name="diff_attention",
    )(jnp.pad(lam_vec, ((0, SUBLANES - lam_vec.shape[0]), (0, 0))), gain, qk, qk, vt)


def _diff_layer(z, mods, norm_g, ctx_len, w_qkv, w_o, qk_g, lam_vec, subln_g, lambda_init):
    n_batch, ltot, d = z.shape
    seq_len = ltot - ctx_len
    tm = 768 if ltot % 768 == 0 else 512
    cos, sin = _rope_tables(ctx_len, seq_len)
    w_bf = w_qkv.astype(bf16)
    qk, h = _qk_call(z, mods, norm_g, w_bf, qk_g, cos, sin, ctx_len, tm)
    vt = _vt_call(h, w_bf, tm)
    o = _attn_call(qk, vt, lam_vec, subln_g, ctx_len, lambda_init)
    return _mm_resid_call(o, w_o.astype(bf16), z, mods, 0, 256, chunk=2, row_off=ctx_len)


def kernel(x, c, ctx, c_ctx, ada_w, ada_b, norm_g, ffn_w_in, ffn_w_out, pool_w, pool_scale, rwkv_mu, rwkv_w_rkv, rwkv_w_o, rwkv_dir_vec, rwkv_w_lora_a, rwkv_w_lora_b, rwkv_a_lora_a, rwkv_a_lora_b, rwkv_g_lora_a, rwkv_g_lora_b, rwkv_r_k, rwkv_ln_x, diff_w_qkv, diff_w_o, diff_qk_g, diff_lambda, diff_subln_g):
    n_batch, seq_len, d = x.shape
    depth = ada_w.shape[0]
    ctx_len = ctx.shape[1]
    assert n_batch + 1 <= SUBLANES
    cs = jnp.concatenate([c, c_ctx[None, :], jnp.zeros((SUBLANES - n_batch - 1, d), f32)], axis=0)
    mods_all = _mods_call(cs, ada_w, ada_b)

    last_reader = max((i for i in range(depth) if i % N_MIXERS != 0), default=-1)
    cur_ctx = ctx_len if last_reader >= 0 else 0
    join_in_pool = cur_ctx == 256 and depth > 0
    z = x if (cur_ctx == 0 or join_in_pool) else jnp.concatenate([ctx, x], axis=1)
    w_in_all = ffn_w_in.astype(bf16)
    w_out_all = ffn_w_out.astype(bf16)
    for i in range(depth):
        kind, j = i % N_MIXERS, i // N_MIXERS
        mods = mods_all[i]
        if kind == 0:
            z = _pool_call(z, mods, norm_g[i, 0], pool_w[j], pool_scale[j], cur_ctx,
                           ctx=ctx if (i == 0 and join_in_pool) else None)
        elif kind == 1:
            z = _rwkv_layer(z, mods, norm_g[i, 0], cur_ctx, rwkv_mu[j], rwkv_w_rkv[j], rwkv_w_o[j], rwkv_dir_vec[j],
                            rwkv_w_lora_a[j], rwkv_w_lora_b[j], rwkv_a_lora_a[j], rwkv_a_lora_b[j],
                            rwkv_g_lora_a[j], rwkv_g_lora_b[j], rwkv_r_k[j], rwkv_ln_x[j])
        else:
            lambda_init = 0.8 - 0.6 * math.exp(-0.3 * i)
            z = _diff_layer(z, mods, norm_g[i, 0], cur_ctx, diff_w_qkv[j], diff_w_o[j], diff_qk_g[j],
                            diff_lambda[j], diff_subln_g[j], lambda_init)
            cur_ctx = 0
        if cur_ctx and i >= last_reader:
            z = z[:, cur_ctx:]
            cur_ctx = 0
        ltot = z.shape[1]
        tm = 768 if ltot % 768 == 0 else 512
        z = _ffn_call(z, mods, norm_g[i, 1], w_in_all, w_out_all, i, cur_ctx, tm)
    return z[:, cur_ctx:] if cur_ctx else z
```

```python
import functools
import math

import jax
import jax.numpy as jnp
import numpy as np
from jax import lax
from jax.experimental import pallas as pl
from jax.experimental.pallas import tpu as pltpu

f32 = jnp.float32
bf16 = jnp.bfloat16

N_MIXERS = 3
EPS = 1e-6
POOL_WINDOWS = (2, 4, 8, 16)
POOL_HALO = 8
RWKV_HEAD = 64
LN_X_EPS = 64e-5
DIFF_HEAD = 128
ROPE_BASE = 10000.0
LOG2E = 1.4426950408889634
GRID_W = 64
LANES = 128
SUBLANES = 8
WKV_CHUNK = 64
ROW_CHUNK = 256
NORM_ROWS = 16
ATTN_GROUP = 2
VMEM_LIMIT = 60 * 1024 * 1024

NT_DIMS = (((1,), (1,)), ((), ()))
TN_DIMS = (((0,), (0,)), ((), ()))


def _cparams(sem):
    return pltpu.CompilerParams(dimension_semantics=sem, vmem_limit_bytes=VMEM_LIMIT)


def _dot(a, b, dims=None):
    a = a.astype(bf16)
    b = b.astype(bf16)
    if dims is None:
        return jnp.dot(a, b, preferred_element_type=f32)
    return lax.dot_general(a, b, dims, preferred_element_type=f32)


def _sigmoid(x):
    return 1.0 / (1.0 + jnp.exp(-x))


def _row_ids(shape, row0):
    return lax.broadcasted_iota(jnp.int32, shape, 0) + row0


def _mod_rows(mods_ref, chunk, b, n_batch, is_ctx, d):
    lat = mods_ref[pl.ds(b, 1), chunk * d:(chunk + 1) * d]
    if is_ctx is None:
        return lat
    ctx = mods_ref[n_batch:n_batch + 1, chunk * d:(chunk + 1) * d]
    return jnp.where(is_ctx, ctx, lat)


def _norm_mod(x, g, shift, scale):
    ms = jnp.mean(x * x, axis=-1, keepdims=True)
    h = x * lax.rsqrt(ms + EPS) * g
    return h * (1.0 + scale) + shift


def _is_ctx(shape, row0, ctx_len):
    if ctx_len == 0:
        return None
    return _row_ids(shape, row0) < ctx_len


def _mods_kernel(s_ref, w_ref, b_ref, o_ref):
    s = s_ref[...]
    s = s * _sigmoid(s)
    o_ref[...] = _dot(s, w_ref[...]) + b_ref[...]


def _mods_call(cs, ada_w, ada_b):
    depth, d, n6 = ada_w.shape
    tn = 1024
    return pl.pallas_call(
        _mods_kernel,
        grid=(depth, n6 // tn),
        in_specs=[
            pl.BlockSpec((SUBLANES, d), lambda l, n: (0, 0)),
            pl.BlockSpec((None, d, tn), lambda l, n: (l, 0, n)),
            pl.BlockSpec((None, 1, tn), lambda l, n: (l, 0, n)),
        ],
        out_specs=pl.BlockSpec((None, SUBLANES, tn), lambda l, n: (l, 0, n)),
        out_shape=jax.ShapeDtypeStruct((depth, SUBLANES, n6), f32),
        compiler_params=_cparams(("parallel", "parallel")),
        name="adaln_mods",
    )(cs, ada_w, ada_b.reshape(depth, 1, n6))


def _halo_specs(tq, d, n_tiles):
    per = tq // POOL_HALO
    last = n_tiles * per - 1
    cur = pl.BlockSpec((None, tq, d), lambda b, t: (b, t, 0))
    prev = pl.BlockSpec((None, POOL_HALO, d), lambda b, t: (b, jnp.maximum(t * per - 1, 0), 0))
    nxt = pl.BlockSpec((None, POOL_HALO, d), lambda b, t: (b, jnp.minimum((t + 1) * per, last), 0))
    return cur, prev, nxt


def _stream_edges(t, tq, ctx_len, n_tiles):
    ctx_tiles = ctx_len // tq
    first = t == 0
    last = t == n_tiles - 1
    if ctx_tiles:
        first = first | (t == ctx_tiles)
        last = last | (t == ctx_tiles - 1)
    return jnp.logical_not(first), jnp.logical_not(last)


def _pool_kernel(*refs, tq, d, ctx_len, n_tiles, n_batch, seq_len, split):
    if split:
        mods_ref, g_ref, w_ref, ls_ref, zc_ref, zp_ref, zn_ref, ctx_ref, o_ref, ext_ref, wa_ref, wb_ref = refs
    else:
        mods_ref, g_ref, w_ref, ls_ref, zc_ref, zp_ref, zn_ref, o_ref, ext_ref, wa_ref, wb_ref = refs
        ctx_ref = None
    b = pl.program_id(0)
    t = pl.program_id(1)
    row0 = t * tq
    g = g_ref[...]

    def hmod(x, r0):
        ic = _is_ctx((x.shape[0], 1), r0, ctx_len)
        sh = _mod_rows(mods_ref, 0, b, n_batch, ic, d)
        sc = _mod_rows(mods_ref, 1, b, n_batch, ic, d)
        return _norm_mod(x, g, sh, sc)

    has_prev, has_next = _stream_edges(t, tq, ctx_len, n_tiles)
    if ctx_ref is None:
        src_ref = zc_ref
    else:
        o_ref[...] = jnp.where(t == 0, ctx_ref[...], zc_ref[...])
        src_ref = o_ref
    ext_ref[0:POOL_HALO, :] = jnp.where(has_prev, hmod(zp_ref[...], row0 - POOL_HALO), 0.0)
    _norm_mod_tile(src_ref, ext_ref, mods_ref, g_ref, 0, b, row0, tq, d, ctx_len, n_batch, out_off=POOL_HALO)
    ext_ref[POOL_HALO + tq:, :] = jnp.where(has_next, hmod(zn_ref[...], row0 + tq), 0.0)

    rows = _row_ids((tq, 1), row0)
    if ctx_len:
        in_ctx = rows < ctx_len
        pos = jnp.where(in_ctx, rows, rows - ctx_len)
        slen = jnp.where(in_ctx, ctx_len, seq_len)
    else:
        pos, slen = rows, seq_len

    cg = d // len(POOL_WINDOWS)
    ys = []
    for gi, win in enumerate(POOL_WINDOWS):
        lo_off, hi_off = win // 2, win - win // 2
        c0 = gi * cg
        cols = slice(c0, c0 + cg)
        levels = win.bit_length() - 1
        assert win == 1 << levels and win // 2 <= POOL_HALO
        src, span, reach = ext_ref, 1, tq + 2 * POOL_HALO
        for k in range(levels - 1):
            dst = wa_ref if k % 2 == 0 else wb_ref
            reach -= span
            if k == 0:
                dst[0:reach, :] = src[0:reach, cols] + src[span:span + reach, cols]
            else:
                dst[0:reach, :] = src[0:reach, :] + src[span:span + reach, :]
            src, span = dst, 2 * span
        start = POOL_HALO - lo_off
        if levels == 1:
            acc = ext_ref[start:start + tq, cols] + ext_ref[start + 1:start + 1 + tq, cols]
        else:
            acc = src[start:start + tq, :] + src[start + span:start + span + tq, :]
        cnt = jnp.minimum(pos + hi_off, slen) - jnp.maximum(pos - lo_off, 0)
        p = acc * (1.0 / cnt.astype(f32)) - ext_ref[POOL_HALO:POOL_HALO + tq, c0:c0 + cg]
        ys.append(_dot(p, w_ref[gi]))
    y = jnp.concatenate(ys, axis=1) * ls_ref[...]
    ic = _is_ctx((tq, 1), row0, ctx_len)
    gate = _mod_rows(mods_ref, 2, b, n_batch, ic, d)
    o_ref[...] = src_ref[...] + gate * y


def _pool_call(z, mods, norm_g, pool_w, pool_scale, ctx_len, ctx=None):
    n_batch, rows_z, d = z.shape
    tq = 256
    split = ctx is not None
    ltot = rows_z + (ctx_len if split else 0)
    n_tiles = ltot // tq
    ng, cg, _ = pool_w.shape
    if split:
        assert ctx_len == tq and ctx.shape[1] == tq
        per = tq // POOL_HALO
        last = (n_tiles - 1) * per - 1
        cur = pl.BlockSpec((None, tq, d), lambda b, t: (b, jnp.maximum(t - 1, 0), 0))
        prev = pl.BlockSpec((None, POOL_HALO, d), lambda b, t: (b, jnp.maximum((t - 1) * per - 1, 0), 0))
        nxt = pl.BlockSpec((None, POOL_HALO, d), lambda b, t: (b, jnp.minimum(t * per, last), 0))
        extra_specs = [pl.BlockSpec((None, tq, d), lambda b, t: (b, 0, 0))]
        extra_args = [ctx]
    else:
        cur, prev, nxt = _halo_specs(tq, d, n_tiles)
        extra_specs, extra_args = [], []
    kern = functools.partial(_pool_kernel, tq=tq, d=d, ctx_len=ctx_len, n_tiles=n_tiles,
                             n_batch=n_batch, seq_len=ltot - ctx_len, split=split)
    return pl.pallas_call(
        kern,
        grid=(n_batch, n_tiles),
        in_specs=[
            pl.BlockSpec(mods.shape, lambda b, t: (0, 0)),
            pl.BlockSpec((1, d), lambda b, t: (0, 0)),
            pl.BlockSpec((ng, cg, cg), lambda b, t: (0, 0, 0)),
            pl.BlockSpec((1, d), lambda b, t: (0, 0)),
            cur, prev, nxt,
        ] + extra_specs,
        out_specs=pl.BlockSpec((None, tq, d), lambda b, t: (b, t, 0)),
        out_shape=jax.ShapeDtypeStruct((n_batch, ltot, d), f32),
        scratch_shapes=[pltpu.VMEM((tq + 2 * POOL_HALO, d), f32),
                        pltpu.VMEM((tq + 2 * POOL_HALO, cg), f32),
                        pltpu.VMEM((tq + 2 * POOL_HALO, cg), f32)],
        compiler_params=_cparams(("parallel", "parallel")),
        name="pool_mix",
    )(mods, norm_g.reshape(1, d), pool_w.astype(bf16), pool_scale.reshape(1, d), z, z, z, *extra_args)


def _for_rows(n_rows, body):
    def step(i, carry):
        body(pl.multiple_of(i * ROW_CHUNK, ROW_CHUNK))
        return carry
    lax.fori_loop(0, n_rows // ROW_CHUNK, step, 0)


def _norm_mod_tile(z_ref, h_ref, mods_ref, g_ref, chunk, b, row0, tm, d, ctx_len, n_batch, out_off=0):
    assert ctx_len % NORM_ROWS == 0 and tm % (4 * NORM_ROWS) == 0

    def step(i, carry):
        r0 = pl.multiple_of(i * NORM_ROWS, NORM_ROWS)
        ic = None if ctx_len == 0 else (row0 + r0) < ctx_len
        sh = _mod_rows(mods_ref, chunk, b, n_batch, ic, d)
        sc = _mod_rows(mods_ref, chunk + 1, b, n_batch, ic, d)
        h = _norm_mod(z_ref[pl.ds(r0, NORM_ROWS), :], g_ref[...], sh, sc)
        h_ref[pl.ds(out_off + r0, NORM_ROWS), :] = h.astype(h_ref.dtype)
        return carry
    lax.fori_loop(0, tm // NORM_ROWS, step, 0, unroll=4)


def _ffn_kernel(mods_ref, g_ref, z_ref, wg_ref, wu_ref, wo_ref, o_ref, h_ref,
                *, tm, d, ctx_len, n_batch, n_f):
    b = pl.program_id(0)
    m = pl.program_id(1)
    f = pl.program_id(2)

    @pl.when(f == 0)
    def _():
        _norm_mod_tile(z_ref, h_ref, mods_ref, g_ref, 3, b, m * tm, tm, d, ctx_len, n_batch)
        o_ref[...] = jnp.zeros((tm, d), f32)

    def gate_up(c):
        h = h_ref[c * ROW_CHUNK:(c + 1) * ROW_CHUNK, :]
        return (jnp.dot(h, wg_ref[...], preferred_element_type=f32),
                jnp.dot(h, wu_ref[...], preferred_element_type=f32))

    n_slabs = tm // ROW_CHUNK
    nxt = gate_up(0)
    for c in range(n_slabs):
        gate, up = nxt
        if c + 1 < n_slabs:
            nxt = gate_up(c + 1)
        act = (gate * _sigmoid(gate) * up).astype(bf16)
        o_ref[c * ROW_CHUNK:(c + 1) * ROW_CHUNK, :] += jnp.dot(act, wo_ref[...], preferred_element_type=f32)

    @pl.when(f == n_f - 1)
    def _():
        def epi(r0):
            rows = pl.ds(r0, ROW_CHUNK)
            ic = _is_ctx((ROW_CHUNK, 1), m * tm + r0, ctx_len)
            g2 = _mod_rows(mods_ref, 5, b, n_batch, ic, d)
            o_ref[rows, :] = z_ref[rows, :] + g2 * o_ref[rows, :]
        _for_rows(tm, epi)


def _ffn_call(z, mods, norm_g, w_in, w_out, layer, ctx_len, tm):
    n_batch, ltot, d = z.shape
    fh = w_out.shape[1]
    tf = 512
    n_f = fh // tf
    kern = functools.partial(_ffn_kernel, tm=tm, d=d, ctx_len=ctx_len, n_batch=n_batch, n_f=n_f)
    return pl.pallas_call(
        kern,
        grid=(n_batch, ltot // tm, n_f),
        in_specs=[
            pl.BlockSpec(mods.shape, lambda b, m, f: (0, 0)),
            pl.BlockSpec((1, d), lambda b, m, f: (0, 0)),
            pl.BlockSpec((None, tm, d), lambda b, m, f: (b, m, 0)),
            pl.BlockSpec((None, d, tf), lambda b, m, f: (layer, 0, f)),
            pl.BlockSpec((None, d, tf), lambda b, m, f: (layer, 0, f + n_f)),
            pl.BlockSpec((None, tf, d), lambda b, m, f: (layer, f, 0)),
        ],
        out_specs=pl.BlockSpec((None, tm, d), lambda b, m, f: (b, m, 0)),
        out_shape=jax.ShapeDtypeStruct((n_batch, ltot, d), f32),
        scratch_shapes=[pltpu.VMEM((tm, d), bf16)],
        compiler_params=_cparams(("parallel", "parallel", "arbitrary")),
        name="swiglu_ffn",
    )(mods, norm_g.reshape(1, d), z, w_in, w_in, w_out)


def _mm_resid_kernel(mods_ref, x_ref, w_ref, z_ref, o_ref, *, tm, ctx_len, n_batch, row_off):
    b = pl.program_id(0)
    m = pl.program_id(1)
    lat = mods_ref[pl.ds(b, 1), :]

    def residual(c, y):
        rows = slice(c * ROW_CHUNK, (c + 1) * ROW_CHUNK)
        ic = _is_ctx((ROW_CHUNK, 1), m * tm + row_off + c * ROW_CHUNK, ctx_len)
        gate = lat if ic is None else jnp.where(ic, mods_ref[n_batch:n_batch + 1, :], lat)
        o_ref[rows, :] = z_ref[rows, :] + gate * y

    _slab_products(x_ref, w_ref, tm, residual)


def _mm_resid_call(x, w, z, mods, ctx_len, tm, chunk, row_off=0):
    n_batch, rows, k = x.shape
    d = w.shape[1]
    tn = d
    assert row_off % tm == 0 and rows % tm == 0
    m_off = row_off // tm
    g_off = chunk * d // tn
    kern = functools.partial(_mm_resid_kernel, tm=tm, ctx_len=ctx_len, n_batch=n_batch, row_off=row_off)
    return pl.pallas_call(
        kern,
        grid=(n_batch, rows // tm, d // tn),
        in_specs=[
            pl.BlockSpec((SUBLANES, tn), lambda b, m, n: (0, g_off + n)),
            pl.BlockSpec((None, tm, k), lambda b, m, n: (b, m, 0)),
            pl.BlockSpec((k, tn), lambda b, m, n: (0, n)),
            pl.BlockSpec((None, tm, tn), lambda b, m, n: (b, m + m_off, n)),
        ],
        out_specs=pl.BlockSpec((None, tm, tn), lambda b, m, n: (b, m, n)),
        out_shape=jax.ShapeDtypeStruct((n_batch, rows, d), f32),
        compiler_params=_cparams(("parallel", "parallel", "arbitrary")),
        name="proj_residual",
    )(mods, x, w, z)


def _rwkv_mix_kernel(mods_ref, g_ref, mu_ref, zc_ref, zp_ref, zn_ref, o_ref, ext_ref,
                     *, tq, d, ctx_len, n_tiles, n_batch):
    b = pl.program_id(0)
    t = pl.program_id(1)
    row0 = t * tq
    g = g_ref[...]

    def hmod(x, r0):
        ic = _is_ctx((x.shape[0], 1), r0, ctx_len)
        sh = _mod_rows(mods_ref, 0, b, n_batch, ic, d)
        sc = _mod_rows(mods_ref, 1, b, n_batch, ic, d)
        return _norm_mod(x, g, sh, sc)

    has_prev, has_next = _stream_edges(t, tq, ctx_len, n_tiles)
    ext_ref[0:POOL_HALO, :] = jnp.where(has_prev, hmod(zp_ref[...], row0 - POOL_HALO), 0.0)
    _norm_mod_tile(zc_ref, ext_ref, mods_ref, g_ref, 0, b, row0, tq, d, ctx_len, n_batch, out_off=POOL_HALO)
    ext_ref[POOL_HALO + tq:, :] = jnp.where(has_next, hmod(zn_ref[...], row0 + tq), 0.0)

    for r0 in range(0, tq, NORM_ROWS):
        hc = ext_ref[POOL_HALO + r0:POOL_HALO + r0 + NORM_ROWS, :]
        up = ext_ref[POOL_HALO - 1 + r0:POOL_HALO - 1 + r0 + NORM_ROWS, :]
        dn = ext_ref[POOL_HALO + 1 + r0:POOL_HALO + 1 + r0 + NORM_ROWS, :]
        xx = 0.5 * (up + dn) - hc
        for mi in range(6):
            o_ref[mi, r0:r0 + NORM_ROWS, :] = (hc + xx * mu_ref[mi:mi + 1, :]).astype(bf16)


def _rwkv_mix_call(z, mods, norm_g, mu, ctx_len):
    n_batch, ltot, d = z.shape
    tq = 256
    n_tiles = ltot // tq
    cur, prev, nxt = _halo_specs(tq, d, n_tiles)
    kern = functools.partial(_rwkv_mix_kernel, tq=tq, d=d, ctx_len=ctx_len, n_tiles=n_tiles, n_batch=n_batch)
    return pl.pallas_call(
        kern,
        grid=(n_batch, n_tiles),
        in_specs=[
            pl.BlockSpec(mods.shape, lambda b, t: (0, 0)),
            pl.BlockSpec((1, d), lambda b, t: (0, 0)),
            pl.BlockSpec((SUBLANES, d), lambda b, t: (0, 0)),
            cur, prev, nxt,
        ],
        out_specs=pl.BlockSpec((6, None, tq, d), lambda b, t: (0, b, t, 0)),
        out_shape=jax.ShapeDtypeStruct((6, n_batch, ltot, d), bf16),
        scratch_shapes=[pltpu.VMEM((tq + 2 * POOL_HALO, d), f32)],
        compiler_params=_cparams(("parallel", "parallel")),
        name="rwkv_shift_mix",
    )(mods, norm_g.reshape(1, d), jnp.pad(mu, ((0, SUBLANES - mu.shape[0]), (0, 0))), z, z, z)


def _rkv_kernel(x_ref, w_ref, o_ref, *, tm, tn):
    def store(c, y):
        for gi in range(tn // LANES):
            o_ref[gi, c * ROW_CHUNK:(c + 1) * ROW_CHUNK, :] = y[:, gi * LANES:(gi + 1) * LANES]

    _slab_products(x_ref, w_ref, tm, store)


def _rkv_call(mixes, w_rkv, tm):
    _, n_batch, ltot, d = mixes.shape
    tn = d
    src = (0, 2, 3)

    def x_map(b, m, p, n):
        return (jnp.where(p == 0, src[0], jnp.where(p == 1, src[1], src[2])), b, m, 0)

    return pl.pallas_call(
        functools.partial(_rkv_kernel, tm=tm, tn=tn),
        grid=(n_batch, ltot // tm, 3, d // tn),
        in_specs=[
            pl.BlockSpec((None, None, tm, d), x_map),
            pl.BlockSpec((None, d, tn), lambda b, m, p, n: (p, 0, n)),
        ],
        out_specs=pl.BlockSpec((None, None, tn // LANES, tm, LANES), lambda b, m, p, n: (p, b, n, m, 0)),
        out_shape=jax.ShapeDtypeStruct((3, n_batch, d // LANES, ltot, LANES), f32),
        compiler_params=_cparams(("parallel", "parallel", "arbitrary", "arbitrary")),
        name="rwkv_rkv_proj",
    )(mixes, w_rkv)


def _lora_a_kernel(x_ref, w_ref, o_ref):
    o_ref[...] = jnp.dot(x_ref[...], w_ref[...], preferred_element_type=f32)


def _lora_a_call(mixes, w_a, tm):
    _, n_batch, ltot, d = mixes.shape
    nh = w_a.shape[2]
    src = (1, 4, 5)

    def x_map(b, m, p):
        return (jnp.where(p == 0, src[0], jnp.where(p == 1, src[1], src[2])), b, m, 0)

    return pl.pallas_call(
        _lora_a_kernel,
        grid=(n_batch, ltot // tm, 3),
        in_specs=[
            pl.BlockSpec((None, None, tm, d), x_map),
            pl.BlockSpec((None, d, nh), lambda b, m, p: (p, 0, 0)),
        ],
        out_specs=pl.BlockSpec((None, None, tm, nh), lambda b, m, p: (p, b, m, 0)),
        out_shape=jax.ShapeDtypeStruct((3, n_batch, ltot, nh), f32),
        compiler_params=_cparams(("parallel", "parallel", "arbitrary")),
        name="rwkv_lora_a",
    )(mixes, w_a)


def _lora_b_kernel(h_ref, wb_ref, ab_ref, gb_ref, bias_ref, o_ref, *, d):
    n_pairs = d // LANES
    hw = jnp.tanh(h_ref[0])
    ha = h_ref[1]
    hg = _sigmoid(h_ref[2])
    outs = []
    for di in range(2):
        outs.append(_dot(hw[:, di * LANES:(di + 1) * LANES], wb_ref[di]) + bias_ref[di:di + 1, :])
    for di in range(2):
        outs.append(_dot(ha[:, di * LANES:(di + 1) * LANES], ab_ref[di]) + bias_ref[2 + di:3 + di, :])
    outs.append(_dot(hg, gb_ref[...]))
    for oi, y in enumerate(outs):
        for p in range(n_pairs):
            o_ref[oi, p] = y[:, p * LANES:(p + 1) * LANES]


def _lora_b_call(h, w_b, a_b, g_b, bias, tm):
    _, n_batch, ltot, nh = h.shape
    d = g_b.shape[1]
    n_pairs = d // LANES
    return pl.pallas_call(
        functools.partial(_lora_b_kernel, d=d),
        grid=(n_batch, ltot // tm),
        in_specs=[
            pl.BlockSpec((3, None, tm, nh), lambda b, m: (0, b, m, 0)),
            pl.BlockSpec(w_b.shape, lambda b, m: (0, 0, 0)),
            pl.BlockSpec(a_b.shape, lambda b, m: (0, 0, 0)),
            pl.BlockSpec(g_b.shape, lambda b, m: (0, 0)),
            pl.BlockSpec(bias.shape, lambda b, m: (0, 0)),
        ],
        out_specs=pl.BlockSpec((5, None, n_pairs, tm, LANES), lambda b, m: (0, b, 0, m, 0)),
        out_shape=jax.ShapeDtypeStruct((5, n_batch, n_pairs, ltot, LANES), f32),
        compiler_params=_cparams(("parallel", "parallel")),
        name="rwkv_lora_b",
    )(h, w_b, a_b, g_b, bias)


def _seg_sum(x, lo):
    s_lo = jnp.sum(jnp.where(lo, x, 0.0), axis=1, keepdims=True)
    s_all = jnp.sum(x, axis=1, keepdims=True)
    return jnp.where(lo, s_lo, s_all - s_lo)


def _wkv_kernel(*refs, n_pairs, reverse, fused):
    if fused:
        (r_ref, k_ref, v_ref, w_ref, a_ref, par_ref, y0_ref, bon0_ref, gate_ref, ln_ref, og_ref, s_ref,
         lhs_s, rhs1_s, rhs2_s, vbd_s, dec_s, mab_s, mak_s, arbk_s, tinv_s, tmp_s, w_s, g1r_s, uv_s, bon_ref) = refs
    else:
        (r_ref, k_ref, v_ref, w_ref, a_ref, par_ref, y_ref, bon_ref, s_ref,
         lhs_s, rhs1_s, rhs2_s, vbd_s, dec_s, mab_s, mak_s, arbk_s, tinv_s, tmp_s, w_s, g1r_s, uv_s) = refs
    _wkv_body(r_ref, k_ref, v_ref, w_ref, a_ref, par_ref, bon_ref, s_ref,
              lhs_s, rhs1_s, rhs2_s, vbd_s, dec_s, mab_s, mak_s, arbk_s, tinv_s, tmp_s, w_s, g1r_s, uv_s,
              (y0_ref, bon0_ref, gate_ref, ln_ref, og_ref) if fused else (y_ref,),
              n_pairs=n_pairs, reverse=reverse)


def _wkv_body(r_ref, k_ref, v_ref, w_ref, a_ref, par_ref, bon_ref, s_ref,
              lhs_s, rhs1_s, rhs2_s, vbd_s, dec_s, mab_s, mak_s, arbk_s, tinv_s, tmp_s, w_s, g1r_s, uv_s,
              out_refs, *, n_pairs, reverse):
    c = pl.program_id(1)
    L = WKV_CHUNK
    P = 2 * L

    @pl.when(c == 0)
    def _():
        s_ref[...] = jnp.zeros_like(s_ref)

    lane = lax.broadcasted_iota(jnp.int32, (L, LANES), 1)
    lo = lane < RWKV_HEAD
    ri = lax.broadcasted_iota(jnp.int32, (P, P), 0)
    ci = lax.broadcasted_iota(jnp.int32, (P, P), 1)
    same = (ri // L) == (ci // L)
    ii, jj = ri % L, ci % L
    if reverse:
        strict, incl = same & (jj > ii), same & (jj >= ii)
    else:
        strict, incl = same & (jj < ii), same & (jj <= ii)
    eye = (ri == ci).astype(f32)
    ti = lax.broadcasted_iota(jnp.int32, (L, L), 0)
    tj = lax.broadcasted_iota(jnp.int32, (L, L), 1)
    tri = ((tj >= ti) if reverse else (tj <= ti)).astype(bf16)
    diag_p = ((lax.broadcasted_iota(jnp.int32, (P, LANES), 0) // L)
              == (lax.broadcasted_iota(jnp.int32, (P, LANES), 1) // RWKV_HEAD))

    def blockdiag(x):
        return jnp.where(diag_p, jnp.concatenate([x, x], axis=0), 0.0)

    def stack(x):
        return jnp.concatenate([x, x], axis=0)

    def prepare(p):
        r, k, v = r_ref[p], k_ref[p], v_ref[p]
        k_k = par_ref[p, 0:1, :]
        k_a = par_ref[p, 1:2, :]
        r_k = par_ref[p, 2:3, :]
        lw = -math.exp(-0.5) * _sigmoid(w_ref[p])
        a = _sigmoid(a_ref[p])
        kn = k * k_k
        kk = kn * lax.rsqrt(jnp.maximum(_seg_sum(kn * kn, lo), 1e-24))
        kd = k * (1.0 + (a - 1.0) * k_a)
        bon_ref[p] = _seg_sum(r * kd * r_k, lo) * v

        lw_hi = lw.astype(bf16)
        lw_lo = (lw - lw_hi.astype(f32)).astype(bf16)
        cum = (jnp.dot(tri, lw_hi, preferred_element_type=f32)
               + jnp.dot(tri, lw_lo, preferred_element_type=f32))
        tot = cum[0:1, :] if reverse else cum[L - 1:L, :]
        e_neg = jnp.exp(-cum)
        e_end = jnp.exp(tot - cum)
        b_in = kk * a
        at = -kk * jnp.exp(cum - lw)
        rt = r * jnp.exp(cum)
        lhs_s[p] = jnp.concatenate([blockdiag(at), blockdiag(rt)], axis=0).astype(bf16)
        rhs1_s[p] = jnp.concatenate([stack(b_in * e_neg), stack(kd * e_neg)], axis=0).astype(bf16)
        rhs2_s[p] = jnp.concatenate([stack(b_in * e_end), stack(kd * e_end)], axis=0).astype(bf16)
        vbd_s[p] = blockdiag(v).astype(bf16)
        dec_s[p] = jnp.broadcast_to(jnp.exp(tot), (SUBLANES, LANES))

    def scores(p):
        sc = lax.dot_general(lhs_s[p], rhs1_s[p], NT_DIMS, preferred_element_type=f32)
        m_ab = jnp.where(strict, sc[0:P, 0:P], 0.0)
        mab_s[p] = m_ab.astype(bf16)
        mak_s[p] = jnp.where(strict, sc[0:P, P:2 * P], 0.0).astype(bf16)
        arbk_s[p, :, 0:P] = jnp.where(incl, sc[P:2 * P, 0:P], 0.0).astype(bf16)
        arbk_s[p, :, P:2 * P] = jnp.where(incl, sc[P:2 * P, P:2 * P], 0.0).astype(bf16)
        tinv_s[p] = eye + jnp.where((ri // 2) == (ci // 2), m_ab, 0.0)

    def invert(pairs):
        s = 4
        while s <= L:
            level = ((ri // s) == (ci // s)) & ((ri // (s // 2)) != (ci // (s // 2)))
            for p in pairs:
                e = jnp.where(level, mab_s[p], jnp.zeros((P, P), bf16))
                tmp_s[p] = _dot(tinv_s[p], e).astype(bf16)
            for p in pairs:
                t_inv = tinv_s[p]
                tinv_s[p] = t_inv + _dot(tmp_s[p], t_inv)
            s *= 2

    def read_state(p):
        g1 = _dot(lhs_s[p], s_ref[p], NT_DIMS)
        z = jnp.dot(mak_s[p], vbd_s[p], preferred_element_type=f32)
        w_s[p] = (g1[0:P] + z).astype(bf16)
        g1r_s[p] = g1[P:2 * P]

    def solve(p):
        uv_s[p, 0:P, :] = _dot(tinv_s[p], w_s[p]).astype(bf16)
        uv_s[p, P:2 * P, :] = vbd_s[p]

    def emit(p):
        y = g1r_s[p] + jnp.dot(arbk_s[p], uv_s[p], preferred_element_type=f32)
        y = y[0:L] + y[L:P]
        if len(out_refs) == 1:
            out_refs[0][p] = y
        else:
            y0_ref, bon0_ref, gate_ref, ln_ref, og_ref = out_refs
            y = y + y0_ref[p]
            mean = _seg_sum(y, lo) * (1.0 / RWKV_HEAD)
            yc = y - mean
            var = _seg_sum(yc * yc, lo) * (1.0 / RWKV_HEAD)
            sl = slice(p * LANES, (p + 1) * LANES)
            o = yc * lax.rsqrt(var + LN_X_EPS) * ln_ref[0:1, sl] + ln_ref[1:2, sl] + bon0_ref[p] + bon_ref[p]
            og_ref[:, sl] = (o * gate_ref[p]).astype(bf16)
        upd = lax.dot_general(uv_s[p], rhs2_s[p], TN_DIMS, preferred_element_type=f32)
        s_ref[p] = s_ref[p] * dec_s[p, 0:1, :] + jnp.where(diag_p, upd, 0.0)

    def finish(pairs):
        for stage in (read_state, solve, emit):
            for p in pairs:
                stage(p)

    pairs = range(n_pairs)
    for p in pairs:
        prepare(p)
    for p in pairs:
        scores(p)
    invert(pairs)
    finish(pairs)


def _wkv_call(rkv, pre, params, ctx_len, direction, first=None, ln_x=None):
    _, n_batch, n_pairs, ltot, _ = rkv.shape
    fused = first is not None
    L = WKV_CHUNK
    P = 2 * L
    n_chunks = ltot // L
    ctx_chunks = ctx_len // L
    reverse = direction == 1

    def chunk_of(c):
        if not reverse:
            return c
        return jnp.where(c < ctx_chunks, ctx_chunks - 1 - c, n_chunks - 1 - (c - ctx_chunks))

    def spec(lead):
        return pl.BlockSpec((None, None, n_pairs, L, LANES), lambda b, c: (lead, b, 0, chunk_of(c), 0))

    pair_spec = pl.BlockSpec((None, n_pairs, L, LANES), lambda b, c: (b, 0, chunk_of(c), 0))
    pair_sds = jax.ShapeDtypeStruct((n_batch, n_pairs, ltot, LANES), f32)
    in_specs = [spec(0), spec(1), spec(2),
                pl.BlockSpec((None, None, n_pairs, L, LANES), lambda b, c: (direction, b, 0, chunk_of(c), 0)),
                pl.BlockSpec((None, None, n_pairs, L, LANES), lambda b, c: (2 + direction, b, 0, chunk_of(c), 0)),
                pl.BlockSpec((None, n_pairs, SUBLANES, LANES), lambda b, c: (direction, 0, 0, 0))]
    operands = [rkv, rkv, rkv, pre, pre, params]
    extra_scratch = []
    if fused:
        d = n_pairs * LANES
        in_specs += [pair_spec, pair_spec,
                     pl.BlockSpec((None, None, n_pairs, L, LANES), lambda b, c: (4, b, 0, chunk_of(c), 0)),
                     pl.BlockSpec((SUBLANES, d), lambda b, c: (0, 0))]
        operands += [first[0], first[1], pre, jnp.pad(ln_x, ((0, SUBLANES - ln_x.shape[0]), (0, 0)))]
        out_specs = pl.BlockSpec((None, L, d), lambda b, c: (b, chunk_of(c), 0))
        out_shape = jax.ShapeDtypeStruct((n_batch, ltot, d), bf16)
        extra_scratch = [pltpu.VMEM((n_pairs, L, LANES), f32)]
    else:
        out_specs = [pair_spec, pair_spec]
        out_shape = [pair_sds, pair_sds]
    return pl.pallas_call(
        functools.partial(_wkv_kernel, n_pairs=n_pairs, reverse=reverse, fused=fused),
        grid=(n_batch, n_chunks),
        in_specs=in_specs,
        out_specs=out_specs,
        out_shape=out_shape,
        scratch_shapes=[
            pltpu.VMEM((n_pairs, P, LANES), f32),
            pltpu.VMEM((n_pairs, 2 * P, LANES), bf16),
            pltpu.VMEM((n_pairs, 2 * P, LANES), bf16),
            pltpu.VMEM((n_pairs, 2 * P, LANES), bf16),
            pltpu.VMEM((n_pairs, P, LANES), bf16),
            pltpu.VMEM((n_pairs, SUBLANES, LANES), f32),
            pltpu.VMEM((n_pairs, P, P), bf16),
            pltpu.VMEM((n_pairs, P, P), bf16),
            pltpu.VMEM((n_pairs, P, 2 * P), bf16),
            pltpu.VMEM((n_pairs, P, P), f32),
            pltpu.VMEM((n_pairs, P, P), bf16),
            pltpu.VMEM((n_pairs, P, LANES), bf16),
            pltpu.VMEM((n_pairs, P, LANES), f32),
            pltpu.VMEM((n_pairs, 2 * P, LANES), bf16),
        ] + extra_scratch,
        compiler_params=_cparams(("parallel", "arbitrary")),
        name="rwkv_wkv_fwd" if not reverse else "rwkv_wkv_bwd",
    )(*operands)


def _rwkv_layer(z, mods, norm_g, ctx_len, mu, w_rkv, w_o, dir_vec, w_la, w_lb, a_la, a_lb, g_la, g_lb, r_k, ln_x):
    n_batch, ltot, d = z.shape
    n_pairs = d // LANES
    tm = 768 if ltot % 768 == 0 else 512
    mixes = _rwkv_mix_call(z, mods, norm_g, mu, ctx_len)
    rkv = _rkv_call(mixes, w_rkv.astype(bf16), tm)

    def pad_cols(w):
        return jnp.concatenate([jnp.pad(w[i], ((0, 0), (0, LANES - w.shape[2]))) for i in range(2)], axis=1)

    def pad_rows(w):
        return jnp.pad(w, ((0, 0), (0, LANES - w.shape[1]), (0, 0)))

    w_a = jnp.stack([pad_cols(w_la), pad_cols(a_la), g_la]).astype(bf16)
    h = _lora_a_call(mixes, w_a, tm)
    bias = jnp.pad(jnp.stack([dir_vec[0, 0], dir_vec[1, 0], dir_vec[0, 1], dir_vec[1, 1]]), ((0, 4), (0, 0)))
    pre = _lora_b_call(h, pad_rows(w_lb).astype(bf16), pad_rows(a_lb).astype(bf16), g_lb.astype(bf16), bias, 256)
    rk_row = r_k.reshape(d)
    params = jnp.stack([jnp.stack([dir_vec[di, 2], dir_vec[di, 3], rk_row]) for di in range(2)])
    params = jnp.pad(params, ((0, 0), (0, SUBLANES - 3), (0, 0)))
    params = params.reshape(2, SUBLANES, n_pairs, LANES).transpose(0, 2, 1, 3)
    first = _wkv_call(rkv, pre, params, ctx_len, 0)
    og = _wkv_call(rkv, pre, params, ctx_len, 1, first=first, ln_x=ln_x)
    return _mm_resid_call(og, w_o.astype(bf16), z, mods, ctx_len, tm, chunk=2)


def _rope_tables(ctx_len, seq_len):
    rows = seq_len // GRID_W
    row = np.repeat(np.arange(rows, dtype=np.float32), GRID_W)
    col = np.tile(np.arange(GRID_W, dtype=np.float32), rows)
    n_freq = DIFF_HEAD // 4
    inv = (np.float32(ROPE_BASE) ** (-np.arange(n_freq, dtype=np.float32) / np.float32(n_freq))).astype(np.float32)
    ang = np.concatenate([row[:, None] * inv, col[:, None] * inv], axis=-1).astype(np.float32)
    cos = np.repeat(np.cos(ang), 2, axis=1)
    sin = np.repeat(np.sin(ang), 2, axis=1) * np.tile(np.array([-1.0, 1.0], np.float32), DIFF_HEAD // 2)
    cos = np.concatenate([np.ones((ctx_len, DIFF_HEAD), np.float32), cos], axis=0)
    sin = np.concatenate([np.zeros((ctx_len, DIFF_HEAD), np.float32), sin], axis=0)
    return jnp.asarray(cos, f32), jnp.asarray(sin, f32)


def _slab_products(x_ref, w_ref, tm, consume):
    def product(c):
        return jnp.dot(x_ref[c * ROW_CHUNK:(c + 1) * ROW_CHUNK, :], w_ref[...], preferred_element_type=f32)

    n_slabs = tm // ROW_CHUNK
    nxt = product(0)
    for c in range(n_slabs):
        y = nxt
        if c + 1 < n_slabs:
            nxt = product(c + 1)
        consume(c, y)


def _qk_kernel(mods_ref, g_ref, qkg_ref, cos_ref, sin_ref, z_ref, w_ref, o_ref, h_ref,
               *, tm, tn, d, ctx_len, n_batch):
    b = pl.program_id(0)
    m = pl.program_id(1)
    n = pl.program_id(2)

    @pl.when(n == 0)
    def _():
        _norm_mod_tile(z_ref, h_ref, mods_ref, g_ref, 0, b, m * tm, tm, d, ctx_len, n_batch)

    is_q = n < d // tn
    gain = qkg_ref[pl.ds(jnp.where(is_q, 0, 1), 1), :]
    scale = jnp.where(is_q, DIFF_HEAD ** -0.5 * LOG2E, 1.0)
    even = (lax.broadcasted_iota(jnp.int32, (ROW_CHUNK, LANES), 1) % 2) == 0

    def epilogue(c, y):
        rows = slice(c * ROW_CHUNK, (c + 1) * ROW_CHUNK)
        cos = cos_ref[rows, :]
        sin = sin_ref[rows, :]
        for gi in range(tn // LANES):
            x = y[:, gi * LANES:(gi + 1) * LANES]
            ms = jnp.mean(x * x, axis=-1, keepdims=True)
            x = x * (lax.rsqrt(ms + EPS) * scale) * gain
            partner = jnp.where(even, pltpu.roll(x, LANES - 1, 1), pltpu.roll(x, 1, 1))
            o_ref[rows, gi * LANES:(gi + 1) * LANES] = (x * cos + partner * sin).astype(bf16)

    _slab_products(h_ref, w_ref, tm, epilogue)


def _qk_call(z, mods, norm_g, w_qk, qk_g, cos, sin, ctx_len, tm):
    n_batch, ltot, d = z.shape
    tn = d
    kern = functools.partial(_qk_kernel, tm=tm, tn=tn, d=d, ctx_len=ctx_len, n_batch=n_batch)
    return pl.pallas_call(
        kern,
        grid=(n_batch, ltot // tm, 2 * d // tn),
        in_specs=[
            pl.BlockSpec(mods.shape, lambda b, m, n: (0, 0)),
            pl.BlockSpec((1, d), lambda b, m, n: (0, 0)),
            pl.BlockSpec((SUBLANES, DIFF_HEAD), lambda b, m, n: (0, 0)),
            pl.BlockSpec((tm, DIFF_HEAD), lambda b, m, n: (m, 0)),
            pl.BlockSpec((tm, DIFF_HEAD), lambda b, m, n: (m, 0)),
            pl.BlockSpec((None, tm, d), lambda b, m, n: (b, m, 0)),
            pl.BlockSpec((d, tn), lambda b, m, n: (0, n)),
        ],
        out_specs=[pl.BlockSpec((None, tm, tn), lambda b, m, n: (b, m, n)),
                   pl.BlockSpec((None, tm, d), lambda b, m, n: (b, m, 0))],
        out_shape=[jax.ShapeDtypeStruct((n_batch, ltot, 2 * d), bf16),
                   jax.ShapeDtypeStruct((n_batch, ltot, d), bf16)],
        compiler_params=_cparams(("parallel", "parallel", "arbitrary")),
        name="diff_qk_proj",
    )(mods, norm_g.reshape(1, d), jnp.pad(qk_g, ((0, SUBLANES - qk_g.shape[0]), (0, 0))), cos, sin, z, w_qk)


def _vt_kernel(h_ref, w_ref, o_ref, *, tm):
    def store(c, y):
        o_ref[:, c * ROW_CHUNK:(c + 1) * ROW_CHUNK] = y.astype(bf16).T

    _slab_products(h_ref, w_ref, tm, store)


def _vt_call(h, w_qkv, tm):
    n_batch, ltot, d = h.shape
    tn = d
    v_off = 2 * d // tn
    return pl.pallas_call(
        functools.partial(_vt_kernel, tm=tm),
        grid=(n_batch, ltot // tm, d // tn),
        in_specs=[pl.BlockSpec((None, tm, d), lambda b, m, n: (b, m, 0)),
                  pl.BlockSpec((d, tn), lambda b, m, n: (0, v_off + n))],
        out_specs=pl.BlockSpec((None, None, tn, tm), lambda b, m, n: (b, m, n, 0)),
        out_shape=jax.ShapeDtypeStruct((n_batch, ltot // tm, d, tm), bf16),
        compiler_params=_cparams(("parallel", "parallel", "arbitrary")),
        name="diff_v_proj",
    )(h, w_qkv)


def _attn_kernel(lam_ref, sg_ref, q_ref, k_ref, vt_ref, o_ref, sa_s, sb_s, *, lambda_init, ctx_len):
    tq = o_ref.shape[0]
    n_k, _, tk = vt_ref.shape
    hd = DIFF_HEAD
    q_row0 = pl.multiple_of(ctx_len + pl.program_id(2) * tq, 256)
    qt = q_ref[pl.ds(q_row0, tq), :].astype(f32).T
    top = lax.broadcasted_iota(jnp.int32, qt.shape, 0) < hd
    w = jnp.concatenate([jnp.where(top, qt, 0.0), jnp.where(top, 0.0, qt)], axis=1).astype(bf16)
    groups = [(j0, min(ATTN_GROUP, n_k - j0)) for j0 in range(0, n_k, ATTN_GROUP)]
    bufs = (sa_s, sb_s)

    def scores(g):
        j0, nb = groups[g]
        bufs[g % 2][0:nb * tk, :] = jnp.dot(k_ref[j0 * tk:(j0 + nb) * tk, :], w, preferred_element_type=f32)

    def absorb(g, m_old, l_old, acc):
        j0, nb = groups[g]
        s = bufs[g % 2][0:nb * tk, :]
        m_new = jnp.maximum(m_old, jnp.max(s, axis=0, keepdims=True))
        alpha = jnp.exp2(m_old - m_new)
        p = jnp.exp2(s - m_new)
        l_new = alpha * l_old + jnp.sum(p, axis=0, keepdims=True)
        vt = jnp.concatenate([vt_ref[j0 + i] for i in range(nb)], axis=1) if nb > 1 else vt_ref[j0]
        return m_new, l_new, alpha * acc + jnp.dot(vt, p.astype(bf16), preferred_element_type=f32)

    m = jnp.full((1, 2 * tq), -jnp.inf, f32)
    l = jnp.zeros((1, 2 * tq), f32)
    acc = jnp.zeros((2 * hd, 2 * tq), f32)
    scores(0)
    for g in range(len(groups)):
        if g + 1 < len(groups):
            scores(g + 1)
        m, l, acc = absorb(g, m, l, acc)
    lv = lam_ref[...]
    lam = (jnp.exp(jnp.sum(lv[0:1] * lv[1:2], axis=1, keepdims=True))
           - jnp.exp(jnp.sum(lv[2:3] * lv[3:4], axis=1, keepdims=True)) + lambda_init)
    o = acc[:, 0:tq] / l[:, 0:tq] - lam * (acc[:, tq:] / l[:, tq:])
    ms = jnp.mean(o * o, axis=0, keepdims=True)
    o = o * lax.rsqrt(ms + EPS) * sg_ref[...]
    o_ref[...] = o.T.astype(bf16)


def _attn_call(qk, vt, lam_vec, subln_g, ctx_len, lambda_init):
    n_batch, ltot, d2 = qk.shape
    d = d2 // 2
    _, n_kb, _, tk = vt.shape
    hw = 2 * DIFF_HEAD
    n_heads = d // hw
    seq_len = ltot - ctx_len
    tq = 512
    assert ctx_len % 256 == 0 and seq_len % tq == 0
    gain = jnp.broadcast_to((subln_g * (1.0 - lambda_init)).reshape(hw, 1), (hw, tq))
    kern = functools.partial(_attn_kernel, lambda_init=lambda_init, ctx_len=ctx_len)
    return pl.pallas_call(
        kern,
        grid=(n_batch, n_heads, seq_len // tq),
        in_specs=[
            pl.BlockSpec((SUBLANES, DIFF_HEAD), lambda b, h, t: (0, 0)),
            pl.BlockSpec((hw, tq), lambda b, h, t: (0, 0)),
            pl.BlockSpec((None, ltot, hw), lambda b, h, t: (b, 0, h)),
            pl.BlockSpec((None, ltot, hw), lambda b, h, t: (b, 0, n_heads + h)),
            pl.BlockSpec((None, n_kb, hw, tk), lambda b, h, t: (b, 0, h, 0)),
        ],
        out_specs=pl.BlockSpec((None, tq, hw), lambda b, h, t: (b, t, h)),
        out_shape=jax.ShapeDtypeStruct((n_batch, seq_len, d), bf16),
        scratch_shapes=[pltpu.VMEM((min(ATTN_GROUP, n_kb) * tk, 2 * tq), f32),
                        pltpu.VMEM((min(ATTN_GROUP, n_kb) * tk, 2 * tq), f32)],
        compiler_params=_cparams(("parallel", "parallel", "parallel")),
        name="diff_attention",
    )(jnp.pad(lam_vec, ((0, SUBLANES - lam_vec.shape[0]), (0, 0))), gain, qk, qk, vt)


def _diff_layer(z, mods, norm_g, ctx_len, w_qkv, w_o, qk_g, lam_vec, subln_g, lambda_init):
    n_batch, ltot, d = z.shape
    seq_len = ltot - ctx_len
    tm = 768 if ltot % 768 == 0 else 512
    cos, sin = _rope_tables(ctx_len, seq_len)
    w_bf = w_qkv.astype(bf16)
    qk, h = _qk_call(z, mods, norm_g, w_bf, qk_g, cos, sin, ctx_len, tm)
    vt = _vt_call(h, w_bf, tm)
    o = _attn_call(qk, vt, lam_vec, subln_g, ctx_len, lambda_init)
    return _mm_resid_call(o, w_o.astype(bf16), z, mods, 0, 256, chunk=2, row_off=ctx_len)


def kernel(x, c, ctx, c_ctx, ada_w, ada_b, norm_g, ffn_w_in, ffn_w_out, pool_w, pool_scale, rwkv_mu, rwkv_w_rkv, rwkv_w_o, rwkv_dir_vec, rwkv_w_lora_a, rwkv_w_lora_b, rwkv_a_lora_a, rwkv_a_lora_b, rwkv_g_lora_a, rwkv_g_lora_b, rwkv_r_k, rwkv_ln_x, diff_w_qkv, diff_w_o, diff_qk_g, diff_lambda, diff_subln_g):
    n_batch, seq_len, d = x.shape
    depth = ada_w.shape[0]
    ctx_len = ctx.shape[1]
    assert n_batch + 1 <= SUBLANES
    cs = jnp.concatenate([c, c_ctx[None, :], jnp.zeros((SUBLANES - n_batch - 1, d), f32)], axis=0)
    mods_all = _mods_call(cs, ada_w, ada_b)

    last_reader = max((i for i in range(depth) if i % N_MIXERS != 0), default=-1)
    cur_ctx = ctx_len if last_reader >= 0 else 0
    join_in_pool = cur_ctx == 256 and depth > 0
    z = x if (cur_ctx == 0 or join_in_pool) else jnp.concatenate([ctx, x], axis=1)
    w_in_all = ffn_w_in.astype(bf16)
    w_out_all = ffn_w_out.astype(bf16)
    for i in range(depth):
        kind, j = i % N_MIXERS, i // N_MIXERS
        mods = mods_all[i]
        if kind == 0:
            z = _pool_call(z, mods, norm_g[i, 0], pool_w[j], pool_scale[j], cur_ctx,
                           ctx=ctx if (i == 0 and join_in_pool) else None)
        elif kind == 1:
            z = _rwkv_layer(z, mods, norm_g[i, 0], cur_ctx, rwkv_mu[j], rwkv_w_rkv[j], rwkv_w_o[j], rwkv_dir_vec[j],
                            rwkv_w_lora_a[j], rwkv_w_lora_b[j], rwkv_a_lora_a[j], rwkv_a_lora_b[j],
                            rwkv_g_lora_a[j], rwkv_g_lora_b[j], rwkv_r_k[j], rwkv_ln_x[j])
        else:
            lambda_init = 0.8 - 0.6 * math.exp(-0.3 * i)
            z = _diff_layer(z, mods, norm_g[i, 0], cur_ctx, diff_w_qkv[j], diff_w_o[j], diff_qk_g[j],
                            diff_lambda[j], diff_subln_g[j], lambda_init)
            cur_ctx = 0
        if cur_ctx and i >= last_reader:
            z = z[:, cur_ctx:]
            cur_ctx = 0
        ltot = z.shape[1]
        tm = 768 if ltot % 768 == 0 else 512
        z = _ffn_call(z, mods, norm_g[i, 1], w_in_all, w_out_all, i, cur_ctx, tm)
    return z[:, cur_ctx:] if cur_ctx else z
```
